```python
import jax, jax.numpy as jnp
from jax import lax
import numpy as np

D_MODEL = 1024
BATCH = 8
SEQ = 2048
DEPTH = 4

D_FF = 2816
N_EVEN = (DEPTH + 1) // 2
N_ODD = DEPTH // 2
N_ADA = 9
EPS = 1e-6
NEG_INF = -1e30
Q_BLOCK = 128

CONV_WIDTH = 512
CONV_GROUPS = 8
CONV_TAPS = 3
MLA_HEADS = 8
MLA_NOPE = 64
MLA_ROPE = 32
MLA_V = 64
Q_LORA = 256
KV_LORA = 128
ROPE_THETA = 10000.0
HY_IN = 3 * CONV_WIDTH + Q_LORA + KV_LORA + MLA_ROPE
HY_MIX = CONV_WIDTH + MLA_HEADS * MLA_V
NSA_HEADS = 16
NSA_KV_HEADS = 2
NSA_GROUP = NSA_HEADS // NSA_KV_HEADS
NSA_DK = 64
CMP_BLOCK = 32
CMP_STRIDE = 16
CMP_HID = 128
SLC_BLOCK = 64
N_SEL = 8
WINDOW = 512
FORCE_SCORE = 1e4
NSA_KV_W = 2 * NSA_KV_HEADS * NSA_DK
NSA_IN = NSA_HEADS * NSA_DK + 3 * NSA_KV_W + 3 * NSA_HEADS

kernel_name = "hybrid_conv_mla_nsa_macaron_adaln"


def rms_norm(x, g):
    xf = x.astype(jnp.float32)
    y = xf * lax.rsqrt(jnp.mean(xf * xf, axis=-1, keepdims=True) + EPS)
    return (y * g.astype(jnp.float32)).astype(x.dtype)


def modulate(x, g, shift, scale):
    return rms_norm(x, g) * (1 + scale[:, None]) + shift[:, None]


def swiglu(h, w13, w2):
    a, b = jnp.split(h @ w13, 2, axis=-1)
    return (jax.nn.silu(a) * b) @ w2


def masked_softmax(s, mask):
    s = jnp.where(mask, s.astype(jnp.float32), NEG_INF)
    m = jnp.max(s, axis=-1, keepdims=True)
    p = jnp.exp(s - m) * mask
    return p / jnp.maximum(jnp.sum(p, axis=-1, keepdims=True), 1e-20)


def rope_tables(positions):
    half = MLA_ROPE // 2
    inv = ROPE_THETA ** (-jnp.arange(half, dtype=jnp.float32) / half)
    ang = positions.astype(jnp.float32)[..., None] * inv
    return jnp.cos(ang), jnp.sin(ang)


def apply_rope(t, cos, sin):
    half = t.shape[-1] // 2
    t1 = t[..., :half].astype(jnp.float32)
    t2 = t[..., half:].astype(jnp.float32)
    return jnp.concatenate([t1 * cos - t2 * sin, t1 * sin + t2 * cos], axis=-1).astype(t.dtype)


def to_blocks(t, nqb):
    return jnp.moveaxis(t.reshape(t.shape[0], nqb, Q_BLOCK, *t.shape[2:]), 1, 0)


def from_blocks(o):
    o = jnp.moveaxis(o, 0, 1)
    return o.reshape(o.shape[0], o.shape[1] * o.shape[2], *o.shape[3:])


def short_conv(u, conv_w):
    ch = u.shape[-1]
    return lax.conv_general_dilated(
        u, conv_w[:, None, :], window_strides=(1,), padding=((CONV_TAPS - 1, 0),),
        dimension_numbers=('NWC', 'WIO', 'NWC'), feature_group_count=ch)


def mla_attention(q_nope, q_rope, k_nope, k_rope, v):
    S = q_nope.shape[1]
    nqb = S // Q_BLOCK
    scale = (MLA_NOPE + MLA_ROPE) ** -0.5
    kpos = jnp.arange(S)

    def block(args):
        i, qn, qr = args
        s = (jnp.einsum('bqhd,bkhd->bhqk', qn, k_nope)
             + jnp.einsum('bqhd,bkd->bhqk', qr, k_rope)) * scale
        qpos = i * Q_BLOCK + jnp.arange(Q_BLOCK)
        p = masked_softmax(s, kpos[None, :] <= qpos[:, None])
        return jnp.einsum('bhqk,bkhd->bqhd', p.astype(v.dtype), v)

    o = lax.map(block, (jnp.arange(nqb), to_blocks(q_nope, nqb), to_blocks(q_rope, nqb)))
    return from_blocks(o)


def hybrid_conv_mla(h, cos, sin, w_in, conv_w, q_norm, kv_norm, w_uq, w_ukv, w_out):
    B, S, _ = h.shape
    z = h @ w_in
    cw = CONV_WIDTH
    u, gate_c, gate_b, cq, ckv, kr = jnp.split(
        z, [cw, 2 * cw, 3 * cw, 3 * cw + Q_LORA, 3 * cw + Q_LORA + KV_LORA], axis=-1)
    y_conv = gate_b * short_conv(gate_c * u, conv_w)
    q = (rms_norm(cq, q_norm) @ w_uq).reshape(B, S, MLA_HEADS, MLA_NOPE + MLA_ROPE)
    q_nope, q_rope = q[..., :MLA_NOPE], apply_rope(q[..., MLA_NOPE:], cos[:, :, None], sin[:, :, None])
    kv = (rms_norm(ckv, kv_norm) @ w_ukv).reshape(B, S, MLA_HEADS, MLA_NOPE + MLA_V)
    k_nope, v = kv[..., :MLA_NOPE], kv[..., MLA_NOPE:]
    k_rope = apply_rope(kr, cos, sin)
    y_att = mla_attention(q_nope, q_rope, k_nope, k_rope, v).reshape(B, S, MLA_HEADS * MLA_V)
    return jnp.concatenate([y_conv, y_att], axis=-1) @ w_out


def nsa_attention(h, w_in, cmp_pe, cmp_w1, cmp_w2, gate_b, w_out):
    B, S, _ = h.shape
    G, R, DK = NSA_KV_HEADS, NSA_GROUP, NSA_DK
    dt = h.dtype
    z = h @ w_in
    qd = NSA_HEADS * DK
    q, kv_c, kv_s, kv_w, g = jnp.split(
        z, [qd, qd + NSA_KV_W, qd + 2 * NSA_KV_W, qd + 3 * NSA_KV_W], axis=-1)
    q = q.reshape(B, S, G, R, DK)
    gates = jax.nn.sigmoid(g + gate_b).reshape(B, S, G, R, 3)
    kv_c = kv_c.reshape(B, S, 2, G, DK)
    kv_s = kv_s.reshape(B, S, 2, G, DK)
    kv_w = kv_w.reshape(B, S, 2, G, DK)

    nc = (S - CMP_BLOCK) // CMP_STRIDE + 1
    cidx = jnp.arange(nc)[:, None] * CMP_STRIDE + jnp.arange(CMP_BLOCK)[None, :]

    def compress(t, pe, w1, w2):
        blk = t[:, cidx] + pe[:, None, :]
        blk = jnp.moveaxis(blk, 3, 2).reshape(B, nc, G, CMP_BLOCK * DK)
        return jax.nn.silu(blk @ w1) @ w2

    k_cmp = compress(kv_c[:, :, 0], cmp_pe[0], cmp_w1[0], cmp_w2[0])
    v_cmp = compress(kv_c[:, :, 1], cmp_pe[1], cmp_w1[1], cmp_w2[1])
    cmp_end = jnp.arange(nc) * CMP_STRIDE + CMP_BLOCK - 1

    ns = S // SLC_BLOCK
    n_sel = min(N_SEL, ns)
    cs = jnp.arange(nc)[:, None] * CMP_STRIDE
    ss = jnp.arange(ns)[None, :] * SLC_BLOCK
    overlap = jnp.clip(jnp.minimum(cs + CMP_BLOCK, ss + SLC_BLOCK) - jnp.maximum(cs, ss), 0, None)
    agg = overlap.astype(jnp.float32) / CMP_BLOCK
    blk_idx = jnp.arange(ns)
    k_slc = kv_s[:, :, 0].reshape(B, ns, SLC_BLOCK, G, DK).transpose(0, 3, 1, 2, 4)
    v_slc = kv_s[:, :, 1].reshape(B, ns, SLC_BLOCK, G, DK).transpose(0, 3, 1, 2, 4)
    bi = jnp.arange(B)[:, None, None, None]
    gi = jnp.arange(G)[None, :, None, None]

    pad = ((0, 0), (WINDOW, 0), (0, 0), (0, 0))
    k_win = jnp.pad(kv_w[:, :, 0], pad)
    v_win = jnp.pad(kv_w[:, :, 1], pad)
    scale = DK ** -0.5
    nqb = S // Q_BLOCK

    def block(args):
        i, qb, gb = args
        t = i * Q_BLOCK + jnp.arange(Q_BLOCK)
        s_c = jnp.einsum('bqgrd,bngd->bgrqn', qb, k_cmp) * scale
        p_c = masked_softmax(s_c, cmp_end[None, :] <= t[:, None])
        o_c = jnp.einsum('bgrqn,bngd->bqgrd', p_c.astype(dt), v_cmp)
        imp = jnp.einsum('bgrqn,nj->bgqj', p_c, agg)
        cur = t // SLC_BLOCK
        forced = ((blk_idx[None, :] == 0) | (blk_idx[None, :] == cur[:, None])
                  | (blk_idx[None, :] == cur[:, None] - 1))
        causal_blk = blk_idx[None, :] * SLC_BLOCK <= t[:, None]
        imp = jnp.where(forced, FORCE_SCORE, jnp.where(causal_blk, imp, NEG_INF))
        _, sel = lax.top_k(imp, n_sel)
        ks = k_slc[bi, gi, sel].reshape(B, G, Q_BLOCK, n_sel * SLC_BLOCK, DK)
        vs = v_slc[bi, gi, sel].reshape(B, G, Q_BLOCK, n_sel * SLC_BLOCK, DK)
        kpos_s = (sel[..., None] * SLC_BLOCK + jnp.arange(SLC_BLOCK)).reshape(B, G, Q_BLOCK, -1)
        s_s = jnp.einsum('bqgrd,bgqkd->bgrqk', qb, ks) * scale
        p_s = masked_softmax(s_s, kpos_s[:, :, None] <= t[:, None])
        o_s = jnp.einsum('bgrqk,bgqkd->bqgrd', p_s.astype(dt), vs)
        kw = lax.dynamic_slice_in_dim(k_win, i * Q_BLOCK, WINDOW + Q_BLOCK, axis=1)
        vw = lax.dynamic_slice_in_dim(v_win, i * Q_BLOCK, WINDOW + Q_BLOCK, axis=1)
        kpos_w = i * Q_BLOCK - WINDOW + jnp.arange(WINDOW + Q_BLOCK)
        mask_w = ((kpos_w[None, :] <= t[:, None]) & (kpos_w[None, :] > t[:, None] - WINDOW)
                  & (kpos_w[None, :] >= 0))
        s_w = jnp.einsum('bqgrd,bkgd->bgrqk', qb, kw) * scale
        p_w = masked_softmax(s_w, mask_w)
        o_w = jnp.einsum('bgrqk,bkgd->bqgrd', p_w.astype(dt), vw)
        return gb[..., 0:1] * o_c + gb[..., 1:2] * o_s + gb[..., 2:3] * o_w

    o = lax.map(block, (jnp.arange(nqb), to_blocks(q, nqb), to_blocks(gates, nqb)))
    return from_blocks(o).reshape(B, S, NSA_HEADS * DK) @ w_out


def setup_inputs(seed: int = 0) -> dict:
    key = jax.random.key(seed)
    ks = jax.random.split(key, 24)
    nrm = lambda k, shape, s: jax.random.normal(k, shape, jnp.float32) * s
    D = D_MODEL
    return {
        "x": nrm(ks[0], (BATCH, SEQ, D), 1.0),
        "c": nrm(ks[1], (BATCH, D), 1.0),
        "positions": (jnp.arange(SEQ, dtype=jnp.int32)[None, :]
                      + jax.random.randint(ks[2], (BATCH, 1), 0, 1024, dtype=jnp.int32)),
        "ada_w": nrm(ks[3], (DEPTH, D, N_ADA * D), 0.5 * D ** -0.5),
        "ada_b": nrm(ks[4], (DEPTH, N_ADA * D), 0.02),
        "norm_g": 1.0 + nrm(ks[5], (DEPTH, 3, D), 0.02),
        "final_g": 1.0 + nrm(ks[6], (D,), 0.02),
        "ff_w13": nrm(ks[7], (DEPTH, 2, D, 2 * D_FF), D ** -0.5),
        "ff_w2": nrm(ks[8], (DEPTH, 2, D_FF, D), D_FF ** -0.5),
        "hy_w_in": nrm(ks[9], (N_EVEN, D, HY_IN), D ** -0.5),
        "hy_conv_w": nrm(ks[10], (N_EVEN, CONV_TAPS, CONV_WIDTH), CONV_TAPS ** -0.5),
        "hy_q_norm": 1.0 + nrm(ks[11], (N_EVEN, Q_LORA), 0.02),
        "hy_kv_norm": 1.0 + nrm(ks[12], (N_EVEN, KV_LORA), 0.02),
        "hy_w_uq": nrm(ks[13], (N_EVEN, Q_LORA, MLA_HEADS * (MLA_NOPE + MLA_ROPE)), Q_LORA ** -0.5),
        "hy_w_ukv": nrm(ks[14], (N_EVEN, KV_LORA, MLA_HEADS * (MLA_NOPE + MLA_V)), KV_LORA ** -0.5),
        "hy_w_out": nrm(ks[15], (N_EVEN, HY_MIX, D), HY_MIX ** -0.5),
        "nsa_w_in": nrm(ks[16], (N_ODD, D, NSA_IN), D ** -0.5),
        "nsa_cmp_pe": nrm(ks[17], (N_ODD, 2, CMP_BLOCK, NSA_DK), 0.1),
        "nsa_cmp_w1": nrm(ks[18], (N_ODD, 2, CMP_BLOCK * NSA_DK, CMP_HID), (CMP_BLOCK * NSA_DK) ** -0.5),
        "nsa_cmp_w2": nrm(ks[19], (N_ODD, 2, CMP_HID, NSA_DK), CMP_HID ** -0.5),
        "nsa_gate_b": nrm(ks[20], (N_ODD, 3 * NSA_HEADS), 0.1),
        "nsa_w_out": nrm(ks[21], (N_ODD, NSA_HEADS * NSA_DK, D), (NSA_HEADS * NSA_DK) ** -0.5),
    }


def reference(x, c, positions, ada_w, ada_b, norm_g, final_g, ff_w13, ff_w2,
              hy_w_in, hy_conv_w, hy_q_norm, hy_kv_norm, hy_w_uq, hy_w_ukv, hy_w_out,
              nsa_w_in, nsa_cmp_pe, nsa_cmp_w1, nsa_cmp_w2, nsa_gate_b, nsa_w_out):
    cos, sin = rope_tables(positions)
    c_act = jax.nn.silu(c)
    for l in range(DEPTH):
        mod = c_act @ ada_w[l] + ada_b[l]
        sh1, sc1, g1, sh2, sc2, g2, sh3, sc3, g3 = jnp.split(mod, N_ADA, axis=-1)
        h = modulate(x, norm_g[l, 0], sh1, sc1)
        x = x + 0.5 * g1[:, None] * swiglu(h, ff_w13[l, 0], ff_w2[l, 0])
        h = modulate(x, norm_g[l, 1], sh2, sc2)
        m = l // 2
        if l % 2 == 0:
            y = hybrid_conv_mla(h, cos, sin, hy_w_in[m], hy_conv_w[m], hy_q_norm[m],
                                hy_kv_norm[m], hy_w_uq[m], hy_w_ukv[m], hy_w_out[m])
        else:
            y = nsa_attention(h, nsa_w_in[m], nsa_cmp_pe[m], nsa_cmp_w1[m], nsa_cmp_w2[m],
                              nsa_gate_b[m], nsa_w_out[m])
        x = x + g2[:, None] * y
        h = modulate(x, norm_g[l, 2], sh3, sc3)
        x = x + 0.5 * g3[:, None] * swiglu(h, ff_w13[l, 1], ff_w2[l, 1])
    return rms_norm(x, final_g)
```

```python
import functools

import jax
import jax.numpy as jnp
import numpy as np
from jax import lax
from jax.experimental import pallas as pl
from jax.experimental.pallas import tpu as pltpu

F32 = jnp.float32
BF16 = jnp.bfloat16

N_ADA = 9
EPS = 1e-6
NEG_INF = -1e30
CONV_WIDTH = 512
CONV_TAPS = 3
MLA_HEADS = 8
MLA_NOPE = 64
MLA_ROPE = 32
MLA_V = 64
Q_LORA = 256
KV_LORA = 128
ROPE_THETA = 10000.0
NSA_HEADS = 16
NSA_KV_HEADS = 2
NSA_GROUP = NSA_HEADS // NSA_KV_HEADS
NSA_DK = 64
CMP_BLOCK = 32
CMP_STRIDE = 16
CMP_HID = 128
SLC_BLOCK = 64
N_SEL = 8
WINDOW = 512
FORCE_SCORE = 1e4
NSA_KV_W = 2 * NSA_KV_HEADS * NSA_DK

LANES = 128
HEAD_PAD = 128
VMEM_LIMIT = 56 * 1024 * 1024
ROW_TILE = 512
MLA_TQ = 512
NSA_TQ = 128
NSA_TK = 512


def _cparams(sem):
    return pltpu.CompilerParams(dimension_semantics=sem, vmem_limit_bytes=VMEM_LIMIT)


def _const_spec(shape):
    nd = len(shape)
    return pl.BlockSpec(shape, lambda *_: (0,) * nd, pipeline_mode=pl.Buffered(1))


def _sigmoid(v):
    return 1.0 / (1.0 + jnp.exp(-v))


def _rms(v, g):
    return v * lax.rsqrt(jnp.mean(v * v, axis=-1, keepdims=True) + EPS) * g


def _modulate(x, g, mod_ref, k0):
    shift = mod_ref[k0:k0 + 1, :]
    scale = mod_ref[k0 + 1:k0 + 2, :]
    return _rms(x, g) * (1.0 + scale) + shift


def _dot(a, b):
    return jnp.dot(a, b, preferred_element_type=F32)


def _dot_nt(a, b):
    return lax.dot_general(a, b, (((1,), (1,)), ((), ())), preferred_element_type=F32)


def _dot_split3(a, b):
    hi = a.astype(BF16)
    r1 = a - hi.astype(F32)
    mid = r1.astype(BF16)
    lo = (r1 - mid.astype(F32)).astype(BF16)
    return _dot(hi, b) + _dot(mid, b) + _dot(lo, b)


def _ada_kernel(c_ref, w_ref, b_ref, o_ref):
    c = c_ref[...]
    ca = (c * _sigmoid(c)).astype(BF16)
    o_ref[...] = _dot(ca, w_ref[...].astype(BF16)) + b_ref[...]


def _ada_mod(c, ada_w, ada_b):
    depth, d, n = ada_w.shape
    b = c.shape[0]
    tn = n // 8
    out = pl.pallas_call(
        _ada_kernel,
        out_shape=jax.ShapeDtypeStruct((depth, b, n), F32),
        grid=(depth, n // tn),
        in_specs=[
            pl.BlockSpec((b, d), lambda l, j: (0, 0)),
            pl.BlockSpec((None, d, tn), lambda l, j: (l, 0, j)),
            pl.BlockSpec((None, 1, tn), lambda l, j: (l, 0, j)),
        ],
        out_specs=pl.BlockSpec((None, b, tn), lambda l, j: (l, 0, j)),
        compiler_params=_cparams(("arbitrary", "arbitrary")),
        name="ada_mod",
    )(c, ada_w, ada_b.reshape(depth, 1, n))
    return out.reshape(depth, b, N_ADA, d)


def _ffn_kernel(x_ref, mod_ref, g_ref, w13_ref, w2_ref, fg_ref, o_ref, *, k0, d_ff, final):
    x = x_ref[...]
    h = _modulate(x, g_ref[...], mod_ref, k0).astype(BF16)
    ab = _dot(h, w13_ref[...])
    a = ab[:, :d_ff]
    b = ab[:, d_ff:]
    u = (a * _sigmoid(a) * b).astype(BF16)
    y = _dot(u, w2_ref[...])
    out = x + (0.5 * mod_ref[k0 + 2:k0 + 3, :]) * y
    if final:
        out = _rms(out, fg_ref[...])
    o_ref[...] = out


def _ffn(x, mod_l, g, w13, w2, final_g, *, k0, final):
    b, s, d = x.shape
    d_ff = w2.shape[0]
    tm = min(ROW_TILE, s)
    kern = functools.partial(_ffn_kernel, k0=k0, d_ff=d_ff, final=final)
    return pl.pallas_call(
        kern,
        out_shape=jax.ShapeDtypeStruct(x.shape, F32),
        grid=(b, s // tm),
        in_specs=[
            pl.BlockSpec((None, tm, d), lambda i, j: (i, j, 0)),
            pl.BlockSpec((None, N_ADA, d), lambda i, j: (i, 0, 0)),
            _const_spec((1, d)),
            _const_spec((d, 2 * d_ff)),
            _const_spec((d_ff, d)),
            _const_spec((1, d)),
        ],
        out_specs=pl.BlockSpec((None, tm, d), lambda i, j: (i, j, 0)),
        compiler_params=_cparams(("arbitrary", "arbitrary")),
        name="ffn",
    )(x, mod_l, g.reshape(1, d), w13, w2, final_g.reshape(1, d))


def _out_proj_kernel(*refs, n_parts):
    x_ref, mod_ref = refs[0], refs[1]
    part_refs = refs[2:2 + n_parts]
    w_refs = refs[2 + n_parts:2 + 2 * n_parts]
    o_ref = refs[2 + 2 * n_parts]
    y = _dot(part_refs[0][...], w_refs[0][...])
    for p_ref, w_ref in zip(part_refs[1:], w_refs[1:]):
        y = y + _dot(p_ref[...], w_ref[...])
    o_ref[...] = x_ref[...] + mod_ref[5:6, :] * y


def _out_proj(x, mod_l, parts, w_parts):
    b, s, d = x.shape
    tm = min(ROW_TILE, s)
    n_parts = len(parts)
    in_specs = [
        pl.BlockSpec((None, tm, d), lambda i, j: (i, j, 0)),
        pl.BlockSpec((None, N_ADA, d), lambda i, j: (i, 0, 0)),
    ]
    in_specs += [pl.BlockSpec((None, tm, p.shape[-1]), lambda i, j: (i, j, 0)) for p in parts]
    in_specs += [_const_spec(w.shape) for w in w_parts]
    return pl.pallas_call(
        functools.partial(_out_proj_kernel, n_parts=n_parts),
        out_shape=jax.ShapeDtypeStruct(x.shape, F32),
        grid=(b, s // tm),
        in_specs=in_specs,
        out_specs=pl.BlockSpec((None, tm, d), lambda i, j: (i, j, 0)),
        compiler_params=_cparams(("arbitrary", "arbitrary")),
        name="out_proj",
    )(x, mod_l, *parts, *w_parts)


def _rope_padded(t, rc, rs_dn, rs_up):
    width = t.shape[-1]
    return (t * rc + pltpu.roll(t, MLA_ROPE // 2, 1) * rs_dn
            + pltpu.roll(t, width - MLA_ROPE // 2, 1) * rs_up)


def _hy_proj_kernel(x_ref, mod_ref, g_ref, win_ref, convw_ref, qn_ref, kvn_ref, wuq_ref, wk_ref,
                    wv_ref, place_ref, rc_ref, rdn_ref, rup_ref,
                    yconv_ref, q_ref, k_ref, v_ref, carry_ref, *, q_scale):
    tm = x_ref.shape[0]
    cw = CONV_WIDTH
    h = _modulate(x_ref[...], g_ref[...], mod_ref, 3).astype(BF16)
    z = _dot(h, win_ref[...])
    u, gate_c, gate_b = z[:, :cw], z[:, cw:2 * cw], z[:, 2 * cw:3 * cw]
    o = 3 * cw
    cq = z[:, o:o + Q_LORA]
    ckv = z[:, o + Q_LORA:o + Q_LORA + KV_LORA]
    kr = z[:, o + Q_LORA + KV_LORA:o + Q_LORA + KV_LORA + MLA_ROPE]

    @pl.when(pl.program_id(1) == 0)
    def _():
        carry_ref[...] = jnp.zeros_like(carry_ref)

    v = gate_c * u
    prev = carry_ref[...]
    row = lax.broadcasted_iota(jnp.int32, (tm, 1), 0)
    v1 = jnp.where(row == 0, prev[7:8, :], pltpu.roll(v, 1, 0))
    v2 = jnp.where(row == 0, prev[6:7, :], jnp.where(row == 1, prev[7:8, :], pltpu.roll(v, 2, 0)))
    w = convw_ref[...]
    yconv_ref[...] = (gate_b * (w[0:1, :] * v2 + w[1:2, :] * v1 + w[2:3, :] * v)).astype(BF16)
    carry_ref[...] = v[tm - 8:, :]

    qn = _rms(cq, qn_ref[...]).astype(BF16)
    kvn = _rms(ckv, kvn_ref[...]).astype(BF16)
    q = _dot(qn, wuq_ref[...])
    k = _dot(kvn, wk_ref[...]) + _dot(kr.astype(BF16), place_ref[...])
    n_rep = q.shape[-1] // HEAD_PAD
    rc = jnp.concatenate([rc_ref[...]] * n_rep, axis=1)
    rdn = jnp.concatenate([rdn_ref[...]] * n_rep, axis=1)
    rup = jnp.concatenate([rup_ref[...]] * n_rep, axis=1)
    q_ref[...] = (_rope_padded(q, rc, rdn, rup) * q_scale).astype(BF16)
    k_ref[...] = _rope_padded(k, rc, rdn, rup).astype(BF16)
    v_ref[...] = _dot(kvn, wv_ref[...]).astype(BF16)


def _hy_weights(w_uq, w_ukv):
    hp = HEAD_PAD
    qd = MLA_NOPE + MLA_ROPE
    wq = w_uq.reshape(Q_LORA, MLA_HEADS, qd)
    wq = jnp.pad(wq, ((0, 0), (0, 0), (0, hp - qd))).reshape(Q_LORA, MLA_HEADS * hp)
    wkv = w_ukv.reshape(KV_LORA, MLA_HEADS, MLA_NOPE + MLA_V)
    wk = jnp.pad(wkv[..., :MLA_NOPE], ((0, 0), (0, 0), (0, hp - MLA_NOPE)))
    wk = wk.reshape(KV_LORA, MLA_HEADS * hp)
    wv = wkv[..., MLA_NOPE:].reshape(KV_LORA, MLA_HEADS * MLA_V)
    place = np.zeros((MLA_ROPE, MLA_HEADS, hp), np.float32)
    for j in range(MLA_ROPE):
        place[j, :, MLA_NOPE + j] = 1.0
    place = jnp.asarray(place.reshape(MLA_ROPE, MLA_HEADS * hp))
    return wq.astype(BF16), wk.astype(BF16), wv.astype(BF16), place.astype(BF16)


def _rope_tables(positions):
    half = MLA_ROPE // 2
    inv = ROPE_THETA ** (-jnp.arange(half, dtype=F32) / half)
    ang = positions.astype(F32)[..., None] * inv
    cos, sin = jnp.cos(ang), jnp.sin(ang)
    lead = positions.shape
    ones = jnp.ones(lead + (MLA_NOPE,), F32)
    z16 = jnp.zeros(lead + (half,), F32)
    ztail = jnp.zeros(lead + (HEAD_PAD - MLA_NOPE - MLA_ROPE,), F32)
    zhead = jnp.zeros(lead + (MLA_NOPE,), F32)
    rc = jnp.concatenate([ones, cos, cos, ztail], axis=-1)
    r_dn = jnp.concatenate([zhead, z16, sin, ztail], axis=-1)
    r_up = jnp.concatenate([zhead, -sin, z16, ztail], axis=-1)
    return rc, r_dn, r_up


def _hy_proj(x, mod_l, g, w_in, conv_w, q_norm, kv_norm, wq, wk, wv, place, ropes):
    b, s, d = x.shape
    tm = min(ROW_TILE, s)
    hp = HEAD_PAD
    qw = MLA_HEADS * hp
    vw = MLA_HEADS * MLA_V
    row = lambda width: pl.BlockSpec((None, tm, width), lambda i, j: (i, j, 0))
    q_scale = (MLA_NOPE + MLA_ROPE) ** -0.5
    return pl.pallas_call(
        functools.partial(_hy_proj_kernel, q_scale=q_scale),
        out_shape=(
            jax.ShapeDtypeStruct((b, s, CONV_WIDTH), BF16),
            jax.ShapeDtypeStruct((b, s, qw), BF16),
            jax.ShapeDtypeStruct((b, s, qw), BF16),
            jax.ShapeDtypeStruct((b, s, vw), BF16),
        ),
        grid=(b, s // tm),
        in_specs=[
            row(d),
            pl.BlockSpec((None, N_ADA, d), lambda i, j: (i, 0, 0)),
            _const_spec((1, d)),
            _const_spec(w_in.shape),
            _const_spec(conv_w.shape),
            _const_spec((1, Q_LORA)),
            _const_spec((1, KV_LORA)),
            _const_spec(wq.shape),
            _const_spec(wk.shape),
            _const_spec(wv.shape),
            _const_spec(place.shape),
            row(hp), row(hp), row(hp),
        ],
        out_specs=(row(CONV_WIDTH), row(qw), row(qw), row(vw)),
        scratch_shapes=[pltpu.VMEM((8, CONV_WIDTH), F32)],
        compiler_params=_cparams(("arbitrary", "arbitrary")),
        name="hy_proj",
    )(x, mod_l, g.reshape(1, d), w_in, conv_w, q_norm.reshape(1, -1), kv_norm.reshape(1, -1),
      wq, wk, wv, place, *ropes)


def _online_update(s, mask, v, m, l, acc):
    s = jnp.where(mask, s, NEG_INF)
    m_new = jnp.maximum(m, jnp.max(s, axis=-1, keepdims=True))
    alpha = jnp.exp(m - m_new)
    p = jnp.where(mask, jnp.exp(s - m_new), 0.0)
    l = alpha * l + jnp.sum(p, axis=-1, keepdims=True)
    acc = alpha * acc + _dot(p.astype(BF16), v)
    return m_new, l, acc


def _mla_attn_kernel(q_ref, k_ref, v_ref, o_ref, *, tk):
    tq = q_ref.shape[0]
    hp = HEAD_PAD
    q0 = pl.program_id(2) * tq
    q = q_ref[...]
    qpos = q0 + lax.broadcasted_iota(jnp.int32, (tq, 1), 0)
    n_chunks = (q0 + tq + tk - 1) // tk

    def body(c, carry):
        start = pl.multiple_of(c * tk, tk)
        k = k_ref[pl.ds(start, tk), :]
        v = v_ref[pl.ds(start, tk), :]
        mask = (start + lax.broadcasted_iota(jnp.int32, (1, tk), 1)) <= qpos
        out = []
        for hh in range(2):
            m, l, acc = carry[hh]
            s = _dot_nt(q[:, hh * hp:(hh + 1) * hp], k[:, hh * hp:(hh + 1) * hp])
            out.append(_online_update(s, mask, v, m, l, acc))
        return tuple(out)

    init = tuple((jnp.full((tq, 1), NEG_INF, F32), jnp.zeros((tq, 1), F32),
                  jnp.zeros((tq, 2 * MLA_V), F32)) for _ in range(2))
    (_, l0, a0), (_, l1, a1) = lax.fori_loop(0, n_chunks, body, init)
    lane = lax.broadcasted_iota(jnp.int32, (1, 2 * MLA_V), 1)
    o = jnp.where(lane < MLA_V, a0 / jnp.maximum(l0, 1e-20), a1 / jnp.maximum(l1, 1e-20))
    o_ref[...] = o.astype(BF16)


def _mla_attn(q, k, v):
    b, s, _ = q.shape
    tq = min(MLA_TQ, s)
    hp2 = 2 * HEAD_PAD
    v2 = 2 * MLA_V
    return pl.pallas_call(
        functools.partial(_mla_attn_kernel, tk=tq),
        out_shape=jax.ShapeDtypeStruct((b, s, MLA_HEADS * MLA_V), BF16),
        grid=(b, MLA_HEADS // 2, s // tq),
        in_specs=[
            pl.BlockSpec((None, tq, hp2), lambda i, h, j: (i, j, h)),
            pl.BlockSpec((None, s, hp2), lambda i, h, j: (i, 0, h)),
            pl.BlockSpec((None, s, v2), lambda i, h, j: (i, 0, h)),
        ],
        out_specs=pl.BlockSpec((None, tq, v2), lambda i, h, j: (i, j, h)),
        compiler_params=_cparams(("arbitrary", "arbitrary", "arbitrary")),
        name="mla_attn",
    )(q, k, v)


def _nsa_proj_kernel(x_ref, mod_ref, g_ref, win_ref, gb_ref, q_ref, kvc_ref, kvs_ref, kvw_ref,
                     gate_ref, *, q_scale):
    h = _modulate(x_ref[...], g_ref[...], mod_ref, 3).astype(BF16)
    z = _dot(h, win_ref[...])
    qd = NSA_HEADS * NSA_DK
    kw = NSA_KV_W
    q_ref[...] = (z[:, :qd] * q_scale).astype(BF16)
    kvc_ref[...] = z[:, qd:qd + kw].astype(BF16)
    kvs_ref[...] = z[:, qd + kw:qd + 2 * kw].astype(BF16)
    kvw_ref[...] = z[:, qd + 2 * kw:qd + 3 * kw].astype(BF16)
    gate_ref[...] = _sigmoid(z[:, qd + 3 * kw:] + gb_ref[...])


def _nsa_proj(x, mod_l, g, w_in, gate_b):
    b, s, d = x.shape
    tm = min(ROW_TILE, s)
    qd = NSA_HEADS * NSA_DK
    ng = 3 * NSA_HEADS
    row = lambda width: pl.BlockSpec((None, tm, width), lambda i, j: (i, j, 0))
    return pl.pallas_call(
        functools.partial(_nsa_proj_kernel, q_scale=NSA_DK ** -0.5),
        out_shape=(
            jax.ShapeDtypeStruct((b, s, qd), BF16),
            jax.ShapeDtypeStruct((b, s, NSA_KV_W), BF16),
            jax.ShapeDtypeStruct((b, s, NSA_KV_W), BF16),
            jax.ShapeDtypeStruct((b, s, NSA_KV_W), BF16),
            jax.ShapeDtypeStruct((b, s, ng), F32),
        ),
        grid=(b, s // tm),
        in_specs=[
            row(d),
            pl.BlockSpec((None, N_ADA, d), lambda i, j: (i, 0, 0)),
            _const_spec((1, d)),
            _const_spec(w_in.shape),
            _const_spec((1, ng)),
        ],
        out_specs=(row(qd), row(NSA_KV_W), row(NSA_KV_W), row(NSA_KV_W), row(ng)),
        compiler_params=_cparams(("arbitrary", "arbitrary")),
        name="nsa_proj",
    )(x, mod_l, g.reshape(1, d), w_in, gate_b.reshape(1, ng))


def _nsa_cmp_kernel(ch_ref, w1a_ref, w1b_ref, pea_ref, peb_ref, w2_ref, o_ref):
    ch = ch_ref[...]
    n = ch.shape[0]
    first = _dot(ch, w1a_ref[...])
    second = _dot(ch, w1b_ref[...])
    bias = _dot(pea_ref[...], w1a_ref[...]) + _dot(peb_ref[...], w1b_ref[...])
    hid = first + pltpu.roll(second, n - 1, 0) + bias[0:1, :]
    act = (hid * _sigmoid(hid)).astype(BF16)
    o_ref[...] = _dot(act, w2_ref[...]).astype(BF16)


def _nsa_cmp_weights(cmp_pe, cmp_w1, cmp_w2):
    ncomp = 2 * NSA_KV_HEADS
    half = CMP_BLOCK // 2
    kv_of = np.arange(ncomp) // NSA_KV_HEADS
    eye = jnp.eye(ncomp, dtype=F32)
    w1 = cmp_w1.reshape(2, 2, half, NSA_DK, CMP_HID)[kv_of]
    big1 = jnp.einsum("chldj,ce->hlcdej", w1, eye).reshape(2, half * ncomp * NSA_DK, ncomp * CMP_HID)
    w2 = cmp_w2[kv_of]
    big2 = jnp.einsum("cjd,ce->cjed", w2, eye).reshape(ncomp * CMP_HID, ncomp * NSA_DK)
    pe = cmp_pe.reshape(2, 2, half, NSA_DK)[kv_of]
    pe = jnp.transpose(pe, (1, 2, 0, 3)).reshape(2, 1, half * ncomp * NSA_DK)
    pe = jnp.broadcast_to(pe, (2, 8, half * ncomp * NSA_DK))
    return (big1[0].astype(BF16), big1[1].astype(BF16), pe[0].astype(BF16), pe[1].astype(BF16),
            big2.astype(BF16))


def _nsa_cmp(kvc, w1a, w1b, pea, peb, w2):
    b, s, kw = kvc.shape
    nchunk = s // CMP_STRIDE
    chunks = kvc.reshape(b, nchunk, CMP_STRIDE * kw)
    return pl.pallas_call(
        _nsa_cmp_kernel,
        out_shape=jax.ShapeDtypeStruct((b, nchunk, kw), BF16),
        grid=(b,),
        in_specs=[
            pl.BlockSpec((None, nchunk, CMP_STRIDE * kw), lambda i: (i, 0, 0)),
            _const_spec(w1a.shape), _const_spec(w1b.shape),
            _const_spec(pea.shape), _const_spec(peb.shape), _const_spec(w2.shape),
        ],
        out_specs=pl.BlockSpec((None, nchunk, kw), lambda i: (i, 0, 0)),
        compiler_params=_cparams(("arbitrary",)),
        name="nsa_cmp",
    )(chunks, w1a, w1b, pea, peb, w2)


def _softmax_rows(s, mask):
    s = jnp.where(mask, s, NEG_INF)
    m = jnp.max(s, axis=-1, keepdims=True)
    p = jnp.where(mask, jnp.exp(s - m), 0.0)
    return p / jnp.maximum(jnp.sum(p, axis=-1, keepdims=True), 1e-20)


def _nsa_attn_kernel(q_ref, kc_ref, kvs_ref, kvw_ref, gate_ref, agg_ref, spread_ref, o_ref,
                     *, tk, n_sel):
    tq = q_ref.shape[0]
    seq = kvs_ref.shape[0]
    ncp = kc_ref.shape[0]
    ns = agg_ref.shape[1]
    dk = NSA_DK
    rep = NSA_GROUP
    q0 = pl.program_id(1) * tq
    t = q0 + lax.broadcasted_iota(jnp.int32, (tq, 1), 0)
    t_rows = jnp.concatenate([t] * rep, axis=0)

    blk = lax.broadcasted_iota(jnp.int32, (1, ns), 1)
    cur = lax.shift_right_logical(t, int(np.log2(SLC_BLOCK)))
    forced = (blk == 0) | (blk == cur) | (blk == cur - 1)
    causal_blk = blk * SLC_BLOCK <= t
    cmp_end = lax.broadcasted_iota(jnp.int32, (1, ncp), 1) * CMP_STRIDE + (CMP_BLOCK - 1)
    cmp_mask = (cmp_end <= t_rows) & (cmp_end < seq)
    win_len = WINDOW + tq
    w_start = pl.multiple_of(jnp.maximum(q0 - WINDOW, 0), tq)
    wpos = w_start + lax.broadcasted_iota(jnp.int32, (1, win_len), 1)
    win_mask = (wpos <= t_rows) & (wpos > t_rows - WINDOW)
    n_chunks = (q0 + tq + tk - 1) // tk
    gates = gate_ref[...]

    for g in range(NSA_KV_HEADS):
        qg = q_ref[:, g * rep * dk:(g + 1) * rep * dk]
        qs = jnp.concatenate([qg[:, r * dk:(r + 1) * dk] for r in range(rep)], axis=0)
        kcol = slice(g * dk, (g + 1) * dk)
        vcol = slice((NSA_KV_HEADS + g) * dk, (NSA_KV_HEADS + g + 1) * dk)

        p_c = _softmax_rows(_dot_nt(qs, kc_ref[:, kcol]), cmp_mask)
        o_c = _dot(p_c.astype(BF16), kc_ref[:, vcol])

        p_sum = p_c[0:tq]
        for r in range(1, rep):
            p_sum = p_sum + p_c[r * tq:(r + 1) * tq]
        imp = _dot_split3(p_sum, agg_ref[...])
        val = jnp.where(forced, FORCE_SCORE, jnp.where(causal_blk, imp, NEG_INF))
        rank = jnp.zeros((tq, ns), F32)
        for i in range(ns):
            col = val[:, i:i + 1]
            beats = (col > val) | ((col == val) & (blk > i))
            rank = rank + jnp.where(beats, 1.0, 0.0)
        sel = jnp.where(rank < n_sel, 1.0, 0.0).astype(BF16)

        def body(c, carry):
            m, l, acc = carry
            start = pl.multiple_of(c * tk, tk)
            k = kvs_ref[pl.ds(start, tk), kcol]
            v = kvs_ref[pl.ds(start, tk), vcol]
            kpos = start + lax.broadcasted_iota(jnp.int32, (1, tk), 1)
            picked = _dot(sel, spread_ref[c])
            mask = (picked > 0.5) & (kpos <= t)
            mask = jnp.concatenate([mask] * rep, axis=0)
            return _online_update(_dot_nt(qs, k), mask, v, m, l, acc)

        init = (jnp.full((rep * tq, 1), NEG_INF, F32), jnp.zeros((rep * tq, 1), F32),
                jnp.zeros((rep * tq, dk), F32))
        _, l_s, acc_s = lax.fori_loop(0, n_chunks, body, init)
        o_s = acc_s / jnp.maximum(l_s, 1e-20)

        kw = kvw_ref[pl.ds(w_start, win_len), kcol]
        vw = kvw_ref[pl.ds(w_start, win_len), vcol]
        p_w = _softmax_rows(_dot_nt(qs, kw), win_mask)
        o_w = _dot(p_w.astype(BF16), vw)

        for r in range(rep):
            rows = slice(r * tq, (r + 1) * tq)
            gc = g * 3 * rep + r
            o_h = (gates[:, gc:gc + 1] * o_c[rows] + gates[:, gc + rep:gc + rep + 1] * o_s[rows]
                   + gates[:, gc + 2 * rep:gc + 2 * rep + 1] * o_w[rows])
            h = g * rep + r
            o_ref[:, h * dk:(h + 1) * dk] = o_h.astype(BF16)


def _nsa_tables(seq, tk):
    nchunk = seq // CMP_STRIDE
    ns = seq // SLC_BLOCK
    cs = np.arange(nchunk)[:, None] * CMP_STRIDE
    ss = np.arange(ns)[None, :] * SLC_BLOCK
    overlap = np.clip(np.minimum(cs + CMP_BLOCK, ss + SLC_BLOCK) - np.maximum(cs, ss), 0, None)
    agg = overlap.astype(np.float32) / CMP_BLOCK
    key_blk = np.arange(seq) // SLC_BLOCK
    spread = (np.arange(ns)[:, None] == key_blk[None, :]).astype(np.float32)
    spread = spread.reshape(ns, seq // tk, tk).transpose(1, 0, 2)
    return jnp.asarray(agg, BF16), jnp.asarray(spread, BF16)


def _nsa_attn(q, kc, kvs, kvw, gates):
    b, s, qd = q.shape
    tq = min(NSA_TQ, s)
    tk = min(NSA_TK, s)
    assert s >= WINDOW + tq and s % tk == 0 and WINDOW % tq == 0
    ncp = kc.shape[1]
    ns = s // SLC_BLOCK
    agg, spread = _nsa_tables(s, tk)
    whole = lambda a: pl.BlockSpec((None,) + a.shape[1:], lambda i, j: (i, 0, 0))
    return pl.pallas_call(
        functools.partial(_nsa_attn_kernel, tk=tk, n_sel=min(N_SEL, ns)),
        out_shape=jax.ShapeDtypeStruct((b, s, qd), BF16),
        grid=(b, s // tq),
        in_specs=[
            pl.BlockSpec((None, tq, qd), lambda i, j: (i, j, 0)),
            whole(kc), whole(kvs), whole(kvw),
            pl.BlockSpec((None, tq, gates.shape[-1]), lambda i, j: (i, j, 0)),
            _const_spec(agg.shape), _const_spec(spread.shape),
        ],
        out_specs=pl.BlockSpec((None, tq, qd), lambda i, j: (i, j, 0)),
        compiler_params=_cparams(("arbitrary", "arbitrary")),
        name="nsa_attn",
    )(q, kc, kvs, kvw, gates, agg, spread)


def _nsa_gate_layout(w_in, gate_b):
    qd = NSA_HEADS * NSA_DK
    o = qd + 3 * NSA_KV_W
    perm = np.arange(3 * NSA_HEADS).reshape(NSA_KV_HEADS, NSA_GROUP, 3).transpose(0, 2, 1).reshape(-1)
    w_in = jnp.concatenate([w_in[:, :o], w_in[:, o:][:, perm]], axis=1)
    return w_in, gate_b[perm]


def kernel(x, c, positions, ada_w, ada_b, norm_g, final_g, ff_w13, ff_w2, hy_w_in, hy_conv_w,
           hy_q_norm, hy_kv_norm, hy_w_uq, hy_w_ukv, hy_w_out, nsa_w_in, nsa_cmp_pe, nsa_cmp_w1,
           nsa_cmp_w2, nsa_gate_b, nsa_w_out):
    depth = ada_w.shape[0]
    mod = _ada_mod(c, ada_w, ada_b)
    ropes = _rope_tables(positions)
    w13 = ff_w13.astype(BF16)
    w2 = ff_w2.astype(BF16)
    for l in range(depth):
        m = l // 2
        x = _ffn(x, mod[l], norm_g[l, 0], w13[l, 0], w2[l, 0], final_g, k0=0, final=False)
        if l % 2 == 0:
            wq, wk, wv, place = _hy_weights(hy_w_uq[m], hy_w_ukv[m])
            y_conv, q, k, v = _hy_proj(x, mod[l], norm_g[l, 1], hy_w_in[m].astype(BF16),
                                       hy_conv_w[m], hy_q_norm[m], hy_kv_norm[m],
                                       wq, wk, wv, place, ropes)
            y_att = _mla_attn(q, k, v)
            w_out = hy_w_out[m].astype(BF16)
            x = _out_proj(x, mod[l], [y_conv, y_att], [w_out[:CONV_WIDTH], w_out[CONV_WIDTH:]])
        else:
            w_in, gate_b = _nsa_gate_layout(nsa_w_in[m], nsa_gate_b[m])
            q, kvc, kvs, kvw, gates = _nsa_proj(x, mod[l], norm_g[l, 1], w_in.astype(BF16), gate_b)
            kc = _nsa_cmp(kvc, *_nsa_cmp_weights(nsa_cmp_pe[m], nsa_cmp_w1[m], nsa_cmp_w2[m]))
            o = _nsa_attn(q, kc, kvs, kvw, gates)
            x = _out_proj(x, mod[l], [o], [nsa_w_out[m].astype(BF16)])
        x = _ffn(x, mod[l], norm_g[l, 2], w13[l, 1], w2[l, 1], final_g, k0=6, final=(l == depth - 1))
    return x
```

```python
import functools

import jax
import jax.numpy as jnp
import numpy as np
from jax import lax
from jax.experimental import pallas as pl
from jax.experimental.pallas import tpu as pltpu

F32 = jnp.float32
BF16 = jnp.bfloat16

N_ADA = 9
EPS = 1e-6
NEG_INF = -1e30
CONV_WIDTH = 512
CONV_TAPS = 3
MLA_HEADS = 8
MLA_NOPE = 64
MLA_ROPE = 32
MLA_V = 64
Q_LORA = 256
KV_LORA = 128
ROPE_THETA = 10000.0
NSA_HEADS = 16
NSA_KV_HEADS = 2
NSA_GROUP = NSA_HEADS // NSA_KV_HEADS
NSA_DK = 64
CMP_BLOCK = 32
CMP_STRIDE = 16
CMP_HID = 128
SLC_BLOCK = 64
N_SEL = 8
WINDOW = 512
FORCE_SCORE = 1e4
NSA_KV_W = 2 * NSA_KV_HEADS * NSA_DK

HEAD_PAD = 128
VMEM_LIMIT = 56 * 1024 * 1024
ROW_TILE = 512
MLA_TQ = 512
MLA_HEADS_PER_STEP = 4
NSA_TQ = 128
NSA_TK = 512

LOG2E = float(np.log2(np.e))
MASK_BIG = float(2.0 ** 100)


def _cparams(sem):
    return pltpu.CompilerParams(dimension_semantics=sem, vmem_limit_bytes=VMEM_LIMIT)


def _const_spec(shape):
    nd = len(shape)
    return pl.BlockSpec(shape, lambda *_: (0,) * nd, pipeline_mode=pl.Buffered(1))


def _sigmoid(v):
    return 1.0 / (1.0 + jnp.exp(-v))


def _rms(v, g):
    return v * lax.rsqrt(jnp.mean(v * v, axis=-1, keepdims=True) + EPS) * g


def _modulate(x, g, mod_ref, k0):
    shift = mod_ref[k0:k0 + 1, :]
    scale = mod_ref[k0 + 1:k0 + 2, :]
    return _rms(x, g) * (1.0 + scale) + shift


def _dot(a, b):
    return jnp.dot(a, b, preferred_element_type=F32)


def _dot_nt(a, b):
    return lax.dot_general(a, b, (((1,), (1,)), ((), ())), preferred_element_type=F32)


def _split3(a):
    hi = a.astype(BF16)
    r1 = a - hi.astype(F32)
    mid = r1.astype(BF16)
    lo = (r1 - mid.astype(F32)).astype(BF16)
    return hi, mid, lo


def _row_recip(v):
    return 1.0 / jnp.maximum(v, 1e-20)


def _ada_kernel(c_ref, w_ref, b_ref, o_ref):
    c = c_ref[...]
    ca = (c * _sigmoid(c)).astype(BF16)
    o_ref[...] = _dot(ca, w_ref[...].astype(BF16)) + b_ref[...]


def _ada_mod(c, ada_w, ada_b):
    depth, d, n = ada_w.shape
    b = c.shape[0]
    tn = n // 8
    out = pl.pallas_call(
        _ada_kernel,
        out_shape=jax.ShapeDtypeStruct((depth, b, n), F32),
        grid=(depth, n // tn),
        in_specs=[
            pl.BlockSpec((b, d), lambda l, j: (0, 0)),
            pl.BlockSpec((None, d, tn), lambda l, j: (l, 0, j)),
            pl.BlockSpec((None, 1, tn), lambda l, j: (l, 0, j)),
        ],
        out_specs=pl.BlockSpec((None, b, tn), lambda l, j: (l, 0, j)),
        compiler_params=_cparams(("arbitrary", "arbitrary")),
        name="ada_mod",
    )(c, ada_w, ada_b.reshape(depth, 1, n))
    return out.reshape(depth, b, N_ADA, d)


def _ffn_kernel(x_ref, mod_ref, g_ref, w13_ref, w2_ref, fg_ref, o_ref, *, k0, d_ff, final):
    x = x_ref[...]
    h = _modulate(x, g_ref[...], mod_ref, k0).astype(BF16)
    ab = _dot(h, w13_ref[...])
    a = ab[:, :d_ff]
    b = ab[:, d_ff:]
    u = (a * _sigmoid(a) * b).astype(BF16)
    y = _dot(u, w2_ref[...])
    out = x + (0.5 * mod_ref[k0 + 2:k0 + 3, :]) * y
    if final:
        out = _rms(out, fg_ref[...])
    o_ref[...] = out


def _ffn(x, mod_l, g, w13, w2, final_g, *, k0, final):
    b, s, d = x.shape
    d_ff = w2.shape[0]
    tm = min(ROW_TILE, s)
    kern = functools.partial(_ffn_kernel, k0=k0, d_ff=d_ff, final=final)
    return pl.pallas_call(
        kern,
        out_shape=jax.ShapeDtypeStruct(x.shape, F32),
        grid=(b, s // tm),
        in_specs=[
            pl.BlockSpec((None, tm, d), lambda i, j: (i, j, 0)),
            pl.BlockSpec((None, N_ADA, d), lambda i, j: (i, 0, 0)),
            _const_spec((1, d)),
            _const_spec((d, 2 * d_ff)),
            _const_spec((d_ff, d)),
            _const_spec((1, d)),
        ],
        out_specs=pl.BlockSpec((None, tm, d), lambda i, j: (i, j, 0)),
        compiler_params=_cparams(("arbitrary", "arbitrary")),
        name="ffn",
    )(x, mod_l, g.reshape(1, d), w13, w2, final_g.reshape(1, d))


def _out_proj_kernel(*refs, n_parts):
    x_ref, mod_ref = refs[0], refs[1]
    part_refs = refs[2:2 + n_parts]
    w_refs = refs[2 + n_parts:2 + 2 * n_parts]
    o_ref = refs[2 + 2 * n_parts]
    y = _dot(part_refs[0][...], w_refs[0][...])
    for p_ref, w_ref in zip(part_refs[1:], w_refs[1:]):
        y = y + _dot(p_ref[...], w_ref[...])
    o_ref[...] = x_ref[...] + mod_ref[5:6, :] * y


def _out_proj(x, mod_l, parts, w_parts):
    b, s, d = x.shape
    tm = min(ROW_TILE, s)
    n_parts = len(parts)
    in_specs = [
        pl.BlockSpec((None, tm, d), lambda i, j: (i, j, 0)),
        pl.BlockSpec((None, N_ADA, d), lambda i, j: (i, 0, 0)),
    ]
    in_specs += [pl.BlockSpec((None, tm, p.shape[-1]), lambda i, j: (i, j, 0)) for p in parts]
    in_specs += [_const_spec(w.shape) for w in w_parts]
    return pl.pallas_call(
        functools.partial(_out_proj_kernel, n_parts=n_parts),
        out_shape=jax.ShapeDtypeStruct(x.shape, F32),
        grid=(b, s // tm),
        in_specs=in_specs,
        out_specs=pl.BlockSpec((None, tm, d), lambda i, j: (i, j, 0)),
        compiler_params=_cparams(("arbitrary", "arbitrary")),
        name="out_proj",
    )(x, mod_l, *parts, *w_parts)


def _rope_swap(t):
    width = t.shape[-1]
    half = MLA_ROPE // 2
    lane = lax.broadcasted_iota(jnp.int32, (1, width), 1) & (HEAD_PAD - 1)
    return jnp.where(lane < MLA_NOPE + half, pltpu.roll(t, width - half, 1), pltpu.roll(t, half, 1))


def _hy_proj_kernel(x_ref, mod_ref, g_ref, win_ref, convw_ref, qn_ref, kvn_ref, wq_ref,
                    wk_ref, wv_ref, rc_ref, rs_ref,
                    yconv_ref, q_ref, k_ref, v_ref, carry_ref, *, q_scale):
    tm = x_ref.shape[0]
    cw = CONV_WIDTH
    h = _modulate(x_ref[...], g_ref[...], mod_ref, 3).astype(BF16)
    z = _dot(h, win_ref[...])
    u, gate_c, gate_b = z[:, :cw], z[:, cw:2 * cw], z[:, 2 * cw:3 * cw]
    o = 3 * cw
    cq = z[:, o:o + Q_LORA]
    ckv = z[:, o + Q_LORA:o + Q_LORA + KV_LORA]
    kr_tile = z[:, o + Q_LORA + KV_LORA:]

    @pl.when(pl.program_id(1) == 0)
    def _():
        carry_ref[...] = jnp.zeros_like(carry_ref)

    v = gate_c * u
    prev = carry_ref[...]
    row = lax.broadcasted_iota(jnp.int32, (tm, 1), 0)
    v1 = jnp.where(row == 0, prev[7:8, :], pltpu.roll(v, 1, 0))
    v2 = jnp.where(row == 0, prev[6:7, :], jnp.where(row == 1, prev[7:8, :], pltpu.roll(v, 2, 0)))
    w = convw_ref[...]
    yconv_ref[...] = (gate_b * (w[0:1, :] * v2 + w[1:2, :] * v1 + w[2:3, :] * v)).astype(BF16)
    carry_ref[...] = v[tm - 8:, :]

    qn = _rms(cq, qn_ref[...]).astype(BF16)
    kvn = _rms(ckv, kvn_ref[...]).astype(BF16)
    n_rep = q_ref.shape[-1] // HEAD_PAD
    rc = rc_ref[...]
    rs = rs_ref[...]
    q = _dot(qn, wq_ref[...])
    q = q * jnp.concatenate([rc] * n_rep, axis=1) + _rope_swap(q) * jnp.concatenate([rs] * n_rep, axis=1)
    kr = pltpu.roll(kr_tile, MLA_NOPE, 1)
    kr = kr * rc + _rope_swap(kr) * rs
    k = _dot(kvn, wk_ref[...]) + jnp.concatenate([kr] * n_rep, axis=1)
    q_ref[...] = (q * q_scale).astype(BF16)
    k_ref[...] = k.astype(BF16)
    lane = lax.broadcasted_iota(jnp.int32, (1, v_ref.shape[-1]), 1)
    ones_col = jnp.where((lane & (HEAD_PAD - 1)) == MLA_V, 1.0, 0.0)
    v_ref[...] = (_dot(kvn, wv_ref[...]) + ones_col).astype(BF16)


def _hy_weights(w_uq, w_ukv):
    hp = HEAD_PAD
    pad_heads = lambda w: jnp.pad(w, ((0, 0), (0, 0), (0, hp - w.shape[-1]))).reshape(w.shape[0], -1)
    wq = pad_heads(w_uq.reshape(Q_LORA, MLA_HEADS, MLA_NOPE + MLA_ROPE))
    wkv = w_ukv.reshape(KV_LORA, MLA_HEADS, MLA_NOPE + MLA_V)
    wk = pad_heads(wkv[..., :MLA_NOPE])
    wv = pad_heads(wkv[..., MLA_NOPE:])
    return wq.astype(BF16), wk.astype(BF16), wv.astype(BF16)


def _rope_tables(positions):
    half = MLA_ROPE // 2
    inv = ROPE_THETA ** (-jnp.arange(half, dtype=F32) / half)
    ang = positions.astype(F32)[..., None] * inv
    cos, sin = jnp.cos(ang), jnp.sin(ang)
    lead = positions.shape
    ones = jnp.ones(lead + (MLA_NOPE,), F32)
    ztail = jnp.zeros(lead + (HEAD_PAD - MLA_NOPE - MLA_ROPE,), F32)
    zhead = jnp.zeros(lead + (MLA_NOPE,), F32)
    rc = jnp.concatenate([ones, cos, cos, ztail], axis=-1)
    rs = jnp.concatenate([zhead, -sin, sin, ztail], axis=-1)
    return rc, rs


def _hy_proj(x, mod_l, g, w_in, conv_w, q_norm, kv_norm, hy_w, ropes):
    b, s, d = x.shape
    tm = min(ROW_TILE, s)
    hp = HEAD_PAD
    qw = MLA_HEADS * hp
    row = lambda width: pl.BlockSpec((None, tm, width), lambda i, j: (i, j, 0))
    q_scale = (MLA_NOPE + MLA_ROPE) ** -0.5 * LOG2E
    return pl.pallas_call(
        functools.partial(_hy_proj_kernel, q_scale=q_scale),
        out_shape=(
            jax.ShapeDtypeStruct((b, s, CONV_WIDTH), BF16),
            jax.ShapeDtypeStruct((b, s, qw), BF16),
            jax.ShapeDtypeStruct((b, s, qw), BF16),
            jax.ShapeDtypeStruct((b, s, qw), BF16),
        ),
        grid=(b, s // tm),
        in_specs=[
            row(d),
            pl.BlockSpec((None, N_ADA, d), lambda i, j: (i, 0, 0)),
            _const_spec((1, d)),
            _const_spec(w_in.shape),
            _const_spec(conv_w.shape),
            _const_spec((1, Q_LORA)),
            _const_spec((1, KV_LORA)),
        ] + [_const_spec(w.shape) for w in hy_w] + [row(hp), row(hp)],
        out_specs=(row(CONV_WIDTH), row(qw), row(qw), row(qw)),
        scratch_shapes=[pltpu.VMEM((8, CONV_WIDTH), F32)],
        compiler_params=_cparams(("arbitrary", "arbitrary")),
        name="hy_proj",
    )(x, mod_l, g.reshape(1, d), w_in, conv_w, q_norm.reshape(1, -1), kv_norm.reshape(1, -1),
      *hy_w, *ropes)


def _online_step(s, v, m, acc):
    m_new = jnp.maximum(m, jnp.max(s, axis=-1, keepdims=True))
    p = jnp.exp2(s - m_new)
    acc = jnp.exp2(m - m_new) * acc + _dot(p.astype(BF16), v)
    return m_new, acc


def _mla_attn_kernel(q_ref, k_ref, v_ref, o_ref):
    tq = q_ref.shape[0]
    hp = HEAD_PAD
    n_heads = q_ref.shape[1] // hp
    qi = pl.program_id(2)
    q = q_ref[...]
    rel = (lax.broadcasted_iota(jnp.int32, (1, tq), 1)
           <= lax.broadcasted_iota(jnp.int32, (tq, 1), 0))
    diag_bias = jnp.where(rel, 0.0, -MASK_BIG)

    def chunk(c, carry, diag):
        start = pl.multiple_of(c * tq, tq)
        out = []
        for hh in range(n_heads):
            cols = slice(hh * hp, (hh + 1) * hp)
            s = _dot_nt(q[:, cols], k_ref[pl.ds(start, tq), cols])
            if diag:
                s = s + diag_bias
            out.append(_online_step(s, v_ref[pl.ds(start, tq), cols], *carry[hh]))
        return tuple(out)

    init = tuple((jnp.full((tq, 1), NEG_INF, F32), jnp.zeros((tq, hp), F32)) for _ in range(n_heads))
    carry = lax.fori_loop(0, qi, functools.partial(chunk, diag=False), init)
    outs = [acc * _row_recip(acc[:, MLA_V:MLA_V + 1]) for _, acc in chunk(qi, carry, True)]
    lane = lax.broadcasted_iota(jnp.int32, (1, hp), 1)
    for pair in range(n_heads // 2):
        both = jnp.where(lane < MLA_V, outs[2 * pair], pltpu.roll(outs[2 * pair + 1], MLA_V, 1))
        o_ref[:, pair * hp:(pair + 1) * hp] = both.astype(BF16)


def _mla_attn(q, k, v):
    b, s, _ = q.shape
    tq = min(MLA_TQ, s)
    qw = MLA_HEADS_PER_STEP * HEAD_PAD
    ow = MLA_HEADS_PER_STEP * MLA_V
    return pl.pallas_call(
        _mla_attn_kernel,
        out_shape=jax.ShapeDtypeStruct((b, s, MLA_HEADS * MLA_V), BF16),
        grid=(b, MLA_HEADS // MLA_HEADS_PER_STEP, s // tq),
        in_specs=[
            pl.BlockSpec((None, tq, qw), lambda i, h, j: (i, j, h)),
            pl.BlockSpec((None, s, qw), lambda i, h, j: (i, 0, h)),
            pl.BlockSpec((None, s, qw), lambda i, h, j: (i, 0, h)),
        ],
        out_specs=pl.BlockSpec((None, tq, ow), lambda i, h, j: (i, j, h)),
        compiler_params=_cparams(("arbitrary", "arbitrary", "arbitrary")),
        name="mla_attn",
    )(q, k, v)


def _nsa_proj_kernel(x_ref, mod_ref, g_ref, win_ref, gb_ref, q_ref, kvc_ref, ks_ref, vs_ref,
                     kw_ref, vw_ref, gate_ref, *, q_scale):
    tm = x_ref.shape[0]
    hp, dk = HEAD_PAD, NSA_DK
    pad = hp - dk
    h = _modulate(x_ref[...], g_ref[...], mod_ref, 3).astype(BF16)
    z = _dot(h, win_ref[...])
    qd = NSA_HEADS * dk
    kw = NSA_KV_W
    zeros = jnp.zeros((tm, pad), BF16)
    qz = (z[:, :qd] * q_scale).astype(BF16)
    for hd in range(NSA_HEADS):
        q_ref[:, hd * hp:hd * hp + dk] = qz[:, hd * dk:(hd + 1) * dk]
        q_ref[:, hd * hp + dk:(hd + 1) * hp] = zeros
    kvc_ref[...] = z[:, qd:qd + kw].astype(BF16)
    pos = pl.program_id(1) * tm + lax.broadcasted_iota(jnp.int32, (tm, 1), 0)
    lane = lax.broadcasted_iota(jnp.int32, (1, pad), 1)
    blk_onehot = jnp.where(lax.shift_right_logical(pos, int(np.log2(SLC_BLOCK))) == lane, 1.0, 0.0)
    blk_onehot = blk_onehot.astype(BF16)
    ones_col = jnp.broadcast_to(jnp.where(lane == 0, 1.0, 0.0), (tm, pad)).astype(BF16)
    kvs = z[:, qd + kw:qd + 2 * kw].astype(BF16)
    kvw = z[:, qd + 2 * kw:qd + 3 * kw].astype(BF16)
    for g in range(NSA_KV_HEADS):
        lo, mid, hi = g * hp, g * hp + dk, (g + 1) * hp
        kc, vc = slice(g * dk, (g + 1) * dk), slice((NSA_KV_HEADS + g) * dk, (NSA_KV_HEADS + g + 1) * dk)
        ks_ref[:, lo:mid] = kvs[:, kc]
        ks_ref[:, mid:hi] = blk_onehot
        vs_ref[:, lo:mid] = kvs[:, vc]
        vs_ref[:, mid:hi] = ones_col
        kw_ref[:, lo:mid] = kvw[:, kc]
        kw_ref[:, mid:hi] = zeros
        vw_ref[:, lo:mid] = kvw[:, vc]
        vw_ref[:, mid:hi] = ones_col
    gate_ref[...] = _sigmoid(z[:, qd + 3 * kw:] + gb_ref[...])


def _nsa_proj(x, mod_l, g, w_in, gate_b):
    b, s, d = x.shape
    tm = min(ROW_TILE, s)
    assert s // SLC_BLOCK <= HEAD_PAD - NSA_DK
    qw = NSA_HEADS * HEAD_PAD
    kvw = NSA_KV_HEADS * HEAD_PAD
    ng = 3 * NSA_HEADS
    row = lambda width: pl.BlockSpec((None, tm, width), lambda i, j: (i, j, 0))
    sds = lambda width, dt=BF16: jax.ShapeDtypeStruct((b, s, width), dt)
    return pl.pallas_call(
        functools.partial(_nsa_proj_kernel, q_scale=NSA_DK ** -0.5 * LOG2E),
        out_shape=(sds(qw), sds(NSA_KV_W), sds(kvw), sds(kvw), sds(kvw), sds(kvw), sds(ng, F32)),
        grid=(b, s // tm),
        in_specs=[
            row(d),
            pl.BlockSpec((None, N_ADA, d), lambda i, j: (i, 0, 0)),
            _const_spec((1, d)),
            _const_spec(w_in.shape),
            _const_spec((1, ng)),
        ],
        out_specs=(row(qw), row(NSA_KV_W), row(kvw), row(kvw), row(kvw), row(kvw), row(ng)),
        compiler_params=_cparams(("arbitrary", "arbitrary")),
        name="nsa_proj",
    )(x, mod_l, g.reshape(1, d), w_in, gate_b.reshape(1, ng))


def _nsa_cmp_kernel(ch_ref, w1a_ref, w1b_ref, pea_ref, peb_ref, w2_ref, o_ref):
    ch = ch_ref[...]
    n = ch.shape[0]
    first = _dot(ch, w1a_ref[...])
    second = _dot(ch, w1b_ref[...])
    bias = _dot(pea_ref[...], w1a_ref[...]) + _dot(peb_ref[...], w1b_ref[...])
    hid = first + pltpu.roll(second, n - 1, 0) + bias[0:1, :]
    act = (hid * _sigmoid(hid)).astype(BF16)
    o_ref[...] = _dot(act, w2_ref[...]).astype(BF16)


def _nsa_cmp_weights(cmp_pe, cmp_w1, cmp_w2):
    ncomp = 2 * NSA_KV_HEADS
    half = CMP_BLOCK // 2
    kv_of = np.arange(ncomp) // NSA_KV_HEADS
    eye = jnp.eye(ncomp, dtype=F32)
    w1 = cmp_w1.reshape(2, 2, half, NSA_DK, CMP_HID)[kv_of]
    big1 = jnp.einsum("chldj,ce->hlcdej", w1, eye).reshape(2, half * ncomp * NSA_DK, ncomp * CMP_HID)
    w2 = jnp.pad(cmp_w2[kv_of], ((0, 0), (0, 0), (0, HEAD_PAD - NSA_DK)))
    big2 = jnp.einsum("cjd,ce->cjed", w2, eye).reshape(ncomp * CMP_HID, ncomp * HEAD_PAD)
    pe = cmp_pe.reshape(2, 2, half, NSA_DK)[kv_of]
    pe = jnp.transpose(pe, (1, 2, 0, 3)).reshape(2, 1, half * ncomp * NSA_DK)
    pe = jnp.broadcast_to(pe, (2, 8, half * ncomp * NSA_DK))
    return (big1[0].astype(BF16), big1[1].astype(BF16), pe[0].astype(BF16), pe[1].astype(BF16),
            big2.astype(BF16))


def _nsa_cmp(kvc, w1a, w1b, pea, peb, w2):
    b, s, kw = kvc.shape
    nchunk = s // CMP_STRIDE
    ow = w2.shape[1]
    chunks = kvc.reshape(b, nchunk, CMP_STRIDE * kw)
    return pl.pallas_call(
        _nsa_cmp_kernel,
        out_shape=jax.ShapeDtypeStruct((b, nchunk, ow), BF16),
        grid=(b,),
        in_specs=[
            pl.BlockSpec((None, nchunk, CMP_STRIDE * kw), lambda i: (i, 0, 0)),
            _const_spec(w1a.shape), _const_spec(w1b.shape),
            _const_spec(pea.shape), _const_spec(peb.shape), _const_spec(w2.shape),
        ],
        out_specs=pl.BlockSpec((None, nchunk, ow), lambda i: (i, 0, 0)),
        compiler_params=_cparams(("arbitrary",)),
        name="nsa_cmp",
    )(chunks, w1a, w1b, pea, peb, w2)


def _add_per_head(s, bias, rep):
    tq = bias.shape[0]
    return jnp.concatenate([s[r * tq:(r + 1) * tq] + bias for r in range(rep)], axis=0)


def _nsa_attn_kernel(q_ref, kc_ref, ks_ref, vs_ref, kw_ref, vw_ref, gate_ref, aggt_ref, placet_ref,
                     o_ref, *, tk, n_sel):
    tq = q_ref.shape[0]
    seq = ks_ref.shape[0]
    ncp = kc_ref.shape[0]
    ns = aggt_ref.shape[0]
    hp, dk, rep = HEAD_PAD, NSA_DK, NSA_GROUP
    q0 = pl.program_id(1) * tq
    t = q0 + lax.broadcasted_iota(jnp.int32, (tq, 1), 0)
    t_rows = jnp.concatenate([t] * rep, axis=0)
    t_lane = q0 + lax.broadcasted_iota(jnp.int32, (1, tq), 1)

    blk = lax.broadcasted_iota(jnp.int32, (ns, 1), 0)
    cur = lax.shift_right_logical(t_lane, int(np.log2(SLC_BLOCK)))
    forced = (blk == 0) | (blk == cur) | (blk == cur - 1)
    causal_blk = blk * SLC_BLOCK <= t_lane
    cmp_end = lax.broadcasted_iota(jnp.int32, (1, ncp), 1) * CMP_STRIDE + (CMP_BLOCK - 1)
    cmp_mask = (cmp_end <= t_rows) & (cmp_end < seq)
    win_len = WINDOW + tq
    w_start = pl.multiple_of(jnp.maximum(q0 - WINDOW, 0), tq)
    wpos = w_start + lax.broadcasted_iota(jnp.int32, (1, win_len), 1)
    win_bias = jnp.where((wpos <= t) & (wpos > t - WINDOW), 0.0, -MASK_BIG)
    n_chunks = (q0 + tq + tk - 1) // tk
    last_pos = (n_chunks - 1) * tk + lax.broadcasted_iota(jnp.int32, (1, tk), 1)
    last_bias = jnp.where(last_pos <= t, 0.0, -MASK_BIG)
    pad_lane = lax.broadcasted_iota(jnp.int32, (1, hp), 1)
    sel_offset = jnp.where((pad_lane >= dk) & (pad_lane < dk + ns), MASK_BIG, 0.0)
    gates = gate_ref[...]

    kv_cols = [slice(g * hp, (g + 1) * hp) for g in range(NSA_KV_HEADS)]
    qs_g, q_sel_g, o_c_g = [], [], []
    for g in range(NSA_KV_HEADS):
        q_heads = [q_ref[:, (g * rep + r) * hp:(g * rep + r + 1) * hp] for r in range(rep)]
        qs = jnp.concatenate(q_heads, axis=0)

        s = jnp.where(cmp_mask, _dot_nt(qs, kc_ref[:, kv_cols[g]]), NEG_INF)
        p = jnp.where(cmp_mask, jnp.exp2(s - jnp.max(s, axis=-1, keepdims=True)), 0.0)
        p_c = p * _row_recip(jnp.sum(p, axis=-1, keepdims=True))
        cv_cols = slice((NSA_KV_HEADS + g) * hp, (NSA_KV_HEADS + g + 1) * hp)
        o_c = _dot(p_c.astype(BF16), kc_ref[:, cv_cols])

        p_sum = p_c[0:tq]
        for r in range(1, rep):
            p_sum = p_sum + p_c[r * tq:(r + 1) * tq]
        agg_t = aggt_ref[...]
        hi, mid, lo = _split3(p_sum)
        imp = _dot_nt(agg_t, hi) + _dot_nt(agg_t, mid) + _dot_nt(agg_t, lo)
        val = jnp.where(forced, FORCE_SCORE, jnp.where(causal_blk, imp, NEG_INF))
        rank = jnp.zeros((ns, tq), F32)
        for i in range(ns):
            other = val[i:i + 1, :]
            beats = (other > val) | ((other == val) & (blk > i))
            rank = rank + jnp.where(beats, 1.0, 0.0)
        sel = jnp.where((rank < n_sel) & causal_blk, 1.0, 0.0).astype(BF16)
        sel_pad = _dot(placet_ref[...], sel).T - sel_offset
        q_sel = jnp.concatenate([(qh.astype(F32) + sel_pad).astype(BF16) for qh in q_heads], axis=0)
        qs_g.append(qs)
        q_sel_g.append(q_sel)
        o_c_g.append(o_c)

    def chunk(c, carry, last):
        start = pl.multiple_of(c * tk, tk)
        out = []
        for g in range(NSA_KV_HEADS):
            s = _dot_nt(q_sel_g[g], ks_ref[pl.ds(start, tk), kv_cols[g]])
            if last:
                s = _add_per_head(s, last_bias, rep)
            out.append(_online_step(s, vs_ref[pl.ds(start, tk), kv_cols[g]], *carry[g]))
        return tuple(out)

    init = tuple((jnp.full((rep * tq, 1), NEG_INF, F32), jnp.zeros((rep * tq, hp), F32))
                 for _ in range(NSA_KV_HEADS))
    carry = lax.fori_loop(0, n_chunks - 1, functools.partial(chunk, last=False), init)
    sel_out = chunk(n_chunks - 1, carry, True)

    for g in range(NSA_KV_HEADS):
        o_c = o_c_g[g]
        acc_s = sel_out[g][1]
        r_s = _row_recip(acc_s[:, dk:dk + 1])

        s = _add_per_head(_dot_nt(qs_g[g], kw_ref[pl.ds(w_start, win_len), kv_cols[g]]), win_bias, rep)
        p = jnp.exp2(s - jnp.max(s, axis=-1, keepdims=True))
        acc_w = _dot(p.astype(BF16), vw_ref[pl.ds(w_start, win_len), kv_cols[g]])
        r_w = _row_recip(acc_w[:, dk:dk + 1])

        for r in range(rep):
            rows = slice(r * tq, (r + 1) * tq)
            gc = g * 3 * rep + r
            o_h = (gates[:, gc:gc + 1] * o_c[rows]
                   + (gates[:, gc + rep:gc + rep + 1] * r_s[rows]) * acc_s[rows]
                   + (gates[:, gc + 2 * rep:gc + 2 * rep + 1] * r_w[rows]) * acc_w[rows])
            h = g * rep + r
            o_ref[:, h * dk:(h + 1) * dk] = o_h[:, :dk].astype(BF16)


def _nsa_tables(seq):
    nchunk = seq // CMP_STRIDE
    ns = seq // SLC_BLOCK
    cs = np.arange(nchunk)[:, None] * CMP_STRIDE
    ss = np.arange(ns)[None, :] * SLC_BLOCK
    overlap = np.clip(np.minimum(cs + CMP_BLOCK, ss + SLC_BLOCK) - np.maximum(cs, ss), 0, None)
    agg_t = (overlap.astype(np.float32) / CMP_BLOCK).T
    place_t = np.zeros((HEAD_PAD, ns), np.float32)
    place_t[NSA_DK + np.arange(ns), np.arange(ns)] = MASK_BIG
    return jnp.asarray(agg_t, BF16), jnp.asarray(place_t, BF16)


def _nsa_attn(q, kc, ks, vs, kw, vw, gates):
    b, s, qw = q.shape
    tq = min(NSA_TQ, s)
    tk = min(NSA_TK, s)
    assert s >= WINDOW + tq and s % tk == 0 and WINDOW % tq == 0
    agg_t, place_t = _nsa_tables(s)
    whole = lambda a: pl.BlockSpec((None,) + a.shape[1:], lambda i, j: (i, 0, 0))
    ow = NSA_HEADS * NSA_DK
    return pl.pallas_call(
        functools.partial(_nsa_attn_kernel, tk=tk, n_sel=min(N_SEL, s // SLC_BLOCK)),
        out_shape=jax.ShapeDtypeStruct((b, s, ow), BF16),
        grid=(b, s // tq),
        in_specs=[
            pl.BlockSpec((None, tq, qw), lambda i, j: (i, j, 0)),
            whole(kc), whole(ks), whole(vs), whole(kw), whole(vw),
            pl.BlockSpec((None, tq, gates.shape[-1]), lambda i, j: (i, j, 0)),
            _const_spec(agg_t.shape), _const_spec(place_t.shape),
        ],
        out_specs=pl.BlockSpec((None, tq, ow), lambda i, j: (i, j, 0)),
        compiler_params=_cparams(("arbitrary", "arbitrary")),
        name="nsa_attn",
    )(q, kc, ks, vs, kw, vw, gates, agg_t, place_t)


def _nsa_gate_layout(w_in, gate_b):
    qd = NSA_HEADS * NSA_DK
    o = qd + 3 * NSA_KV_W
    perm = np.arange(3 * NSA_HEADS).reshape(NSA_KV_HEADS, NSA_GROUP, 3).transpose(0, 2, 1).reshape(-1)
    w_in = jnp.concatenate([w_in[:, :o], w_in[:, o:][:, perm]], axis=1)
    return w_in, gate_b[perm]


def kernel(x, c, positions, ada_w, ada_b, norm_g, final_g, ff_w13, ff_w2, hy_w_in, hy_conv_w,
           hy_q_norm, hy_kv_norm, hy_w_uq, hy_w_ukv, hy_w_out, nsa_w_in, nsa_cmp_pe, nsa_cmp_w1,
           nsa_cmp_w2, nsa_gate_b, nsa_w_out):
    depth = ada_w.shape[0]
    mod = _ada_mod(c, ada_w, ada_b)
    ropes = _rope_tables(positions)
    w13 = ff_w13.astype(BF16)
    w2 = ff_w2.astype(BF16)
    for l in range(depth):
        m = l // 2
        x = _ffn(x, mod[l], norm_g[l, 0], w13[l, 0], w2[l, 0], final_g, k0=0, final=False)
        if l % 2 == 0:
            w_in = hy_w_in[m].astype(BF16)
            w_in = jnp.pad(w_in, ((0, 0), (0, -w_in.shape[1] % HEAD_PAD)))
            y_conv, q, k, v = _hy_proj(x, mod[l], norm_g[l, 1], w_in,
                                       hy_conv_w[m], hy_q_norm[m], hy_kv_norm[m],
                                       _hy_weights(hy_w_uq[m], hy_w_ukv[m]), ropes)
            y_att = _mla_attn(q, k, v)
            w_out = hy_w_out[m].astype(BF16)
            x = _out_proj(x, mod[l], [y_conv, y_att], [w_out[:CONV_WIDTH], w_out[CONV_WIDTH:]])
        else:
            w_in, gate_b = _nsa_gate_layout(nsa_w_in[m], nsa_gate_b[m])
            q, kvc, ks, vs, kw, vw, gates = _nsa_proj(x, mod[l], norm_g[l, 1], w_in.astype(BF16), gate_b)
            kc = _nsa_cmp(kvc, *_nsa_cmp_weights(nsa_cmp_pe[m], nsa_cmp_w1[m], nsa_cmp_w2[m]))
            o = _nsa_attn(q, kc, ks, vs, kw, vw, gates)
            x = _out_proj(x, mod[l], [o], [nsa_w_out[m].astype(BF16)])
        x = _ffn(x, mod[l], norm_g[l, 2], w13[l, 1], w2[l, 1], final_g, k0=6, final=(l == depth - 1))
    return x
```

```python
import functools

import jax
import jax.numpy as jnp
import numpy as np
from jax import lax
from jax.experimental import pallas as pl
from jax.experimental.pallas import tpu as pltpu

F32 = jnp.float32
BF16 = jnp.bfloat16

N_ADA = 9
EPS = 1e-6
NEG_INF = -1e30
CONV_WIDTH = 512
CONV_TAPS = 3
MLA_HEADS = 8
MLA_NOPE = 64
MLA_ROPE = 32
MLA_V = 64
Q_LORA = 256
KV_LORA = 128
ROPE_THETA = 10000.0
NSA_HEADS = 16
NSA_KV_HEADS = 2
NSA_GROUP = NSA_HEADS // NSA_KV_HEADS
NSA_DK = 64
CMP_BLOCK = 32
CMP_STRIDE = 16
CMP_HID = 128
SLC_BLOCK = 64
N_SEL = 8
WINDOW = 512
FORCE_SCORE = 1e4
NSA_KV_W = 2 * NSA_KV_HEADS * NSA_DK

HEAD_PAD = 128
VMEM_LIMIT = 56 * 1024 * 1024
ROW_TILE = 512
MLA_TQ = 512
MLA_HEADS_PER_STEP = 4
NSA_TQ = 128
NSA_TK = 512

LOG2E = float(np.log2(np.e))
MASK_BIG = float(2.0 ** 100)


def _cparams(sem):
    return pltpu.CompilerParams(dimension_semantics=sem, vmem_limit_bytes=VMEM_LIMIT)


def _const_spec(shape):
    nd = len(shape)
    return pl.BlockSpec(shape, lambda *_: (0,) * nd, pipeline_mode=pl.Buffered(1))


def _sigmoid(v):
    return 1.0 / (1.0 + jnp.exp(-v))


def _rms(v, g):
    return v * lax.rsqrt(jnp.mean(v * v, axis=-1, keepdims=True) + EPS) * g


def _modulate(x, g, mod_ref, k0):
    shift = mod_ref[k0:k0 + 1, :]
    scale = mod_ref[k0 + 1:k0 + 2, :]
    return _rms(x, g) * (1.0 + scale) + shift


def _dot(a, b):
    return jnp.dot(a, b, preferred_element_type=F32)


def _dot_nt(a, b):
    return lax.dot_general(a, b, (((1,), (1,)), ((), ())), preferred_element_type=F32)


def _split3(a):
    hi = a.astype(BF16)
    r1 = a - hi.astype(F32)
    mid = r1.astype(BF16)
    lo = (r1 - mid.astype(F32)).astype(BF16)
    return hi, mid, lo


def _row_recip(v):
    return 1.0 / jnp.maximum(v, 1e-20)


def _ada_kernel(c_ref, w_ref, b_ref, o_ref):
    c = c_ref[...]
    ca = (c * _sigmoid(c)).astype(BF16)
    o_ref[...] = _dot(ca, w_ref[...].astype(BF16)) + b_ref[...]


def _ada_mod(c, ada_w, ada_b):
    depth, d, n = ada_w.shape
    b = c.shape[0]
    tn = n // 8
    out = pl.pallas_call(
        _ada_kernel,
        out_shape=jax.ShapeDtypeStruct((depth, b, n), F32),
        grid=(depth, n // tn),
        in_specs=[
            pl.BlockSpec((b, d), lambda l, j: (0, 0)),
            pl.BlockSpec((None, d, tn), lambda l, j: (l, 0, j)),
            pl.BlockSpec((None, 1, tn), lambda l, j: (l, 0, j)),
        ],
        out_specs=pl.BlockSpec((None, b, tn), lambda l, j: (l, 0, j)),
        compiler_params=_cparams(("arbitrary", "arbitrary")),
        name="ada_mod",
    )(c, ada_w, ada_b.reshape(depth, 1, n))
    return out.reshape(depth, b, N_ADA, d)


def _ffn_kernel(x_ref, mod_ref, g_ref, w13_ref, w2_ref, fg_ref, o_ref, *, k0, d_ff, final):
    x = x_ref[...]
    h = _modulate(x, g_ref[...], mod_ref, k0).astype(BF16)
    ab = _dot(h, w13_ref[...])
    a = ab[:, :d_ff]
    b = ab[:, d_ff:]
    u = (a * _sigmoid(a) * b).astype(BF16)
    y = _dot(u, w2_ref[...])
    out = x + (0.5 * mod_ref[k0 + 2:k0 + 3, :]) * y
    if final:
        out = _rms(out, fg_ref[...])
    o_ref[...] = out


def _ffn(x, mod_l, g, w13, w2, layer, which, final_g, *, k0, final):
    b, s, d = x.shape
    d_ff = w2.shape[2]
    tm = min(ROW_TILE, s)
    kern = functools.partial(_ffn_kernel, k0=k0, d_ff=d_ff, final=final)
    pick = lambda *_: (layer, which, 0, 0)
    return pl.pallas_call(
        kern,
        out_shape=jax.ShapeDtypeStruct(x.shape, F32),
        grid=(b, s // tm),
        in_specs=[
            pl.BlockSpec((None, tm, d), lambda i, j: (i, j, 0)),
            pl.BlockSpec((None, N_ADA, d), lambda i, j: (i, 0, 0)),
            _const_spec((1, d)),
            pl.BlockSpec((None, None, d, 2 * d_ff), pick, pipeline_mode=pl.Buffered(1)),
            pl.BlockSpec((None, None, d_ff, d), pick, pipeline_mode=pl.Buffered(1)),
            _const_spec((1, d)),
        ],
        out_specs=pl.BlockSpec((None, tm, d), lambda i, j: (i, j, 0)),
        compiler_params=_cparams(("arbitrary", "arbitrary")),
        name="ffn",
    )(x, mod_l, g.reshape(1, d), w13, w2, final_g.reshape(1, d))


def _out_proj_kernel(*refs, n_parts):
    x_ref, mod_ref = refs[0], refs[1]
    part_refs = refs[2:2 + n_parts]
    w_refs = refs[2 + n_parts:2 + 2 * n_parts]
    o_ref = refs[2 + 2 * n_parts]
    y = _dot(part_refs[0][...], w_refs[0][...])
    for p_ref, w_ref in zip(part_refs[1:], w_refs[1:]):
        y = y + _dot(p_ref[...], w_ref[...])
    o_ref[...] = x_ref[...] + mod_ref[5:6, :] * y


def _out_proj(x, mod_l, parts, w_parts):
    b, s, d = x.shape
    tm = min(ROW_TILE, s)
    n_parts = len(parts)
    in_specs = [
        pl.BlockSpec((None, tm, d), lambda i, j: (i, j, 0)),
        pl.BlockSpec((None, N_ADA, d), lambda i, j: (i, 0, 0)),
    ]
    in_specs += [pl.BlockSpec((None, tm, p.shape[-1]), lambda i, j: (i, j, 0)) for p in parts]
    in_specs += [_const_spec(w.shape) for w in w_parts]
    return pl.pallas_call(
        functools.partial(_out_proj_kernel, n_parts=n_parts),
        out_shape=jax.ShapeDtypeStruct(x.shape, F32),
        grid=(b, s // tm),
        in_specs=in_specs,
        out_specs=pl.BlockSpec((None, tm, d), lambda i, j: (i, j, 0)),
        compiler_params=_cparams(("arbitrary", "arbitrary")),
        name="out_proj",
    )(x, mod_l, *parts, *w_parts)


def _rope_swap(t):
    width = t.shape[-1]
    half = MLA_ROPE // 2
    lane = lax.broadcasted_iota(jnp.int32, (1, width), 1) & (HEAD_PAD - 1)
    return jnp.where(lane < MLA_NOPE + half, pltpu.roll(t, width - half, 1), pltpu.roll(t, half, 1))


def _hy_proj_kernel(x_ref, mod_ref, g_ref, win_ref, convw_ref, qn_ref, kvn_ref, wq_ref,
                    wk_ref, wv_ref, rc_ref, rs_ref,
                    yconv_ref, q_ref, k_ref, v_ref, carry_ref, *, q_scale):
    tm = x_ref.shape[0]
    cw = CONV_WIDTH
    h = _modulate(x_ref[...], g_ref[...], mod_ref, 3).astype(BF16)
    z = _dot(h, win_ref[...])
    u, gate_c, gate_b = z[:, :cw], z[:, cw:2 * cw], z[:, 2 * cw:3 * cw]
    o = 3 * cw
    cq = z[:, o:o + Q_LORA]
    ckv = z[:, o + Q_LORA:o + Q_LORA + KV_LORA]
    kr_tile = z[:, o + Q_LORA + KV_LORA:]

    @pl.when(pl.program_id(1) == 0)
    def _():
        carry_ref[...] = jnp.zeros_like(carry_ref)

    v = gate_c * u
    prev = carry_ref[...]
    row = lax.broadcasted_iota(jnp.int32, (tm, 1), 0)
    v1 = jnp.where(row == 0, prev[7:8, :], pltpu.roll(v, 1, 0))
    v2 = jnp.where(row == 0, prev[6:7, :], jnp.where(row == 1, prev[7:8, :], pltpu.roll(v, 2, 0)))
    w = convw_ref[...]
    yconv_ref[...] = (gate_b * (w[0:1, :] * v2 + w[1:2, :] * v1 + w[2:3, :] * v)).astype(BF16)
    carry_ref[...] = v[tm - 8:, :]

    qn = _rms(cq, qn_ref[...]).astype(BF16)
    kvn = _rms(ckv, kvn_ref[...]).astype(BF16)
    n_rep = q_ref.shape[-1] // HEAD_PAD
    rc = rc_ref[...]
    rs = rs_ref[...]
    q = _dot(qn, wq_ref[...])
    q = q * jnp.concatenate([rc] * n_rep, axis=1) + _rope_swap(q) * jnp.concatenate([rs] * n_rep, axis=1)
    kr = pltpu.roll(kr_tile, MLA_NOPE, 1)
    kr = kr * rc + _rope_swap(kr) * rs
    k = _dot(kvn, wk_ref[...]) + jnp.concatenate([kr] * n_rep, axis=1)
    q_ref[...] = (q * q_scale).astype(BF16)
    k_ref[...] = k.astype(BF16)
    lane = lax.broadcasted_iota(jnp.int32, (1, v_ref.shape[-1]), 1)
    ones_col = jnp.where((lane & (HEAD_PAD - 1)) == MLA_V, 1.0, 0.0)
    v_ref[...] = (_dot(kvn, wv_ref[...]) + ones_col).astype(BF16)


def _hy_weights(w_uq, w_ukv):
    hp = HEAD_PAD
    pad_heads = lambda w: jnp.pad(w, ((0, 0), (0, 0), (0, hp - w.shape[-1]))).reshape(w.shape[0], -1)
    wq = pad_heads(w_uq.reshape(Q_LORA, MLA_HEADS, MLA_NOPE + MLA_ROPE))
    wkv = w_ukv.reshape(KV_LORA, MLA_HEADS, MLA_NOPE + MLA_V)
    wk = pad_heads(wkv[..., :MLA_NOPE])
    wv = pad_heads(wkv[..., MLA_NOPE:])
    return wq.astype(BF16), wk.astype(BF16), wv.astype(BF16)


def _rope_tables(positions):
    half = MLA_ROPE // 2
    inv = ROPE_THETA ** (-jnp.arange(half, dtype=F32) / half)
    ang = positions.astype(F32)[..., None] * inv
    cos, sin = jnp.cos(ang), jnp.sin(ang)
    lead = positions.shape
    ones = jnp.ones(lead + (MLA_NOPE,), F32)
    ztail = jnp.zeros(lead + (HEAD_PAD - MLA_NOPE - MLA_ROPE,), F32)
    zhead = jnp.zeros(lead + (MLA_NOPE,), F32)
    rc = jnp.concatenate([ones, cos, cos, ztail], axis=-1)
    rs = jnp.concatenate([zhead, -sin, sin, ztail], axis=-1)
    return rc, rs


def _hy_proj(x, mod_l, g, w_in, conv_w, q_norm, kv_norm, hy_w, ropes):
    b, s, d = x.shape
    tm = min(ROW_TILE, s)
    hp = HEAD_PAD
    qw = MLA_HEADS * hp
    row = lambda width: pl.BlockSpec((None, tm, width), lambda i, j: (i, j, 0))
    q_scale = (MLA_NOPE + MLA_ROPE) ** -0.5 * LOG2E
    return pl.pallas_call(
        functools.partial(_hy_proj_kernel, q_scale=q_scale),
        out_shape=(
            jax.ShapeDtypeStruct((b, s, CONV_WIDTH), BF16),
            jax.ShapeDtypeStruct((b, s, qw), BF16),
            jax.ShapeDtypeStruct((b, s, qw), BF16),
            jax.ShapeDtypeStruct((b, s, qw), BF16),
        ),
        grid=(b, s // tm),
        in_specs=[
            row(d),
            pl.BlockSpec((None, N_ADA, d), lambda i, j: (i, 0, 0)),
            _const_spec((1, d)),
            _const_spec(w_in.shape),
            _const_spec(conv_w.shape),
            _const_spec((1, Q_LORA)),
            _const_spec((1, KV_LORA)),
        ] + [_const_spec(w.shape) for w in hy_w] + [row(hp), row(hp)],
        out_specs=(row(CONV_WIDTH), row(qw), row(qw), row(qw)),
        scratch_shapes=[pltpu.VMEM((8, CONV_WIDTH), F32)],
        compiler_params=_cparams(("arbitrary", "arbitrary")),
        name="hy_proj",
    )(x, mod_l, g.reshape(1, d), w_in, conv_w, q_norm.reshape(1, -1), kv_norm.reshape(1, -1),
      *hy_w, *ropes)


def _prefix_attention(q, k_ref, v_ref, cols, n_main, n_edge, edge_bias):
    edge = slice(n_main, n_main + n_edge)
    s_edge = edge_bias(_dot_nt(q, k_ref[edge, cols]))
    m = jnp.max(s_edge, axis=-1, keepdims=True)
    if n_main == 0:
        return _dot(jnp.exp2(s_edge - m).astype(BF16), v_ref[edge, cols])
    s_main = _dot_nt(q, k_ref[0:n_main, cols])
    m = jnp.maximum(m, jnp.max(s_main, axis=-1, keepdims=True))
    return (_dot(jnp.exp2(s_main - m).astype(BF16), v_ref[0:n_main, cols])
            + _dot(jnp.exp2(s_edge - m).astype(BF16), v_ref[edge, cols]))


def _mla_attn_body(q_ref, k_ref, v_ref, o_ref, n_main):
    tq = q_ref.shape[0]
    hp = HEAD_PAD
    n_heads = q_ref.shape[1] // hp
    rel = (lax.broadcasted_iota(jnp.int32, (1, tq), 1)
           <= lax.broadcasted_iota(jnp.int32, (tq, 1), 0))
    diag_bias = jnp.where(rel, 0.0, -MASK_BIG)
    outs = []
    for hh in range(n_heads):
        cols = slice(hh * hp, (hh + 1) * hp)
        acc = _prefix_attention(q_ref[:, cols], k_ref, v_ref, cols, n_main, tq, lambda s: s + diag_bias)
        outs.append(acc * _row_recip(acc[:, MLA_V:MLA_V + 1]))
    lane = lax.broadcasted_iota(jnp.int32, (1, hp), 1)
    for pair in range(n_heads // 2):
        both = jnp.where(lane < MLA_V, outs[2 * pair], pltpu.roll(outs[2 * pair + 1], MLA_V, 1))
        o_ref[:, pair * hp:(pair + 1) * hp] = both.astype(BF16)


def _mla_attn_kernel(q_ref, k_ref, v_ref, o_ref):
    tq = q_ref.shape[0]
    qi = pl.program_id(2)
    for j in range(k_ref.shape[0] // tq):
        pl.when(qi == j)(functools.partial(_mla_attn_body, q_ref, k_ref, v_ref, o_ref, j * tq))


def _mla_attn(q, k, v):
    b, s, _ = q.shape
    tq = min(MLA_TQ, s)
    qw = MLA_HEADS_PER_STEP * HEAD_PAD
    ow = MLA_HEADS_PER_STEP * MLA_V
    return pl.pallas_call(
        _mla_attn_kernel,
        out_shape=jax.ShapeDtypeStruct((b, s, MLA_HEADS * MLA_V), BF16),
        grid=(b, MLA_HEADS // MLA_HEADS_PER_STEP, s // tq),
        in_specs=[
            pl.BlockSpec((None, tq, qw), lambda i, h, j: (i, j, h)),
            pl.BlockSpec((None, s, qw), lambda i, h, j: (i, 0, h)),
            pl.BlockSpec((None, s, qw), lambda i, h, j: (i, 0, h)),
        ],
        out_specs=pl.BlockSpec((None, tq, ow), lambda i, h, j: (i, j, h)),
        compiler_params=_cparams(("arbitrary", "arbitrary", "arbitrary")),
        name="mla_attn",
    )(q, k, v)


def _nsa_proj_kernel(x_ref, mod_ref, g_ref, win_ref, gb_ref, q_ref, kvc_ref, ks_ref, vs_ref,
                     kw_ref, vw_ref, gate_ref, *, q_scale):
    tm = x_ref.shape[0]
    hp, dk = HEAD_PAD, NSA_DK
    pad = hp - dk
    h = _modulate(x_ref[...], g_ref[...], mod_ref, 3).astype(BF16)
    z = _dot(h, win_ref[...])
    qd = NSA_HEADS * dk
    kw = NSA_KV_W
    zeros = jnp.zeros((tm, pad), BF16)
    qz = (z[:, :qd] * q_scale).astype(BF16)
    for hd in range(NSA_HEADS):
        q_ref[:, hd * hp:hd * hp + dk] = qz[:, hd * dk:(hd + 1) * dk]
        q_ref[:, hd * hp + dk:(hd + 1) * hp] = zeros
    kvc_ref[...] = z[:, qd:qd + kw].astype(BF16)
    pos = pl.program_id(1) * tm + lax.broadcasted_iota(jnp.int32, (tm, 1), 0)
    lane = lax.broadcasted_iota(jnp.int32, (1, pad), 1)
    blk_onehot = jnp.where(lax.shift_right_logical(pos, int(np.log2(SLC_BLOCK))) == lane, 1.0, 0.0)
    blk_onehot = blk_onehot.astype(BF16)
    ones_col = jnp.broadcast_to(jnp.where(lane == 0, 1.0, 0.0), (tm, pad)).astype(BF16)
    kvs = z[:, qd + kw:qd + 2 * kw].astype(BF16)
    kvw = z[:, qd + 2 * kw:qd + 3 * kw].astype(BF16)
    for g in range(NSA_KV_HEADS):
        lo, mid, hi = g * hp, g * hp + dk, (g + 1) * hp
        kc, vc = slice(g * dk, (g + 1) * dk), slice((NSA_KV_HEADS + g) * dk, (NSA_KV_HEADS + g + 1) * dk)
        ks_ref[:, lo:mid] = kvs[:, kc]
        ks_ref[:, mid:hi] = blk_onehot
        vs_ref[:, lo:mid] = kvs[:, vc]
        vs_ref[:, mid:hi] = ones_col
        kw_ref[:, lo:mid] = kvw[:, kc]
        kw_ref[:, mid:hi] = zeros
        vw_ref[:, lo:mid] = kvw[:, vc]
        vw_ref[:, mid:hi] = ones_col
    gate_ref[...] = _sigmoid(z[:, qd + 3 * kw:] + gb_ref[...])


def _nsa_proj(x, mod_l, g, w_in, gate_b):
    b, s, d = x.shape
    tm = min(ROW_TILE, s)
    assert s // SLC_BLOCK <= HEAD_PAD - NSA_DK
    qw = NSA_HEADS * HEAD_PAD
    kvw = NSA_KV_HEADS * HEAD_PAD
    ng = 3 * NSA_HEADS
    row = lambda width: pl.BlockSpec((None, tm, width), lambda i, j: (i, j, 0))
    sds = lambda width, dt=BF16: jax.ShapeDtypeStruct((b, s, width), dt)
    return pl.pallas_call(
        functools.partial(_nsa_proj_kernel, q_scale=NSA_DK ** -0.5 * LOG2E),
        out_shape=(sds(qw), sds(NSA_KV_W), sds(kvw), sds(kvw), sds(kvw), sds(kvw), sds(ng, F32)),
        grid=(b, s // tm),
        in_specs=[
            row(d),
            pl.BlockSpec((None, N_ADA, d), lambda i, j: (i, 0, 0)),
            _const_spec((1, d)),
            _const_spec(w_in.shape),
            _const_spec((1, ng)),
        ],
        out_specs=(row(qw), row(NSA_KV_W), row(kvw), row(kvw), row(kvw), row(kvw), row(ng)),
        compiler_params=_cparams(("arbitrary", "arbitrary")),
        name="nsa_proj",
    )(x, mod_l, g.reshape(1, d), w_in, gate_b.reshape(1, ng))


def _nsa_cmp_kernel(ch_ref, w1a_ref, w1b_ref, pea_ref, peb_ref, w2_ref, o_ref):
    ch = ch_ref[...]
    n = ch.shape[0]
    first = _dot(ch, w1a_ref[...])
    second = _dot(ch, w1b_ref[...])
    bias = _dot(pea_ref[...], w1a_ref[...]) + _dot(peb_ref[...], w1b_ref[...])
    hid = first + pltpu.roll(second, n - 1, 0) + bias[0:1, :]
    act = (hid * _sigmoid(hid)).astype(BF16)
    o_ref[...] = _dot(act, w2_ref[...]).astype(BF16)


def _nsa_cmp_weights(cmp_pe, cmp_w1, cmp_w2):
    ncomp = 2 * NSA_KV_HEADS
    half = CMP_BLOCK // 2
    kv_of = np.arange(ncomp) // NSA_KV_HEADS
    eye = jnp.eye(ncomp, dtype=F32)
    w1 = cmp_w1.reshape(2, 2, half, NSA_DK, CMP_HID)[kv_of]
    big1 = jnp.einsum("chldj,ce->hlcdej", w1, eye).reshape(2, half * ncomp * NSA_DK, ncomp * CMP_HID)
    w2 = jnp.pad(cmp_w2[kv_of], ((0, 0), (0, 0), (0, HEAD_PAD - NSA_DK)))
    big2 = jnp.einsum("cjd,ce->cjed", w2, eye).reshape(ncomp * CMP_HID, ncomp * HEAD_PAD)
    pe = cmp_pe.reshape(2, 2, half, NSA_DK)[kv_of]
    pe = jnp.transpose(pe, (1, 2, 0, 3)).reshape(2, 1, half * ncomp * NSA_DK)
    pe = jnp.broadcast_to(pe, (2, 8, half * ncomp * NSA_DK))
    return (big1[0].astype(BF16), big1[1].astype(BF16), pe[0].astype(BF16), pe[1].astype(BF16),
            big2.astype(BF16))


def _nsa_cmp(kvc, w1a, w1b, pea, peb, w2):
    b, s, kw = kvc.shape
    nchunk = s // CMP_STRIDE
    ow = w2.shape[1]
    chunks = kvc.reshape(b, nchunk, CMP_STRIDE * kw)
    return pl.pallas_call(
        _nsa_cmp_kernel,
        out_shape=jax.ShapeDtypeStruct((b, nchunk, ow), BF16),
        grid=(b,),
        in_specs=[
            pl.BlockSpec((None, nchunk, CMP_STRIDE * kw), lambda i: (i, 0, 0)),
            _const_spec(w1a.shape), _const_spec(w1b.shape),
            _const_spec(pea.shape), _const_spec(peb.shape), _const_spec(w2.shape),
        ],
        out_specs=pl.BlockSpec((None, nchunk, ow), lambda i: (i, 0, 0)),
        compiler_params=_cparams(("arbitrary",)),
        name="nsa_cmp",
    )(chunks, w1a, w1b, pea, peb, w2)


def _add_per_head(s, bias, rep):
    tq = bias.shape[0]
    return jnp.concatenate([s[r * tq:(r + 1) * tq] + bias for r in range(rep)], axis=0)


def _nsa_attn_kernel(q_ref, kc_ref, ks_ref, vs_ref, kw_ref, vw_ref, gate_ref, aggt_ref, placet_ref,
                     o_ref, *, tk, n_sel):
    tq = q_ref.shape[0]
    per = tk // tq
    qi = pl.program_id(1)
    for j in range(ks_ref.shape[0] // tk):
        in_class = (qi >= j * per) & (qi < (j + 1) * per)
        pl.when(in_class)(functools.partial(
            _nsa_attn_body, q_ref, kc_ref, ks_ref, vs_ref, kw_ref, vw_ref, gate_ref, aggt_ref,
            placet_ref, o_ref, tk=tk, n_sel=n_sel, n_main=j * tk))


def _nsa_attn_body(q_ref, kc_ref, ks_ref, vs_ref, kw_ref, vw_ref, gate_ref, aggt_ref, placet_ref,
                   o_ref, *, tk, n_sel, n_main):
    tq = q_ref.shape[0]
    seq = ks_ref.shape[0]
    ncp = kc_ref.shape[0]
    ns = aggt_ref.shape[0]
    hp, dk, rep = HEAD_PAD, NSA_DK, NSA_GROUP
    q0 = pl.program_id(1) * tq
    t = q0 + lax.broadcasted_iota(jnp.int32, (tq, 1), 0)
    t_rows = jnp.concatenate([t] * rep, axis=0)
    t_lane = q0 + lax.broadcasted_iota(jnp.int32, (1, tq), 1)

    blk = lax.broadcasted_iota(jnp.int32, (ns, 1), 0)
    cur = lax.shift_right_logical(t_lane, int(np.log2(SLC_BLOCK)))
    forced = (blk == 0) | (blk == cur) | (blk == cur - 1)
    causal_blk = blk * SLC_BLOCK <= t_lane
    cmp_end = lax.broadcasted_iota(jnp.int32, (1, ncp), 1) * CMP_STRIDE + (CMP_BLOCK - 1)
    cmp_mask = (cmp_end <= t_rows) & (cmp_end < seq)
    win_len = WINDOW + tq
    w_start = pl.multiple_of(jnp.maximum(q0 - WINDOW, 0), tq)
    wpos = w_start + lax.broadcasted_iota(jnp.int32, (1, win_len), 1)
    win_bias = jnp.where((wpos <= t) & (wpos > t - WINDOW), 0.0, -MASK_BIG)
    last_pos = n_main + lax.broadcasted_iota(jnp.int32, (1, tk), 1)
    last_bias = jnp.where(last_pos <= t, 0.0, -MASK_BIG)
    pad_lane = lax.broadcasted_iota(jnp.int32, (1, hp), 1)
    sel_offset = jnp.where((pad_lane >= dk) & (pad_lane < dk + ns), MASK_BIG, 0.0)
    gates = gate_ref[...]

    kv_cols = [slice(g * hp, (g + 1) * hp) for g in range(NSA_KV_HEADS)]
    qs_g, q_sel_g, o_c_g = [], [], []
    for g in range(NSA_KV_HEADS):
        q_heads = [q_ref[:, (g * rep + r) * hp:(g * rep + r + 1) * hp] for r in range(rep)]
        qs = jnp.concatenate(q_heads, axis=0)

        s = jnp.where(cmp_mask, _dot_nt(qs, kc_ref[:, kv_cols[g]]), NEG_INF)
        p = jnp.where(cmp_mask, jnp.exp2(s - jnp.max(s, axis=-1, keepdims=True)), 0.0)
        p_c = p * _row_recip(jnp.sum(p, axis=-1, keepdims=True))
        cv_cols = slice((NSA_KV_HEADS + g) * hp, (NSA_KV_HEADS + g + 1) * hp)
        o_c = _dot(p_c.astype(BF16), kc_ref[:, cv_cols])

        p_sum = p_c[0:tq]
        for r in range(1, rep):
            p_sum = p_sum + p_c[r * tq:(r + 1) * tq]
        agg_t = aggt_ref[...]
        hi, mid, lo = _split3(p_sum)
        imp = _dot_nt(agg_t, hi) + _dot_nt(agg_t, mid) + _dot_nt(agg_t, lo)
        val = jnp.where(forced, FORCE_SCORE, jnp.where(causal_blk, imp, NEG_INF))
        rank = jnp.zeros((ns, tq), F32)
        for i in range(ns):
            other = val[i:i + 1, :]
            beats = (other > val) | ((other == val) & (blk > i))
            rank = rank + jnp.where(beats, 1.0, 0.0)
        sel = jnp.where((rank < n_sel) & causal_blk, 1.0, 0.0).astype(BF16)
        sel_pad = _dot(placet_ref[...], sel).T - sel_offset
        q_sel = jnp.concatenate([(qh.astype(F32) + sel_pad).astype(BF16) for qh in q_heads], axis=0)
        qs_g.append(qs)
        q_sel_g.append(q_sel)
        o_c_g.append(o_c)

    for g in range(NSA_KV_HEADS):
        o_c = o_c_g[g]
        acc_s = _prefix_attention(q_sel_g[g], ks_ref, vs_ref, kv_cols[g], n_main, tk,
                                  lambda s: _add_per_head(s, last_bias, rep))
        r_s = _row_recip(acc_s[:, dk:dk + 1])

        s = _add_per_head(_dot_nt(qs_g[g], kw_ref[pl.ds(w_start, win_len), kv_cols[g]]), win_bias, rep)
        p = jnp.exp2(s - jnp.max(s, axis=-1, keepdims=True))
        acc_w = _dot(p.astype(BF16), vw_ref[pl.ds(w_start, win_len), kv_cols[g]])
        r_w = _row_recip(acc_w[:, dk:dk + 1])

        for r in range(rep):
            rows = slice(r * tq, (r + 1) * tq)
            gc = g * 3 * rep + r
            o_h = (gates[:, gc:gc + 1] * o_c[rows]
                   + (gates[:, gc + rep:gc + rep + 1] * r_s[rows]) * acc_s[rows]
                   + (gates[:, gc + 2 * rep:gc + 2 * rep + 1] * r_w[rows]) * acc_w[rows])
            h = g * rep + r
            o_ref[:, h * dk:(h + 1) * dk] = o_h[:, :dk].astype(BF16)


def _nsa_tables(seq):
    nchunk = seq // CMP_STRIDE
    ns = seq // SLC_BLOCK
    cs = np.arange(nchunk)[:, None] * CMP_STRIDE
    ss = np.arange(ns)[None, :] * SLC_BLOCK
    overlap = np.clip(np.minimum(cs + CMP_BLOCK, ss + SLC_BLOCK) - np.maximum(cs, ss), 0, None)
    agg_t = (overlap.astype(np.float32) / CMP_BLOCK).T
    place_t = np.zeros((HEAD_PAD, ns), np.float32)
    place_t[NSA_DK + np.arange(ns), np.arange(ns)] = MASK_BIG
    return jnp.asarray(agg_t, BF16), jnp.asarray(place_t, BF16)


def _nsa_attn(q, kc, ks, vs, kw, vw, gates):
    b, s, qw = q.shape
    tq = min(NSA_TQ, s)
    tk = min(NSA_TK, s)
    assert s >= WINDOW + tq and s % tk == 0 and WINDOW % tq == 0
    agg_t, place_t = _nsa_tables(s)
    whole = lambda a: pl.BlockSpec((None,) + a.shape[1:], lambda i, j: (i, 0, 0))
    ow = NSA_HEADS * NSA_DK
    return pl.pallas_call(
        functools.partial(_nsa_attn_kernel, tk=tk, n_sel=min(N_SEL, s // SLC_BLOCK)),
        out_shape=jax.ShapeDtypeStruct((b, s, ow), BF16),
        grid=(b, s // tq),
        in_specs=[
            pl.BlockSpec((None, tq, qw), lambda i, j: (i, j, 0)),
            whole(kc), whole(ks), whole(vs), whole(kw), whole(vw),
            pl.BlockSpec((None, tq, gates.shape[-1]), lambda i, j: (i, j, 0)),
            _const_spec(agg_t.shape), _const_spec(place_t.shape),
        ],
        out_specs=pl.BlockSpec((None, tq, ow), lambda i, j: (i, j, 0)),
        compiler_params=_cparams(("arbitrary", "arbitrary")),
        name="nsa_attn",
    )(q, kc, ks, vs, kw, vw, gates, agg_t, place_t)


def _nsa_gate_layout(w_in, gate_b):
    qd = NSA_HEADS * NSA_DK
    o = qd + 3 * NSA_KV_W
    perm = np.arange(3 * NSA_HEADS).reshape(NSA_KV_HEADS, NSA_GROUP, 3).transpose(0, 2, 1).reshape(-1)
    w_in = jnp.concatenate([w_in[:, :o], w_in[:, o:][:, perm]], axis=1)
    return w_in, gate_b[perm]


def kernel(x, c, positions, ada_w, ada_b, norm_g, final_g, ff_w13, ff_w2, hy_w_in, hy_conv_w,
           hy_q_norm, hy_kv_norm, hy_w_uq, hy_w_ukv, hy_w_out, nsa_w_in, nsa_cmp_pe, nsa_cmp_w1,
           nsa_cmp_w2, nsa_gate_b, nsa_w_out):
    depth = ada_w.shape[0]
    mod = _ada_mod(c, ada_w, ada_b)
    ropes = _rope_tables(positions)
    w13 = ff_w13.astype(BF16)
    w2 = ff_w2.astype(BF16)
    for l in range(depth):
        m = l // 2
        x = _ffn(x, mod[l], norm_g[l, 0], w13, w2, l, 0, final_g, k0=0, final=False)
        if l % 2 == 0:
            w_in = hy_w_in[m].astype(BF16)
            w_in = jnp.pad(w_in, ((0, 0), (0, -w_in.shape[1] % HEAD_PAD)))
            y_conv, q, k, v = _hy_proj(x, mod[l], norm_g[l, 1], w_in,
                                       hy_conv_w[m], hy_q_norm[m], hy_kv_norm[m],
                                       _hy_weights(hy_w_uq[m], hy_w_ukv[m]), ropes)
            y_att = _mla_attn(q, k, v)
            w_out = hy_w_out[m].astype(BF16)
            x = _out_proj(x, mod[l], [y_conv, y_att], [w_out[:CONV_WIDTH], w_out[CONV_WIDTH:]])
        else:
            w_in, gate_b = _nsa_gate_layout(nsa_w_in[m], nsa_gate_b[m])
            q, kvc, ks, vs, kw, vw, gates = _nsa_proj(x, mod[l], norm_g[l, 1], w_in.astype(BF16), gate_b)
            kc = _nsa_cmp(kvc, *_nsa_cmp_weights(nsa_cmp_pe[m], nsa_cmp_w1[m], nsa_cmp_w2[m]))
            o = _nsa_attn(q, kc, ks, vs, kw, vw, gates)
            x = _out_proj(x, mod[l], [o], [nsa_w_out[m].astype(BF16)])
        x = _ffn(x, mod[l], norm_g[l, 2], w13, w2, l, 1, final_g, k0=6, final=(l == depth - 1))
    return x
```

```python
import functools

import jax
import jax.numpy as jnp
import numpy as np
from jax import lax
from jax.experimental import pallas as pl
from jax.experimental.pallas import tpu as pltpu

F32 = jnp.float32
BF16 = jnp.bfloat16

N_ADA = 9
EPS = 1e-6
NEG_INF = -1e30
CONV_WIDTH = 512
CONV_TAPS = 3
MLA_HEADS = 8
MLA_NOPE = 64
MLA_ROPE = 32
MLA_V = 64
Q_LORA = 256
KV_LORA = 128
ROPE_THETA = 10000.0
NSA_HEADS = 16
NSA_KV_HEADS = 2
NSA_GROUP = NSA_HEADS // NSA_KV_HEADS
NSA_DK = 64
CMP_BLOCK = 32
CMP_STRIDE = 16
CMP_HID = 128
SLC_BLOCK = 64
N_SEL = 8
WINDOW = 512
FORCE_SCORE = 1e4
NSA_KV_W = 2 * NSA_KV_HEADS * NSA_DK

HEAD_PAD = 128
VMEM_LIMIT = 56 * 1024 * 1024
ROW_TILE = 512
PROJ_SUBTILES = 2
MLA_TQ = 512
MLA_HEADS_PER_STEP = 4
NSA_TQ = 128
NSA_TK = 512

LOG2E = float(np.log2(np.e))
MASK_BIG = float(2.0 ** 100)


def _cparams(sem):
    return pltpu.CompilerParams(dimension_semantics=sem, vmem_limit_bytes=VMEM_LIMIT)


def _const_spec(shape):
    nd = len(shape)
    return pl.BlockSpec(shape, lambda *_: (0,) * nd, pipeline_mode=pl.Buffered(1))


def _sigmoid(v):
    return 1.0 / (1.0 + jnp.exp(-v))


def _rms(v, g):
    return v * lax.rsqrt(jnp.mean(v * v, axis=-1, keepdims=True) + EPS) * g


def _modulate(x, g, mod_ref, k0):
    shift = mod_ref[k0:k0 + 1, :]
    scale = mod_ref[k0 + 1:k0 + 2, :]
    return _rms(x, g) * (1.0 + scale) + shift


def _dot(a, b):
    return jnp.dot(a, b, preferred_element_type=F32)


def _dot_nt(a, b):
    return lax.dot_general(a, b, (((1,), (1,)), ((), ())), preferred_element_type=F32)


def _split3(a):
    hi = a.astype(BF16)
    r1 = a - hi.astype(F32)
    mid = r1.astype(BF16)
    lo = (r1 - mid.astype(F32)).astype(BF16)
    return hi, mid, lo


def _row_recip(v):
    return 1.0 / jnp.maximum(v, 1e-20)


def _ada_kernel(c_ref, w_ref, b_ref, o_ref):
    c = c_ref[...]
    ca = (c * _sigmoid(c)).astype(BF16)
    o_ref[...] = _dot(ca, w_ref[...].astype(BF16)) + b_ref[...]


def _ada_mod(c, ada_w, ada_b):
    depth, d, n = ada_w.shape
    b = c.shape[0]
    tn = n // 8
    out = pl.pallas_call(
        _ada_kernel,
        out_shape=jax.ShapeDtypeStruct((depth, b, n), F32),
        grid=(depth, n // tn),
        in_specs=[
            pl.BlockSpec((b, d), lambda l, j: (0, 0)),
            pl.BlockSpec((None, d, tn), lambda l, j: (l, 0, j)),
            pl.BlockSpec((None, 1, tn), lambda l, j: (l, 0, j)),
        ],
        out_specs=pl.BlockSpec((None, b, tn), lambda l, j: (l, 0, j)),
        compiler_params=_cparams(("arbitrary", "arbitrary")),
        name="ada_mod",
    )(c, ada_w, ada_b.reshape(depth, 1, n))
    return out.reshape(depth, b, N_ADA, d)


def _ffn_kernel(*refs, k0, d_ff, final, n_parts):
    x_ref, mod_ref, g_ref = refs[:3]
    part_refs = refs[3:3 + n_parts]
    wout_refs = refs[3 + n_parts:3 + 2 * n_parts]
    w13_ref, w2_ref, fg_ref, o_ref = refs[3 + 2 * n_parts:]
    x = x_ref[...]
    if n_parts:
        y = _dot(part_refs[0][...], wout_refs[0][...])
        for p_ref, w_ref in zip(part_refs[1:], wout_refs[1:]):
            y = y + _dot(p_ref[...], w_ref[...])
        x = x + mod_ref[5:6, :] * y
    h = _modulate(x, g_ref[...], mod_ref, k0).astype(BF16)
    ab = _dot(h, w13_ref[...])
    a = ab[:, :d_ff]
    b = ab[:, d_ff:]
    u = (a * _sigmoid(a) * b).astype(BF16)
    y = _dot(u, w2_ref[...])
    out = x + (0.5 * mod_ref[k0 + 2:k0 + 3, :]) * y
    if final:
        out = _rms(out, fg_ref[...])
    o_ref[...] = out


def _ffn(x, mod_l, g, w13, w2, layer, which, final_g, *, k0, final, parts=(), w_parts=()):
    b, s, d = x.shape
    d_ff = w2.shape[2]
    tm = min(ROW_TILE, s)
    kern = functools.partial(_ffn_kernel, k0=k0, d_ff=d_ff, final=final, n_parts=len(parts))
    pick = lambda *_: (layer, which, 0, 0)
    row = lambda width: pl.BlockSpec((None, tm, width), lambda i, j: (i, j, 0))
    return pl.pallas_call(
        kern,
        out_shape=jax.ShapeDtypeStruct(x.shape, F32),
        grid=(b, s // tm),
        in_specs=[
            row(d),
            pl.BlockSpec((None, N_ADA, d), lambda i, j: (i, 0, 0)),
            _const_spec((1, d)),
        ] + [row(p.shape[-1]) for p in parts] + [_const_spec(w.shape) for w in w_parts] + [
            pl.BlockSpec((None, None, d, 2 * d_ff), pick, pipeline_mode=pl.Buffered(1)),
            pl.BlockSpec((None, None, d_ff, d), pick, pipeline_mode=pl.Buffered(1)),
            _const_spec((1, d)),
        ],
        out_specs=row(d),
        compiler_params=_cparams(("arbitrary", "arbitrary")),
        name="ffn",
    )(x, mod_l, g.reshape(1, d), *parts, *w_parts, w13, w2, final_g.reshape(1, d))


def _rope_swap(t):
    width = t.shape[-1]
    half = MLA_ROPE // 2
    lane = lax.broadcasted_iota(jnp.int32, (1, width), 1) & (HEAD_PAD - 1)
    return jnp.where(lane < MLA_NOPE + half, pltpu.roll(t, width - half, 1), pltpu.roll(t, half, 1))


def _hy_proj_kernel(x_ref, mod_ref, g_ref, win_ref, convw_ref, qn_ref, kvn_ref, wq_ref,
                    wk_ref, wv_ref, rc_ref, rs_ref,
                    yconv_ref, q_ref, k_ref, v_ref, carry_ref, *, q_scale):
    tm = x_ref.shape[0]
    cw = CONV_WIDTH

    @pl.when(pl.program_id(1) == 0)
    def _():
        carry_ref[...] = jnp.zeros_like(carry_ref)

    sub = tm // PROJ_SUBTILES
    prev = carry_ref[...]
    for part in range(PROJ_SUBTILES):
        rows = slice(part * sub, (part + 1) * sub)
        h = _modulate(x_ref[rows, :], g_ref[...], mod_ref, 3).astype(BF16)
        z = _dot(h, win_ref[...])
        u, gate_c, gate_b = z[:, :cw], z[:, cw:2 * cw], z[:, 2 * cw:3 * cw]
        o = 3 * cw
        cq = z[:, o:o + Q_LORA]
        ckv = z[:, o + Q_LORA:o + Q_LORA + KV_LORA]
        kr_tile = z[:, o + Q_LORA + KV_LORA:]

        v = gate_c * u
        row = lax.broadcasted_iota(jnp.int32, (sub, 1), 0)
        v1 = jnp.where(row == 0, prev[7:8, :], pltpu.roll(v, 1, 0))
        v2 = jnp.where(row == 0, prev[6:7, :], jnp.where(row == 1, prev[7:8, :], pltpu.roll(v, 2, 0)))
        w = convw_ref[...]
        yconv_ref[rows, :] = (gate_b * (w[0:1, :] * v2 + w[1:2, :] * v1 + w[2:3, :] * v)).astype(BF16)
        prev = v[sub - 8:, :]

        qn = _rms(cq, qn_ref[...]).astype(BF16)
        kvn = _rms(ckv, kvn_ref[...]).astype(BF16)
        n_rep = q_ref.shape[-1] // HEAD_PAD
        rc = rc_ref[rows, :]
        rs = rs_ref[rows, :]
        q = _dot(qn, wq_ref[...])
        q = q * jnp.concatenate([rc] * n_rep, axis=1) + _rope_swap(q) * jnp.concatenate([rs] * n_rep, axis=1)
        kr = pltpu.roll(kr_tile, MLA_NOPE, 1)
        kr = kr * rc + _rope_swap(kr) * rs
        k = _dot(kvn, wk_ref[...]) + jnp.concatenate([kr] * n_rep, axis=1)
        q_ref[rows, :] = (q * q_scale).astype(BF16)
        k_ref[rows, :] = k.astype(BF16)
        lane = lax.broadcasted_iota(jnp.int32, (1, v_ref.shape[-1]), 1)
        ones_col = jnp.where((lane & (HEAD_PAD - 1)) == MLA_V, 1.0, 0.0)
        v_ref[rows, :] = (_dot(kvn, wv_ref[...]) + ones_col).astype(BF16)
    carry_ref[...] = prev


def _hy_weights(w_uq, w_ukv):
    hp = HEAD_PAD
    pad_heads = lambda w: jnp.pad(w, ((0, 0), (0, 0), (0, hp - w.shape[-1]))).reshape(w.shape[0], -1)
    wq = pad_heads(w_uq.reshape(Q_LORA, MLA_HEADS, MLA_NOPE + MLA_ROPE))
    wkv = w_ukv.reshape(KV_LORA, MLA_HEADS, MLA_NOPE + MLA_V)
    wk = pad_heads(wkv[..., :MLA_NOPE])
    wv = pad_heads(wkv[..., MLA_NOPE:])
    return wq.astype(BF16), wk.astype(BF16), wv.astype(BF16)


def _rope_tables(positions):
    half = MLA_ROPE // 2
    inv = ROPE_THETA ** (-jnp.arange(half, dtype=F32) / half)
    ang = positions.astype(F32)[..., None] * inv
    cos, sin = jnp.cos(ang), jnp.sin(ang)
    lead = positions.shape
    ones = jnp.ones(lead + (MLA_NOPE,), F32)
    ztail = jnp.zeros(lead + (HEAD_PAD - MLA_NOPE - MLA_ROPE,), F32)
    zhead = jnp.zeros(lead + (MLA_NOPE,), F32)
    rc = jnp.concatenate([ones, cos, cos, ztail], axis=-1)
    rs = jnp.concatenate([zhead, -sin, sin, ztail], axis=-1)
    return rc, rs


def _hy_proj(x, mod_l, g, w_in, conv_w, q_norm, kv_norm, hy_w, ropes):
    b, s, d = x.shape
    tm = min(ROW_TILE, s)
    hp = HEAD_PAD
    qw = MLA_HEADS * hp
    row = lambda width: pl.BlockSpec((None, tm, width), lambda i, j: (i, j, 0))
    q_scale = (MLA_NOPE + MLA_ROPE) ** -0.5 * LOG2E
    return pl.pallas_call(
        functools.partial(_hy_proj_kernel, q_scale=q_scale),
        out_shape=(
            jax.ShapeDtypeStruct((b, s, CONV_WIDTH), BF16),
            jax.ShapeDtypeStruct((b, s, qw), BF16),
            jax.ShapeDtypeStruct((b, s, qw), BF16),
            jax.ShapeDtypeStruct((b, s, qw), BF16),
        ),
        grid=(b, s // tm),
        in_specs=[
            row(d),
            pl.BlockSpec((None, N_ADA, d), lambda i, j: (i, 0, 0)),
            _const_spec((1, d)),
            _const_spec(w_in.shape),
            _const_spec(conv_w.shape),
            _const_spec((1, Q_LORA)),
            _const_spec((1, KV_LORA)),
        ] + [_const_spec(w.shape) for w in hy_w] + [row(hp), row(hp)],
        out_specs=(row(CONV_WIDTH), row(qw), row(qw), row(qw)),
        scratch_shapes=[pltpu.VMEM((8, CONV_WIDTH), F32)],
        compiler_params=_cparams(("arbitrary", "arbitrary")),
        name="hy_proj",
    )(x, mod_l, g.reshape(1, d), w_in, conv_w, q_norm.reshape(1, -1), kv_norm.reshape(1, -1),
      *hy_w, *ropes)


def _prefix_attention(q, k_ref, v_ref, cols, n_main, n_edge, edge_bias):
    edge = slice(n_main, n_main + n_edge)
    s_edge = edge_bias(_dot_nt(q, k_ref[edge, cols]))
    m = jnp.max(s_edge, axis=-1, keepdims=True)
    if n_main == 0:
        return _dot(jnp.exp2(s_edge - m).astype(BF16), v_ref[edge, cols])
    s_main = _dot_nt(q, k_ref[0:n_main, cols])
    m = jnp.maximum(m, jnp.max(s_main, axis=-1, keepdims=True))
    return (_dot(jnp.exp2(s_main - m).astype(BF16), v_ref[0:n_main, cols])
            + _dot(jnp.exp2(s_edge - m).astype(BF16), v_ref[edge, cols]))


def _mla_attn_body(q_ref, k_ref, v_ref, o_ref, n_main):
    tq = q_ref.shape[0]
    hp = HEAD_PAD
    n_heads = q_ref.shape[1] // hp
    rel = (lax.broadcasted_iota(jnp.int32, (1, tq), 1)
           <= lax.broadcasted_iota(jnp.int32, (tq, 1), 0))
    diag_bias = jnp.where(rel, 0.0, -MASK_BIG)
    outs = []
    for hh in range(n_heads):
        cols = slice(hh * hp, (hh + 1) * hp)
        acc = _prefix_attention(q_ref[:, cols], k_ref, v_ref, cols, n_main, tq, lambda s: s + diag_bias)
        outs.append(acc * _row_recip(acc[:, MLA_V:MLA_V + 1]))
    lane = lax.broadcasted_iota(jnp.int32, (1, hp), 1)
    for pair in range(n_heads // 2):
        both = jnp.where(lane < MLA_V, outs[2 * pair], pltpu.roll(outs[2 * pair + 1], MLA_V, 1))
        o_ref[:, pair * hp:(pair + 1) * hp] = both.astype(BF16)


def _mla_attn_kernel(q_ref, k_ref, v_ref, o_ref):
    tq = q_ref.shape[0]
    qi = pl.program_id(2)
    for j in range(k_ref.shape[0] // tq):
        pl.when(qi == j)(functools.partial(_mla_attn_body, q_ref, k_ref, v_ref, o_ref, j * tq))


def _mla_attn(q, k, v):
    b, s, _ = q.shape
    tq = min(MLA_TQ, s)
    qw = MLA_HEADS_PER_STEP * HEAD_PAD
    ow = MLA_HEADS_PER_STEP * MLA_V
    return pl.pallas_call(
        _mla_attn_kernel,
        out_shape=jax.ShapeDtypeStruct((b, s, MLA_HEADS * MLA_V), BF16),
        grid=(b, MLA_HEADS // MLA_HEADS_PER_STEP, s // tq),
        in_specs=[
            pl.BlockSpec((None, tq, qw), lambda i, h, j: (i, j, h)),
            pl.BlockSpec((None, s, qw), lambda i, h, j: (i, 0, h)),
            pl.BlockSpec((None, s, qw), lambda i, h, j: (i, 0, h)),
        ],
        out_specs=pl.BlockSpec((None, tq, ow), lambda i, h, j: (i, j, h)),
        compiler_params=_cparams(("arbitrary", "arbitrary", "arbitrary")),
        name="mla_attn",
    )(q, k, v)


def _nsa_proj_kernel(x_ref, mod_ref, g_ref, win_ref, gb_ref, q_ref, kvc_ref, ks_ref, vs_ref,
                     kw_ref, vw_ref, gate_ref, *, q_scale):
    tm = x_ref.shape[0]
    hp, dk = HEAD_PAD, NSA_DK
    pad = hp - dk
    h = _modulate(x_ref[...], g_ref[...], mod_ref, 3).astype(BF16)
    z = _dot(h, win_ref[...])
    qd = NSA_HEADS * dk
    kw = NSA_KV_W
    zeros = jnp.zeros((tm, pad), BF16)
    qz = (z[:, :qd] * q_scale).astype(BF16)
    for hd in range(NSA_HEADS):
        q_ref[:, hd * hp:hd * hp + dk] = qz[:, hd * dk:(hd + 1) * dk]
        q_ref[:, hd * hp + dk:(hd + 1) * hp] = zeros
    kvc_ref[...] = z[:, qd:qd + kw].astype(BF16)
    pos = pl.program_id(1) * tm + lax.broadcasted_iota(jnp.int32, (tm, 1), 0)
    lane = lax.broadcasted_iota(jnp.int32, (1, pad), 1)
    blk_onehot = jnp.where(lax.shift_right_logical(pos, int(np.log2(SLC_BLOCK))) == lane, 1.0, 0.0)
    blk_onehot = blk_onehot.astype(BF16)
    ones_col = jnp.broadcast_to(jnp.where(lane == 0, 1.0, 0.0), (tm, pad)).astype(BF16)
    kvs = z[:, qd + kw:qd + 2 * kw].astype(BF16)
    kvw = z[:, qd + 2 * kw:qd + 3 * kw].astype(BF16)
    for g in range(NSA_KV_HEADS):
        lo, mid, hi = g * hp, g * hp + dk, (g + 1) * hp
        kc, vc = slice(g * dk, (g + 1) * dk), slice((NSA_KV_HEADS + g) * dk, (NSA_KV_HEADS + g + 1) * dk)
        ks_ref[:, lo:mid] = kvs[:, kc]
        ks_ref[:, mid:hi] = blk_onehot
        vs_ref[:, lo:mid] = kvs[:, vc]
        vs_ref[:, mid:hi] = ones_col
        kw_ref[:, lo:mid] = kvw[:, kc]
        kw_ref[:, mid:hi] = zeros
        vw_ref[:, lo:mid] = kvw[:, vc]
        vw_ref[:, mid:hi] = ones_col
    gate_ref[...] = _sigmoid(z[:, qd + 3 * kw:] + gb_ref[...])


def _nsa_proj(x, mod_l, g, w_in, gate_b):
    b, s, d = x.shape
    tm = min(ROW_TILE, s)
    assert s // SLC_BLOCK <= HEAD_PAD - NSA_DK
    qw = NSA_HEADS * HEAD_PAD
    kvw = NSA_KV_HEADS * HEAD_PAD
    ng = 3 * NSA_HEADS
    row = lambda width: pl.BlockSpec((None, tm, width), lambda i, j: (i, j, 0))
    sds = lambda width, dt=BF16: jax.ShapeDtypeStruct((b, s, width), dt)
    return pl.pallas_call(
        functools.partial(_nsa_proj_kernel, q_scale=NSA_DK ** -0.5 * LOG2E),
        out_shape=(sds(qw), sds(NSA_KV_W), sds(kvw), sds(kvw), sds(kvw), sds(kvw), sds(ng, F32)),
        grid=(b, s // tm),
        in_specs=[
            row(d),
            pl.BlockSpec((None, N_ADA, d), lambda i, j: (i, 0, 0)),
            _const_spec((1, d)),
            _const_spec(w_in.shape),
            _const_spec((1, ng)),
        ],
        out_specs=(row(qw), row(NSA_KV_W), row(kvw), row(kvw), row(kvw), row(kvw), row(ng)),
        compiler_params=_cparams(("arbitrary", "arbitrary")),
        name="nsa_proj",
    )(x, mod_l, g.reshape(1, d), w_in, gate_b.reshape(1, ng))


def _nsa_cmp_kernel(ch_ref, w1a_ref, w1b_ref, pea_ref, peb_ref, w2_ref, o_ref):
    ch = ch_ref[...]
    n = ch.shape[0]
    first = _dot(ch, w1a_ref[...])
    second = _dot(ch, w1b_ref[...])
    bias = _dot(pea_ref[...], w1a_ref[...]) + _dot(peb_ref[...], w1b_ref[...])
    hid = first + pltpu.roll(second, n - 1, 0) + bias[0:1, :]
    act = (hid * _sigmoid(hid)).astype(BF16)
    o_ref[...] = _dot(act, w2_ref[...]).astype(BF16)


def _nsa_cmp_weights(cmp_pe, cmp_w1, cmp_w2):
    ncomp = 2 * NSA_KV_HEADS
    half = CMP_BLOCK // 2
    kv_of = np.arange(ncomp) // NSA_KV_HEADS
    eye = jnp.eye(ncomp, dtype=F32)
    w1 = cmp_w1.reshape(2, 2, half, NSA_DK, CMP_HID)[kv_of]
    big1 = jnp.einsum("chldj,ce->hlcdej", w1, eye).reshape(2, half * ncomp * NSA_DK, ncomp * CMP_HID)
    w2 = jnp.pad(cmp_w2[kv_of], ((0, 0), (0, 0), (0, HEAD_PAD - NSA_DK)))
    big2 = jnp.einsum("cjd,ce->cjed", w2, eye).reshape(ncomp * CMP_HID, ncomp * HEAD_PAD)
    pe = cmp_pe.reshape(2, 2, half, NSA_DK)[kv_of]
    pe = jnp.transpose(pe, (1, 2, 0, 3)).reshape(2, 1, half * ncomp * NSA_DK)
    pe = jnp.broadcast_to(pe, (2, 8, half * ncomp * NSA_DK))
    return (big1[0].astype(BF16), big1[1].astype(BF16), pe[0].astype(BF16), pe[1].astype(BF16),
            big2.astype(BF16))


def _nsa_cmp(kvc, w1a, w1b, pea, peb, w2):
    b, s, kw = kvc.shape
    nchunk = s // CMP_STRIDE
    ow = w2.shape[1]
    chunks = kvc.reshape(b, nchunk, CMP_STRIDE * kw)
    return pl.pallas_call(
        _nsa_cmp_kernel,
        out_shape=jax.ShapeDtypeStruct((b, nchunk, ow), BF16),
        grid=(b,),
        in_specs=[
            pl.BlockSpec((None, nchunk, CMP_STRIDE * kw), lambda i: (i, 0, 0)),
            _const_spec(w1a.shape), _const_spec(w1b.shape),
            _const_spec(pea.shape), _const_spec(peb.shape), _const_spec(w2.shape),
        ],
        out_specs=pl.BlockSpec((None, nchunk, ow), lambda i: (i, 0, 0)),
        compiler_params=_cparams(("arbitrary",)),
        name="nsa_cmp",
    )(chunks, w1a, w1b, pea, peb, w2)


def _add_per_head(s, bias, rep):
    tq = bias.shape[0]
    return jnp.concatenate([s[r * tq:(r + 1) * tq] + bias for r in range(rep)], axis=0)


def _nsa_attn_kernel(q_ref, kc_ref, ks_ref, vs_ref, kw_ref, vw_ref, gate_ref, aggt_ref, placet_ref,
                     o_ref, *, tk, n_sel):
    tq = q_ref.shape[0]
    per = tk // tq
    qi = pl.program_id(1)
    for j in range(ks_ref.shape[0] // tk):
        in_class = (qi >= j * per) & (qi < (j + 1) * per)
        pl.when(in_class)(functools.partial(
            _nsa_attn_body, q_ref, kc_ref, ks_ref, vs_ref, kw_ref, vw_ref, gate_ref, aggt_ref,
            placet_ref, o_ref, tk=tk, n_sel=n_sel, n_main=j * tk))


def _nsa_attn_body(q_ref, kc_ref, ks_ref, vs_ref, kw_ref, vw_ref, gate_ref, aggt_ref, placet_ref,
                   o_ref, *, tk, n_sel, n_main):
    tq = q_ref.shape[0]
    seq = ks_ref.shape[0]
    ncp = kc_ref.shape[0]
    ns = aggt_ref.shape[0]
    hp, dk, rep = HEAD_PAD, NSA_DK, NSA_GROUP
    q0 = pl.program_id(1) * tq
    t = q0 + lax.broadcasted_iota(jnp.int32, (tq, 1), 0)
    t_rows = jnp.concatenate([t] * rep, axis=0)
    t_lane = q0 + lax.broadcasted_iota(jnp.int32, (1, tq), 1)

    blk = lax.broadcasted_iota(jnp.int32, (ns, 1), 0)
    cur = lax.shift_right_logical(t_lane, int(np.log2(SLC_BLOCK)))
    forced = (blk == 0) | (blk == cur) | (blk == cur - 1)
    causal_blk = blk * SLC_BLOCK <= t_lane
    cmp_end = lax.broadcasted_iota(jnp.int32, (1, ncp), 1) * CMP_STRIDE + (CMP_BLOCK - 1)
    cmp_mask = (cmp_end <= t_rows) & (cmp_end < seq)
    win_len = WINDOW + tq
    w_start = pl.multiple_of(jnp.maximum(q0 - WINDOW, 0), tq)
    wpos = w_start + lax.broadcasted_iota(jnp.int32, (1, win_len), 1)
    win_bias = jnp.where((wpos <= t) & (wpos > t - WINDOW), 0.0, -MASK_BIG)
    last_pos = n_main + lax.broadcasted_iota(jnp.int32, (1, tk), 1)
    last_bias = jnp.where(last_pos <= t, 0.0, -MASK_BIG)
    pad_lane = lax.broadcasted_iota(jnp.int32, (1, hp), 1)
    sel_offset = jnp.where((pad_lane >= dk) & (pad_lane < dk + ns), MASK_BIG, 0.0)
    gates = gate_ref[...]

    kv_cols = [slice(g * hp, (g + 1) * hp) for g in range(NSA_KV_HEADS)]
    qs_g, q_sel_g, o_c_g = [], [], []
    for g in range(NSA_KV_HEADS):
        q_heads = [q_ref[:, (g * rep + r) * hp:(g * rep + r + 1) * hp] for r in range(rep)]
        qs = jnp.concatenate(q_heads, axis=0)

        s = jnp.where(cmp_mask, _dot_nt(qs, kc_ref[:, kv_cols[g]]), NEG_INF)
        p = jnp.where(cmp_mask, jnp.exp2(s - jnp.max(s, axis=-1, keepdims=True)), 0.0)
        p_c = p * _row_recip(jnp.sum(p, axis=-1, keepdims=True))
        cv_cols = slice((NSA_KV_HEADS + g) * hp, (NSA_KV_HEADS + g + 1) * hp)
        o_c = _dot(p_c.astype(BF16), kc_ref[:, cv_cols])

        p_sum = p_c[0:tq]
        for r in range(1, rep):
            p_sum = p_sum + p_c[r * tq:(r + 1) * tq]
        agg_t = aggt_ref[...]
        hi, mid, lo = _split3(p_sum)
        imp = _dot_nt(agg_t, hi) + _dot_nt(agg_t, mid) + _dot_nt(agg_t, lo)
        val = jnp.where(forced, FORCE_SCORE, jnp.where(causal_blk, imp, NEG_INF))
        rank = jnp.zeros((ns, tq), F32)
        for i in range(ns):
            other = val[i:i + 1, :]
            beats = (other > val) | ((other == val) & (blk > i))
            rank = rank + jnp.where(beats, 1.0, 0.0)
        sel = jnp.where((rank < n_sel) & causal_blk, 1.0, 0.0).astype(BF16)
        sel_pad = _dot(placet_ref[...], sel).T - sel_offset
        q_sel = jnp.concatenate([(qh.astype(F32) + sel_pad).astype(BF16) for qh in q_heads], axis=0)
        qs_g.append(qs)
        q_sel_g.append(q_sel)
        o_c_g.append(o_c)

    for g in range(NSA_KV_HEADS):
        o_c = o_c_g[g]
        acc_s = _prefix_attention(q_sel_g[g], ks_ref, vs_ref, kv_cols[g], n_main, tk,
                                  lambda s: _add_per_head(s, last_bias, rep))
        r_s = _row_recip(acc_s[:, dk:dk + 1])

        s = _add_per_head(_dot_nt(qs_g[g], kw_ref[pl.ds(w_start, win_len), kv_cols[g]]), win_bias, rep)
        p = jnp.exp2(s - jnp.max(s, axis=-1, keepdims=True))
        acc_w = _dot(p.astype(BF16), vw_ref[pl.ds(w_start, win_len), kv_cols[g]])
        r_w = _row_recip(acc_w[:, dk:dk + 1])

        for r in range(rep):
            rows = slice(r * tq, (r + 1) * tq)
            gc = g * 3 * rep + r
            o_h = (gates[:, gc:gc + 1] * o_c[rows]
                   + (gates[:, gc + rep:gc + rep + 1] * r_s[rows]) * acc_s[rows]
                   + (gates[:, gc + 2 * rep:gc + 2 * rep + 1] * r_w[rows]) * acc_w[rows])
            h = g * rep + r
            o_ref[:, h * dk:(h + 1) * dk] = o_h[:, :dk].astype(BF16)


def _nsa_tables(seq):
    nchunk = seq // CMP_STRIDE
    ns = seq // SLC_BLOCK
    cs = np.arange(nchunk)[:, None] * CMP_STRIDE
    ss = np.arange(ns)[None, :] * SLC_BLOCK
    overlap = np.clip(np.minimum(cs + CMP_BLOCK, ss + SLC_BLOCK) - np.maximum(cs, ss), 0, None)
    agg_t = (overlap.astype(np.float32) / CMP_BLOCK).T
    place_t = np.zeros((HEAD_PAD, ns), np.float32)
    place_t[NSA_DK + np.arange(ns), np.arange(ns)] = MASK_BIG
    return jnp.asarray(agg_t, BF16), jnp.asarray(place_t, BF16)


def _nsa_attn(q, kc, ks, vs, kw, vw, gates):
    b, s, qw = q.shape
    tq = min(NSA_TQ, s)
    tk = min(NSA_TK, s)
    assert s >= WINDOW + tq and s % tk == 0 and WINDOW % tq == 0
    agg_t, place_t = _nsa_tables(s)
    whole = lambda a: pl.BlockSpec((None,) + a.shape[1:], lambda i, j: (i, 0, 0))
    ow = NSA_HEADS * NSA_DK
    return pl.pallas_call(
        functools.partial(_nsa_attn_kernel, tk=tk, n_sel=min(N_SEL, s // SLC_BLOCK)),
        out_shape=jax.ShapeDtypeStruct((b, s, ow), BF16),
        grid=(b, s // tq),
        in_specs=[
            pl.BlockSpec((None, tq, qw), lambda i, j: (i, j, 0)),
            whole(kc), whole(ks), whole(vs), whole(kw), whole(vw),
            pl.BlockSpec((None, tq, gates.shape[-1]), lambda i, j: (i, j, 0)),
            _const_spec(agg_t.shape), _const_spec(place_t.shape),
        ],
        out_specs=pl.BlockSpec((None, tq, ow), lambda i, j: (i, j, 0)),
        compiler_params=_cparams(("arbitrary", "arbitrary")),
        name="nsa_attn",
    )(q, kc, ks, vs, kw, vw, gates, agg_t, place_t)


def _nsa_gate_layout(w_in, gate_b):
    qd = NSA_HEADS * NSA_DK
    o = qd + 3 * NSA_KV_W
    perm = np.arange(3 * NSA_HEADS).reshape(NSA_KV_HEADS, NSA_GROUP, 3).transpose(0, 2, 1).reshape(-1)
    w_in = jnp.concatenate([w_in[:, :o], w_in[:, o:][:, perm]], axis=1)
    return w_in, gate_b[perm]


def kernel(x, c, positions, ada_w, ada_b, norm_g, final_g, ff_w13, ff_w2, hy_w_in, hy_conv_w,
           hy_q_norm, hy_kv_norm, hy_w_uq, hy_w_ukv, hy_w_out, nsa_w_in, nsa_cmp_pe, nsa_cmp_w1,
           nsa_cmp_w2, nsa_gate_b, nsa_w_out):
    depth = ada_w.shape[0]
    mod = _ada_mod(c, ada_w, ada_b)
    ropes = _rope_tables(positions)
    w13 = ff_w13.astype(BF16)
    w2 = ff_w2.astype(BF16)
    for l in range(depth):
        m = l // 2
        x = _ffn(x, mod[l], norm_g[l, 0], w13, w2, l, 0, final_g, k0=0, final=False)
        if l % 2 == 0:
            w_in = hy_w_in[m].astype(BF16)
            w_in = jnp.pad(w_in, ((0, 0), (0, -w_in.shape[1] % HEAD_PAD)))
            y_conv, q, k, v = _hy_proj(x, mod[l], norm_g[l, 1], w_in,
                                       hy_conv_w[m], hy_q_norm[m], hy_kv_norm[m],
                                       _hy_weights(hy_w_uq[m], hy_w_ukv[m]), ropes)
            y_att = _mla_attn(q, k, v)
            w_out = hy_w_out[m].astype(BF16)
            parts, w_parts = [y_conv, y_att], [w_out[:CONV_WIDTH], w_out[CONV_WIDTH:]]
        else:
            w_in, gate_b = _nsa_gate_layout(nsa_w_in[m], nsa_gate_b[m])
            q, kvc, ks, vs, kw, vw, gates = _nsa_proj(x, mod[l], norm_g[l, 1], w_in.astype(BF16), gate_b)
            kc = _nsa_cmp(kvc, *_nsa_cmp_weights(nsa_cmp_pe[m], nsa_cmp_w1[m], nsa_cmp_w2[m]))
            o = _nsa_attn(q, kc, ks, vs, kw, vw, gates)
            parts, w_parts = [o], [nsa_w_out[m].astype(BF16)]
        x = _ffn(x, mod[l], norm_g[l, 2], w13, w2, l, 1, final_g, k0=6, final=(l == depth - 1),
                 parts=parts, w_parts=w_parts)
    return x
```

```python
import functools

import jax
import jax.numpy as jnp
import numpy as np
from jax import lax
from jax.experimental import pallas as pl
from jax.experimental.pallas import tpu as pltpu

F32 = jnp.float32
BF16 = jnp.bfloat16

N_ADA = 9
EPS = 1e-6
NEG_INF = -1e30
CONV_WIDTH = 512
CONV_TAPS = 3
MLA_HEADS = 8
MLA_NOPE = 64
MLA_ROPE = 32
MLA_V = 64
Q_LORA = 256
KV_LORA = 128
ROPE_THETA = 10000.0
NSA_HEADS = 16
NSA_KV_HEADS = 2
NSA_GROUP = NSA_HEADS // NSA_KV_HEADS
NSA_DK = 64
CMP_BLOCK = 32
CMP_STRIDE = 16
CMP_HID = 128
SLC_BLOCK = 64
N_SEL = 8
WINDOW = 512
FORCE_SCORE = 1e4
NSA_KV_W = 2 * NSA_KV_HEADS * NSA_DK

HEAD_PAD = 128
VMEM_LIMIT = 56 * 1024 * 1024
ROW_TILE = 512
PROJ_SUBTILES = 2
MLA_TQ = 512
MLA_HEADS_PER_STEP = 4
NSA_TQ = 128
NSA_TK = 256
NSA_HEADS_PER_CHAIN = NSA_GROUP

LOG2E = float(np.log2(np.e))
MASK_BIG = float(2.0 ** 100)


def _cparams(sem):
    return pltpu.CompilerParams(dimension_semantics=sem, vmem_limit_bytes=VMEM_LIMIT)


def _const_spec(shape):
    nd = len(shape)
    return pl.BlockSpec(shape, lambda *_: (0,) * nd, pipeline_mode=pl.Buffered(1))


def _sigmoid(v):
    return 1.0 / (1.0 + jnp.exp(-v))


def _rms(v, g):
    return v * lax.rsqrt(jnp.mean(v * v, axis=-1, keepdims=True) + EPS) * g


def _modulate(x, g, mod_ref, k0):
    shift = mod_ref[k0:k0 + 1, :]
    scale = mod_ref[k0 + 1:k0 + 2, :]
    return _rms(x, g) * (1.0 + scale) + shift


def _dot(a, b):
    return jnp.dot(a, b, preferred_element_type=F32)


def _dot_nt(a, b):
    return lax.dot_general(a, b, (((1,), (1,)), ((), ())), preferred_element_type=F32)


def _split3(a):
    hi = a.astype(BF16)
    r1 = a - hi.astype(F32)
    mid = r1.astype(BF16)
    lo = (r1 - mid.astype(F32)).astype(BF16)
    return hi, mid, lo


def _row_recip(v):
    return 1.0 / jnp.maximum(v, 1e-20)


def _ada_kernel(c_ref, w_ref, b_ref, o_ref):
    c = c_ref[...]
    ca = (c * _sigmoid(c)).astype(BF16)
    o_ref[...] = _dot(ca, w_ref[...].astype(BF16)) + b_ref[...]


def _ada_mod(c, ada_w, ada_b):
    depth, d, n = ada_w.shape
    b = c.shape[0]
    tn = n // 8
    out = pl.pallas_call(
        _ada_kernel,
        out_shape=jax.ShapeDtypeStruct((depth, b, n), F32),
        grid=(depth, n // tn),
        in_specs=[
            pl.BlockSpec((b, d), lambda l, j: (0, 0)),
            pl.BlockSpec((None, d, tn), lambda l, j: (l, 0, j)),
            pl.BlockSpec((None, 1, tn), lambda l, j: (l, 0, j)),
        ],
        out_specs=pl.BlockSpec((None, b, tn), lambda l, j: (l, 0, j)),
        compiler_params=_cparams(("arbitrary", "arbitrary")),
        name="ada_mod",
    )(c, ada_w, ada_b.reshape(depth, 1, n))
    return out.reshape(depth, b, N_ADA, d)


def _ffn_kernel(*refs, k0, d_ff, final, n_parts):
    x_ref, mod_ref, g_ref = refs[:3]
    part_refs = refs[3:3 + n_parts]
    wout_refs = refs[3 + n_parts:3 + 2 * n_parts]
    w13_ref, w2_ref, fg_ref, o_ref = refs[3 + 2 * n_parts:]
    x = x_ref[...]
    if n_parts:
        y = _dot(part_refs[0][...], wout_refs[0][...])
        for p_ref, w_ref in zip(part_refs[1:], wout_refs[1:]):
            y = y + _dot(p_ref[...], w_ref[...])
        x = x + mod_ref[5:6, :] * y
    h = _modulate(x, g_ref[...], mod_ref, k0).astype(BF16)
    ab = _dot(h, w13_ref[...])
    a = ab[:, :d_ff]
    b = ab[:, d_ff:]
    u = (a * _sigmoid(a) * b).astype(BF16)
    y = _dot(u, w2_ref[...])
    out = x + (0.5 * mod_ref[k0 + 2:k0 + 3, :]) * y
    if final:
        out = _rms(out, fg_ref[...])
    o_ref[...] = out


def _ffn(x, mod_l, g, w13, w2, layer, which, final_g, *, k0, final, parts=(), w_parts=()):
    b, s, d = x.shape
    d_ff = w2.shape[2]
    tm = min(ROW_TILE, s)
    kern = functools.partial(_ffn_kernel, k0=k0, d_ff=d_ff, final=final, n_parts=len(parts))
    pick = lambda *_: (layer, which, 0, 0)
    row = lambda width: pl.BlockSpec((None, tm, width), lambda i, j: (i, j, 0))
    return pl.pallas_call(
        kern,
        out_shape=jax.ShapeDtypeStruct(x.shape, F32),
        grid=(b, s // tm),
        in_specs=[
            row(d),
            pl.BlockSpec((None, N_ADA, d), lambda i, j: (i, 0, 0)),
            _const_spec((1, d)),
        ] + [row(p.shape[-1]) for p in parts] + [_const_spec(w.shape) for w in w_parts] + [
            pl.BlockSpec((None, None, d, 2 * d_ff), pick, pipeline_mode=pl.Buffered(1)),
            pl.BlockSpec((None, None, d_ff, d), pick, pipeline_mode=pl.Buffered(1)),
            _const_spec((1, d)),
        ],
        out_specs=row(d),
        compiler_params=_cparams(("arbitrary", "arbitrary")),
        name="ffn",
    )(x, mod_l, g.reshape(1, d), *parts, *w_parts, w13, w2, final_g.reshape(1, d))


def _rope_swap(t):
    width = t.shape[-1]
    half = MLA_ROPE // 2
    lane = lax.broadcasted_iota(jnp.int32, (1, width), 1) & (HEAD_PAD - 1)
    return jnp.where(lane < MLA_NOPE + half, pltpu.roll(t, width - half, 1), pltpu.roll(t, half, 1))


def _hy_proj_kernel(x_ref, mod_ref, g_ref, win_ref, convw_ref, qn_ref, kvn_ref, wq_ref,
                    wk_ref, wv_ref, rc_ref, rs_ref,
                    yconv_ref, q_ref, k_ref, v_ref, carry_ref, *, q_scale):
    tm = x_ref.shape[0]
    cw = CONV_WIDTH

    @pl.when(pl.program_id(1) == 0)
    def _():
        carry_ref[...] = jnp.zeros_like(carry_ref)

    sub = tm // PROJ_SUBTILES
    prev = carry_ref[...]
    for part in range(PROJ_SUBTILES):
        rows = slice(part * sub, (part + 1) * sub)
        h = _modulate(x_ref[rows, :], g_ref[...], mod_ref, 3).astype(BF16)
        z = _dot(h, win_ref[...])
        u, gate_c, gate_b = z[:, :cw], z[:, cw:2 * cw], z[:, 2 * cw:3 * cw]
        o = 3 * cw
        cq = z[:, o:o + Q_LORA]
        ckv = z[:, o + Q_LORA:o + Q_LORA + KV_LORA]
        kr_tile = z[:, o + Q_LORA + KV_LORA:]

        v = gate_c * u
        row = lax.broadcasted_iota(jnp.int32, (sub, 1), 0)
        v1 = jnp.where(row == 0, prev[7:8, :], pltpu.roll(v, 1, 0))
        v2 = jnp.where(row == 0, prev[6:7, :], jnp.where(row == 1, prev[7:8, :], pltpu.roll(v, 2, 0)))
        w = convw_ref[...]
        yconv_ref[rows, :] = (gate_b * (w[0:1, :] * v2 + w[1:2, :] * v1 + w[2:3, :] * v)).astype(BF16)
        prev = v[sub - 8:, :]

        qn = _rms(cq, qn_ref[...]).astype(BF16)
        kvn = _rms(ckv, kvn_ref[...]).astype(BF16)
        n_rep = q_ref.shape[-1] // HEAD_PAD
        rc = rc_ref[rows, :]
        rs = rs_ref[rows, :]
        q = _dot(qn, wq_ref[...])
        q = q * jnp.concatenate([rc] * n_rep, axis=1) + _rope_swap(q) * jnp.concatenate([rs] * n_rep, axis=1)
        kr = pltpu.roll(kr_tile, MLA_NOPE, 1)
        kr = kr * rc + _rope_swap(kr) * rs
        k = _dot(kvn, wk_ref[...]) + jnp.concatenate([kr] * n_rep, axis=1)
        q_ref[rows, :] = (q * q_scale).astype(BF16)
        k_ref[rows, :] = k.astype(BF16)
        lane = lax.broadcasted_iota(jnp.int32, (1, v_ref.shape[-1]), 1)
        ones_col = jnp.where((lane & (HEAD_PAD - 1)) == MLA_V, 1.0, 0.0)
        v_ref[rows, :] = (_dot(kvn, wv_ref[...]) + ones_col).astype(BF16)
    carry_ref[...] = prev


def _hy_weights(w_uq, w_ukv):
    hp = HEAD_PAD
    pad_heads = lambda w: jnp.pad(w, ((0, 0), (0, 0), (0, hp - w.shape[-1]))).reshape(w.shape[0], -1)
    wq = pad_heads(w_uq.reshape(Q_LORA, MLA_HEADS, MLA_NOPE + MLA_ROPE))
    wkv = w_ukv.reshape(KV_LORA, MLA_HEADS, MLA_NOPE + MLA_V)
    wk = pad_heads(wkv[..., :MLA_NOPE])
    wv = pad_heads(wkv[..., MLA_NOPE:])
    return wq.astype(BF16), wk.astype(BF16), wv.astype(BF16)


def _rope_tables(positions):
    half = MLA_ROPE // 2
    inv = ROPE_THETA ** (-jnp.arange(half, dtype=F32) / half)
    ang = positions.astype(F32)[..., None] * inv
    cos, sin = jnp.cos(ang), jnp.sin(ang)
    lead = positions.shape
    ones = jnp.ones(lead + (MLA_NOPE,), F32)
    ztail = jnp.zeros(lead + (HEAD_PAD - MLA_NOPE - MLA_ROPE,), F32)
    zhead = jnp.zeros(lead + (MLA_NOPE,), F32)
    rc = jnp.concatenate([ones, cos, cos, ztail], axis=-1)
    rs = jnp.concatenate([zhead, -sin, sin, ztail], axis=-1)
    return rc, rs


def _hy_proj(x, mod_l, g, w_in, conv_w, q_norm, kv_norm, hy_w, ropes):
    b, s, d = x.shape
    tm = min(ROW_TILE, s)
    hp = HEAD_PAD
    qw = MLA_HEADS * hp
    row = lambda width: pl.BlockSpec((None, tm, width), lambda i, j: (i, j, 0))
    q_scale = (MLA_NOPE + MLA_ROPE) ** -0.5 * LOG2E
    return pl.pallas_call(
        functools.partial(_hy_proj_kernel, q_scale=q_scale),
        out_shape=(
            jax.ShapeDtypeStruct((b, s, CONV_WIDTH), BF16),
            jax.ShapeDtypeStruct((b, s, qw), BF16),
            jax.ShapeDtypeStruct((b, s, qw), BF16),
            jax.ShapeDtypeStruct((b, s, qw), BF16),
        ),
        grid=(b, s // tm),
        in_specs=[
            row(d),
            pl.BlockSpec((None, N_ADA, d), lambda i, j: (i, 0, 0)),
            _const_spec((1, d)),
            _const_spec(w_in.shape),
            _const_spec(conv_w.shape),
            _const_spec((1, Q_LORA)),
            _const_spec((1, KV_LORA)),
        ] + [_const_spec(w.shape) for w in hy_w] + [row(hp), row(hp)],
        out_specs=(row(CONV_WIDTH), row(qw), row(qw), row(qw)),
        scratch_shapes=[pltpu.VMEM((8, CONV_WIDTH), F32)],
        compiler_params=_cparams(("arbitrary", "arbitrary")),
        name="hy_proj",
    )(x, mod_l, g.reshape(1, d), w_in, conv_w, q_norm.reshape(1, -1), kv_norm.reshape(1, -1),
      *hy_w, *ropes)


def _prefix_attention(q, k_ref, v_ref, cols, n_main, n_edge, edge_bias):
    edge = slice(n_main, n_main + n_edge)
    s_edge = edge_bias(_dot_nt(q, k_ref[edge, cols]))
    m = jnp.max(s_edge, axis=-1, keepdims=True)
    if n_main == 0:
        return _dot(jnp.exp2(s_edge - m).astype(BF16), v_ref[edge, cols])
    s_main = _dot_nt(q, k_ref[0:n_main, cols])
    m = jnp.maximum(m, jnp.max(s_main, axis=-1, keepdims=True))
    return (_dot(jnp.exp2(s_main - m).astype(BF16), v_ref[0:n_main, cols])
            + _dot(jnp.exp2(s_edge - m).astype(BF16), v_ref[edge, cols]))


def _mla_attn_body(q_ref, k_ref, v_ref, o_ref, n_main):
    tq = q_ref.shape[0]
    hp = HEAD_PAD
    n_heads = q_ref.shape[1] // hp
    rel = (lax.broadcasted_iota(jnp.int32, (1, tq), 1)
           <= lax.broadcasted_iota(jnp.int32, (tq, 1), 0))
    diag_bias = jnp.where(rel, 0.0, -MASK_BIG)
    outs = []
    for hh in range(n_heads):
        cols = slice(hh * hp, (hh + 1) * hp)
        acc = _prefix_attention(q_ref[:, cols], k_ref, v_ref, cols, n_main, tq, lambda s: s + diag_bias)
        outs.append(acc * _row_recip(acc[:, MLA_V:MLA_V + 1]))
    lane = lax.broadcasted_iota(jnp.int32, (1, hp), 1)
    for pair in range(n_heads // 2):
        both = jnp.where(lane < MLA_V, outs[2 * pair], pltpu.roll(outs[2 * pair + 1], MLA_V, 1))
        o_ref[:, pair * hp:(pair + 1) * hp] = both.astype(BF16)


def _mla_attn_kernel(q_ref, k_ref, v_ref, o_ref):
    tq = q_ref.shape[0]
    qi = pl.program_id(2)
    for j in range(k_ref.shape[0] // tq):
        pl.when(qi == j)(functools.partial(_mla_attn_body, q_ref, k_ref, v_ref, o_ref, j * tq))


def _mla_attn(q, k, v):
    b, s, _ = q.shape
    tq = min(MLA_TQ, s)
    qw = MLA_HEADS_PER_STEP * HEAD_PAD
    ow = MLA_HEADS_PER_STEP * MLA_V
    return pl.pallas_call(
        _mla_attn_kernel,
        out_shape=jax.ShapeDtypeStruct((b, s, MLA_HEADS * MLA_V), BF16),
        grid=(b, MLA_HEADS // MLA_HEADS_PER_STEP, s // tq),
        in_specs=[
            pl.BlockSpec((None, tq, qw), lambda i, h, j: (i, j, h)),
            pl.BlockSpec((None, s, qw), lambda i, h, j: (i, 0, h)),
            pl.BlockSpec((None, s, qw), lambda i, h, j: (i, 0, h)),
        ],
        out_specs=pl.BlockSpec((None, tq, ow), lambda i, h, j: (i, j, h)),
        compiler_params=_cparams(("arbitrary", "arbitrary", "arbitrary")),
        name="mla_attn",
    )(q, k, v)


def _nsa_proj_kernel(x_ref, mod_ref, g_ref, win_ref, gb_ref, qt_ref, kvc_ref, ks_ref, vst_ref,
                     kw_ref, vw_ref, gt_ref, *, q_scale):
    tm = x_ref.shape[0]
    hp, dk = HEAD_PAD, NSA_DK
    pad = hp - dk
    h = _modulate(x_ref[...], g_ref[...], mod_ref, 3).astype(BF16)
    z = _dot(h, win_ref[...])
    qd = NSA_HEADS * dk
    kw = NSA_KV_W
    zeros = jnp.zeros((tm, pad), BF16)
    zero_rows = jnp.zeros((pad, tm), BF16)
    for pair in range(NSA_HEADS // 2):
        qt = (z[:, pair * hp:(pair + 1) * hp] * q_scale).T.astype(BF16)
        for half in range(2):
            hd = 2 * pair + half
            qt_ref[hd * hp:hd * hp + dk, :] = qt[half * dk:(half + 1) * dk, :]
            qt_ref[hd * hp + dk:(hd + 1) * hp, :] = zero_rows
    kvc_ref[...] = z[:, qd:qd + kw].astype(BF16)
    pos = pl.program_id(1) * tm + lax.broadcasted_iota(jnp.int32, (tm, 1), 0)
    lane = lax.broadcasted_iota(jnp.int32, (1, pad), 1)
    blk_onehot = jnp.where(lax.shift_right_logical(pos, int(np.log2(SLC_BLOCK))) == lane, 1.0, 0.0)
    blk_onehot = blk_onehot.astype(BF16)
    ones_col = jnp.broadcast_to(jnp.where(lane == 0, 1.0, 0.0), (tm, pad)).astype(BF16)
    kvs = z[:, qd + kw:qd + 2 * kw].astype(BF16)
    kvw = z[:, qd + 2 * kw:qd + 3 * kw].astype(BF16)
    for g in range(NSA_KV_HEADS):
        lo, mid, hi = g * hp, g * hp + dk, (g + 1) * hp
        kc, vc = slice(g * dk, (g + 1) * dk), slice((NSA_KV_HEADS + g) * dk, (NSA_KV_HEADS + g + 1) * dk)
        ks_ref[:, lo:mid] = kvs[:, kc]
        ks_ref[:, mid:hi] = blk_onehot
        kw_ref[:, lo:mid] = kvw[:, kc]
        kw_ref[:, mid:hi] = zeros
        vw_ref[:, lo:mid] = kvw[:, vc]
        vw_ref[:, mid:hi] = ones_col
    v_off = qd + kw + NSA_KV_HEADS * dk
    vt = z[:, v_off:v_off + NSA_KV_HEADS * dk].T.astype(BF16)
    ones_row = jnp.where(lax.broadcasted_iota(jnp.int32, (pad, 1), 0) == 0, 1.0, 0.0)
    ones_row = jnp.broadcast_to(ones_row, (pad, tm)).astype(BF16)
    for g in range(NSA_KV_HEADS):
        vst_ref[g * hp:g * hp + dk, :] = vt[g * dk:(g + 1) * dk, :]
        vst_ref[g * hp + dk:(g + 1) * hp, :] = ones_row
    gt_ref[...] = _sigmoid(z[:, qd + 3 * kw:] + gb_ref[...]).T


def _nsa_proj(x, mod_l, g, w_in, gate_b):
    b, s, d = x.shape
    tm = min(ROW_TILE, s)
    assert s // SLC_BLOCK <= HEAD_PAD - NSA_DK
    qw = NSA_HEADS * HEAD_PAD
    kvw = NSA_KV_HEADS * HEAD_PAD
    ng = gate_b.shape[-1]
    row = lambda width: pl.BlockSpec((None, tm, width), lambda i, j: (i, j, 0))
    col = lambda height: pl.BlockSpec((None, height, tm), lambda i, j: (i, 0, j))
    sds = lambda width, dt=BF16: jax.ShapeDtypeStruct((b, s, width), dt)
    sds_t = lambda height, dt=BF16: jax.ShapeDtypeStruct((b, height, s), dt)
    return pl.pallas_call(
        functools.partial(_nsa_proj_kernel, q_scale=NSA_DK ** -0.5 * LOG2E),
        out_shape=(sds_t(qw), sds(NSA_KV_W), sds(kvw), sds_t(kvw), sds(kvw), sds(kvw), sds_t(ng, F32)),
        grid=(b, s // tm),
        in_specs=[
            row(d),
            pl.BlockSpec((None, N_ADA, d), lambda i, j: (i, 0, 0)),
            _const_spec((1, d)),
            _const_spec(w_in.shape),
            _const_spec((1, ng)),
        ],
        out_specs=(col(qw), row(NSA_KV_W), row(kvw), col(kvw), row(kvw), row(kvw), col(ng)),
        compiler_params=_cparams(("arbitrary", "arbitrary")),
        name="nsa_proj",
    )(x, mod_l, g.reshape(1, d), w_in, gate_b.reshape(1, ng))


def _nsa_cmp_kernel(ch_ref, w1a_ref, w1b_ref, pea_ref, peb_ref, w2_ref, k_ref, vt_ref):
    ch = ch_ref[...]
    n = ch.shape[0]
    first = _dot(ch, w1a_ref[...])
    second = _dot(ch, w1b_ref[...])
    bias = _dot(pea_ref[...], w1a_ref[...]) + _dot(peb_ref[...], w1b_ref[...])
    hid = first + pltpu.roll(second, n - 1, 0) + bias[0:1, :]
    act = (hid * _sigmoid(hid)).astype(BF16)
    kv = _dot(act, w2_ref[...])
    half = kv.shape[1] // 2
    k_ref[...] = kv[:, :half].astype(BF16)
    vt_ref[...] = kv[:, half:].T.astype(BF16)


def _nsa_cmp_weights(cmp_pe, cmp_w1, cmp_w2):
    ncomp = 2 * NSA_KV_HEADS
    half = CMP_BLOCK // 2
    kv_of = np.arange(ncomp) // NSA_KV_HEADS
    eye = jnp.eye(ncomp, dtype=F32)
    w1 = cmp_w1.reshape(2, 2, half, NSA_DK, CMP_HID)[kv_of]
    big1 = jnp.einsum("chldj,ce->hlcdej", w1, eye).reshape(2, half * ncomp * NSA_DK, ncomp * CMP_HID)
    w2 = jnp.pad(cmp_w2[kv_of], ((0, 0), (0, 0), (0, HEAD_PAD - NSA_DK)))
    big2 = jnp.einsum("cjd,ce->cjed", w2, eye).reshape(ncomp * CMP_HID, ncomp * HEAD_PAD)
    pe = cmp_pe.reshape(2, 2, half, NSA_DK)[kv_of]
    pe = jnp.transpose(pe, (1, 2, 0, 3)).reshape(2, 1, half * ncomp * NSA_DK)
    pe = jnp.broadcast_to(pe, (2, 8, half * ncomp * NSA_DK))
    return (big1[0].astype(BF16), big1[1].astype(BF16), pe[0].astype(BF16), pe[1].astype(BF16),
            big2.astype(BF16))


def _nsa_cmp(kvc, w1a, w1b, pea, peb, w2):
    b, s, kw = kvc.shape
    nchunk = s // CMP_STRIDE
    ow = w2.shape[1]
    chunks = kvc.reshape(b, nchunk, CMP_STRIDE * kw)
    half = ow // 2
    return pl.pallas_call(
        _nsa_cmp_kernel,
        out_shape=(jax.ShapeDtypeStruct((b, nchunk, half), BF16),
                   jax.ShapeDtypeStruct((b, half, nchunk), BF16)),
        grid=(b,),
        in_specs=[
            pl.BlockSpec((None, nchunk, CMP_STRIDE * kw), lambda i: (i, 0, 0)),
            _const_spec(w1a.shape), _const_spec(w1b.shape),
            _const_spec(pea.shape), _const_spec(peb.shape), _const_spec(w2.shape),
        ],
        out_specs=(pl.BlockSpec((None, nchunk, half), lambda i: (i, 0, 0)),
                   pl.BlockSpec((None, half, nchunk), lambda i: (i, 0, 0))),
        compiler_params=_cparams(("arbitrary",)),
        name="nsa_cmp",
    )(chunks, w1a, w1b, pea, peb, w2)


def _add_per_head(s, bias, rep):
    tq = bias.shape[1]
    return jnp.concatenate([s[:, r * tq:(r + 1) * tq] + bias for r in range(rep)], axis=1)


def _prefix_attention_t(q_t, k_ref, vt_ref, cols, n_main, n_edge, edge_bias):
    edge = slice(n_main, n_main + n_edge)
    s_edge = edge_bias(_dot(k_ref[edge, cols], q_t))
    m = jnp.max(s_edge, axis=0, keepdims=True)
    if n_main == 0:
        return _dot(vt_ref[cols, edge], jnp.exp2(s_edge - m).astype(BF16))
    s_main = _dot(k_ref[0:n_main, cols], q_t)
    m = jnp.maximum(m, jnp.max(s_main, axis=0, keepdims=True))
    return (_dot(vt_ref[cols, 0:n_main], jnp.exp2(s_main - m).astype(BF16))
            + _dot(vt_ref[cols, edge], jnp.exp2(s_edge - m).astype(BF16)))


def _nsa_attn_kernel(qt_ref, kc_ref, vct_ref, ks_ref, vst_ref, kw_ref, vw_ref, gt_ref, aggt_ref,
                     placet_ref, o_ref, *, tk, n_sel):
    tq = qt_ref.shape[1]
    per = tk // tq
    qi = pl.program_id(1)
    for j in range(ks_ref.shape[0] // tk):
        in_class = (qi >= j * per) & (qi < (j + 1) * per)
        pl.when(in_class)(functools.partial(
            _nsa_attn_body, qt_ref, kc_ref, vct_ref, ks_ref, vst_ref, kw_ref, vw_ref, gt_ref,
            aggt_ref, placet_ref, o_ref, tk=tk, n_sel=n_sel, n_main=j * tk))


def _nsa_attn_body(qt_ref, kc_ref, vct_ref, ks_ref, vst_ref, kw_ref, vw_ref, gt_ref, aggt_ref,
                   placet_ref, o_ref, *, tk, n_sel, n_main):
    tq = qt_ref.shape[1]
    seq = ks_ref.shape[0]
    ncp = kc_ref.shape[0]
    ns = aggt_ref.shape[0]
    hp, dk, rep = HEAD_PAD, NSA_DK, NSA_GROUP
    q0 = pl.program_id(1) * tq
    t = q0 + lax.broadcasted_iota(jnp.int32, (1, tq), 1)
    hpc = NSA_HEADS_PER_CHAIN
    n_chains = rep // hpc
    t_heads = jnp.concatenate([t] * hpc, axis=1)

    blk = lax.broadcasted_iota(jnp.int32, (ns, 1), 0)
    cur = lax.shift_right_logical(t, int(np.log2(SLC_BLOCK)))
    forced = (blk == 0) | (blk == cur) | (blk == cur - 1)
    causal_blk = blk * SLC_BLOCK <= t
    cmp_end = lax.broadcasted_iota(jnp.int32, (ncp, 1), 0) * CMP_STRIDE + (CMP_BLOCK - 1)
    cmp_mask = (cmp_end <= t_heads) & (cmp_end < seq)
    win_len = WINDOW + tq
    w_start = pl.multiple_of(jnp.maximum(q0 - WINDOW, 0), tq)
    wpos = w_start + lax.broadcasted_iota(jnp.int32, (win_len, 1), 0)
    win_bias = jnp.where((wpos <= t) & (wpos > t - WINDOW), 0.0, -MASK_BIG)
    last_pos = n_main + lax.broadcasted_iota(jnp.int32, (tk, 1), 0)
    last_bias = jnp.where(last_pos <= t, 0.0, -MASK_BIG)
    pad_row = lax.broadcasted_iota(jnp.int32, (hp, 1), 0)
    sel_offset = jnp.where((pad_row >= dk) & (pad_row < dk + ns), MASK_BIG, 0.0)
    gates = gt_ref[...]

    head_out = []
    for g in range(NSA_KV_HEADS):
        kv_cols = slice(g * hp, (g + 1) * hp)
        q_heads = [qt_ref[(g * rep + r) * hp:(g * rep + r + 1) * hp, :] for r in range(rep)]
        q_chains = [jnp.concatenate(q_heads[c * hpc:(c + 1) * hpc], axis=1) for c in range(n_chains)]

        o_c, p_sum = [], None
        for qc in q_chains:
            s = jnp.where(cmp_mask, _dot(kc_ref[:, kv_cols], qc), NEG_INF)
            p = jnp.where(cmp_mask, jnp.exp2(s - jnp.max(s, axis=0, keepdims=True)), 0.0)
            p_c = p * _row_recip(jnp.sum(p, axis=0, keepdims=True))
            o_c.append(_dot(vct_ref[kv_cols, :], p_c.astype(BF16)))
            for r in range(hpc):
                part = p_c[:, r * tq:(r + 1) * tq]
                p_sum = part if p_sum is None else p_sum + part

        vw_t = vw_ref[pl.ds(w_start, win_len), kv_cols].astype(F32).T.astype(BF16)
        acc_w = []
        for qc in q_chains:
            s = _add_per_head(_dot(kw_ref[pl.ds(w_start, win_len), kv_cols], qc), win_bias, hpc)
            p = jnp.exp2(s - jnp.max(s, axis=0, keepdims=True))
            acc_w.append(_dot(vw_t, p.astype(BF16)))

        agg_t = aggt_ref[...]
        hi, mid, lo = _split3(p_sum)
        imp = _dot(agg_t, hi) + _dot(agg_t, mid) + _dot(agg_t, lo)
        val = jnp.where(forced, FORCE_SCORE, jnp.where(causal_blk, imp, NEG_INF))
        rank = jnp.zeros((ns, tq), F32)
        for i in range(ns):
            other = val[i:i + 1, :]
            beats = (other > val) | ((other == val) & (blk > i))
            rank = rank + jnp.where(beats, 1.0, 0.0)
        sel = jnp.where((rank < n_sel) & causal_blk, 1.0, 0.0).astype(BF16)
        sel_pad = _dot(placet_ref[...], sel) - sel_offset

        for c in range(n_chains):
            q_sel = jnp.concatenate([(qh.astype(F32) + sel_pad).astype(BF16)
                                     for qh in q_heads[c * hpc:(c + 1) * hpc]], axis=1)
            acc_s = _prefix_attention_t(q_sel, ks_ref, vst_ref, kv_cols, n_main, tk,
                                        lambda sc: _add_per_head(sc, last_bias, hpc))
            r_s = _row_recip(acc_s[dk:dk + 1, :])
            r_w = _row_recip(acc_w[c][dk:dk + 1, :])
            for r in range(hpc):
                cols = slice(r * tq, (r + 1) * tq)
                gc = g * 3 * rep + c * hpc + r
                o_h = (gates[gc:gc + 1, :] * o_c[c][:, cols]
                       + (gates[gc + rep:gc + rep + 1, :] * r_s[:, cols]) * acc_s[:, cols]
                       + (gates[gc + 2 * rep:gc + 2 * rep + 1, :] * r_w[:, cols]) * acc_w[c][:, cols])
                head_out.append(o_h[0:dk, :])
    o_ref[...] = jnp.concatenate(head_out, axis=0).T.astype(BF16)


def _nsa_tables(seq):
    nchunk = seq // CMP_STRIDE
    ns = seq // SLC_BLOCK
    cs = np.arange(nchunk)[:, None] * CMP_STRIDE
    ss = np.arange(ns)[None, :] * SLC_BLOCK
    overlap = np.clip(np.minimum(cs + CMP_BLOCK, ss + SLC_BLOCK) - np.maximum(cs, ss), 0, None)
    agg_t = (overlap.astype(np.float32) / CMP_BLOCK).T
    place_t = np.zeros((HEAD_PAD, ns), np.float32)
    place_t[NSA_DK + np.arange(ns), np.arange(ns)] = MASK_BIG
    return jnp.asarray(agg_t, BF16), jnp.asarray(place_t, BF16)


def _nsa_attn(qt, kc, vct, ks, vst, kw, vw, gt):
    b, _, s = qt.shape
    tq = min(NSA_TQ, s)
    tk = min(NSA_TK, s)
    assert s >= WINDOW + tq and s % tk == 0 and WINDOW % tq == 0
    agg_t, place_t = _nsa_tables(s)
    whole = lambda a: pl.BlockSpec((None,) + a.shape[1:], lambda i, j: (i, 0, 0))
    col = lambda a: pl.BlockSpec((None, a.shape[1], tq), lambda i, j: (i, 0, j))
    ow = NSA_HEADS * NSA_DK
    return pl.pallas_call(
        functools.partial(_nsa_attn_kernel, tk=tk, n_sel=min(N_SEL, s // SLC_BLOCK)),
        out_shape=jax.ShapeDtypeStruct((b, s, ow), BF16),
        grid=(b, s // tq),
        in_specs=[
            col(qt), whole(kc), whole(vct), whole(ks), whole(vst), whole(kw), whole(vw), col(gt),
            _const_spec(agg_t.shape), _const_spec(place_t.shape),
        ],
        out_specs=pl.BlockSpec((None, tq, ow), lambda i, j: (i, j, 0)),
        compiler_params=_cparams(("arbitrary", "arbitrary")),
        name="nsa_attn",
    )(qt, kc, vct, ks, vst, kw, vw, gt, agg_t, place_t)


def _nsa_gate_layout(w_in, gate_b):
    qd = NSA_HEADS * NSA_DK
    o = qd + 3 * NSA_KV_W
    perm = np.arange(3 * NSA_HEADS).reshape(NSA_KV_HEADS, NSA_GROUP, 3).transpose(0, 2, 1).reshape(-1)
    extra = HEAD_PAD - perm.size
    w_in = jnp.concatenate([w_in[:, :o], jnp.pad(w_in[:, o:][:, perm], ((0, 0), (0, extra)))], axis=1)
    return w_in, jnp.pad(gate_b[perm], (0, extra))


def kernel(x, c, positions, ada_w, ada_b, norm_g, final_g, ff_w13, ff_w2, hy_w_in, hy_conv_w,
           hy_q_norm, hy_kv_norm, hy_w_uq, hy_w_ukv, hy_w_out, nsa_w_in, nsa_cmp_pe, nsa_cmp_w1,
           nsa_cmp_w2, nsa_gate_b, nsa_w_out):
    depth = ada_w.shape[0]
    mod = _ada_mod(c, ada_w, ada_b)
    ropes = _rope_tables(positions)
    w13 = ff_w13.astype(BF16)
    w2 = ff_w2.astype(BF16)
    for l in range(depth):
        m = l // 2
        x = _ffn(x, mod[l], norm_g[l, 0], w13, w2, l, 0, final_g, k0=0, final=False)
        if l % 2 == 0:
            w_in = hy_w_in[m].astype(BF16)
            w_in = jnp.pad(w_in, ((0, 0), (0, -w_in.shape[1] % HEAD_PAD)))
            y_conv, q, k, v = _hy_proj(x, mod[l], norm_g[l, 1], w_in,
                                       hy_conv_w[m], hy_q_norm[m], hy_kv_norm[m],
                                       _hy_weights(hy_w_uq[m], hy_w_ukv[m]), ropes)
            y_att = _mla_attn(q, k, v)
            w_out = hy_w_out[m].astype(BF16)
            parts, w_parts = [y_conv, y_att], [w_out[:CONV_WIDTH], w_out[CONV_WIDTH:]]
        else:
            w_in, gate_b = _nsa_gate_layout(nsa_w_in[m], nsa_gate_b[m])
            qt, kvc, ks, vst, kw, vw, gt = _nsa_proj(x, mod[l], norm_g[l, 1], w_in.astype(BF16), gate_b)
            kc, vct = _nsa_cmp(kvc, *_nsa_cmp_weights(nsa_cmp_pe[m], nsa_cmp_w1[m], nsa_cmp_w2[m]))
            o = _nsa_attn(qt, kc, vct, ks, vst, kw, vw, gt)
            parts, w_parts = [o], [nsa_w_out[m].astype(BF16)]
        x = _ffn(x, mod[l], norm_g[l, 2], w13, w2, l, 1, final_g, k0=6, final=(l == depth - 1),
                 parts=parts, w_parts=w_parts)
    return x
```

```python
import functools

import jax
import jax.numpy as jnp
import numpy as np
from jax import lax
from jax.experimental import pallas as pl
from jax.experimental.pallas import tpu as pltpu

F32 = jnp.float32
BF16 = jnp.bfloat16

N_ADA = 9
EPS = 1e-6
NEG_INF = -1e30
CONV_WIDTH = 512
CONV_TAPS = 3
MLA_HEADS = 8
MLA_NOPE = 64
MLA_ROPE = 32
MLA_V = 64
Q_LORA = 256
KV_LORA = 128
ROPE_THETA = 10000.0
NSA_HEADS = 16
NSA_KV_HEADS = 2
NSA_GROUP = NSA_HEADS // NSA_KV_HEADS
NSA_DK = 64
CMP_BLOCK = 32
CMP_STRIDE = 16
CMP_HID = 128
SLC_BLOCK = 64
N_SEL = 8
WINDOW = 512
FORCE_SCORE = 1e4
NSA_KV_W = 2 * NSA_KV_HEADS * NSA_DK

HEAD_PAD = 128
VMEM_LIMIT = 56 * 1024 * 1024
ROW_TILE = 512
PROJ_SUBTILES = 2
MLA_TQ = 512
MLA_HEADS_PER_STEP = 4
NSA_TQ = 128
NSA_TK = 512
NSA_HEADS_PER_CHAIN = NSA_GROUP

LOG2E = float(np.log2(np.e))
MASK_BIG = float(2.0 ** 100)


def _cparams(sem):
    return pltpu.CompilerParams(dimension_semantics=sem, vmem_limit_bytes=VMEM_LIMIT)


def _const_spec(shape):
    nd = len(shape)
    return pl.BlockSpec(shape, lambda *_: (0,) * nd, pipeline_mode=pl.Buffered(1))


def _sigmoid(v):
    return 1.0 / (1.0 + jnp.exp(-v))


def _rms(v, g):
    return v * lax.rsqrt(jnp.mean(v * v, axis=-1, keepdims=True) + EPS) * g


def _modulate(x, g, mod_ref, k0):
    shift = mod_ref[k0:k0 + 1, :]
    scale = mod_ref[k0 + 1:k0 + 2, :]
    return _rms(x, g) * (1.0 + scale) + shift


def _dot(a, b):
    return jnp.dot(a, b, preferred_element_type=F32)


def _dot_nt(a, b):
    return lax.dot_general(a, b, (((1,), (1,)), ((), ())), preferred_element_type=F32)


def _split3(a):
    hi = a.astype(BF16)
    r1 = a - hi.astype(F32)
    mid = r1.astype(BF16)
    lo = (r1 - mid.astype(F32)).astype(BF16)
    return hi, mid, lo


def _row_recip(v):
    return 1.0 / jnp.maximum(v, 1e-20)


def _ada_kernel(c_ref, w_ref, b_ref, o_ref):
    c = c_ref[...]
    ca = (c * _sigmoid(c)).astype(BF16)
    o_ref[...] = _dot(ca, w_ref[...].astype(BF16)) + b_ref[...]


def _ada_mod(c, ada_w, ada_b):
    depth, d, n = ada_w.shape
    b = c.shape[0]
    tn = n // 8
    out = pl.pallas_call(
        _ada_kernel,
        out_shape=jax.ShapeDtypeStruct((depth, b, n), F32),
        grid=(depth, n // tn),
        in_specs=[
            pl.BlockSpec((b, d), lambda l, j: (0, 0)),
            pl.BlockSpec((None, d, tn), lambda l, j: (l, 0, j)),
            pl.BlockSpec((None, 1, tn), lambda l, j: (l, 0, j)),
        ],
        out_specs=pl.BlockSpec((None, b, tn), lambda l, j: (l, 0, j)),
        compiler_params=_cparams(("arbitrary", "arbitrary")),
        name="ada_mod",
    )(c, ada_w, ada_b.reshape(depth, 1, n))
    return out.reshape(depth, b, N_ADA, d)


def _ffn_kernel(*refs, k0, d_ff, final, n_parts):
    x_ref, mod_ref, g_ref = refs[:3]
    part_refs = refs[3:3 + n_parts]
    wout_refs = refs[3 + n_parts:3 + 2 * n_parts]
    w13_ref, w2_ref, fg_ref, o_ref = refs[3 + 2 * n_parts:]
    x = x_ref[...]
    if n_parts:
        y = _dot(part_refs[0][...], wout_refs[0][...])
        for p_ref, w_ref in zip(part_refs[1:], wout_refs[1:]):
            y = y + _dot(p_ref[...], w_ref[...])
        x = x + mod_ref[5:6, :] * y
    h = _modulate(x, g_ref[...], mod_ref, k0).astype(BF16)
    ab = _dot(h, w13_ref[...])
    a = ab[:, :d_ff]
    b = ab[:, d_ff:]
    u = (a * _sigmoid(a) * b).astype(BF16)
    y = _dot(u, w2_ref[...])
    out = x + (0.5 * mod_ref[k0 + 2:k0 + 3, :]) * y
    if final:
        out = _rms(out, fg_ref[...])
    o_ref[...] = out


def _ffn(x, mod_l, g, w13, w2, layer, which, final_g, *, k0, final, parts=(), w_parts=()):
    b, s, d = x.shape
    d_ff = w2.shape[2]
    tm = min(ROW_TILE, s)
    kern = functools.partial(_ffn_kernel, k0=k0, d_ff=d_ff, final=final, n_parts=len(parts))
    pick = lambda *_: (layer, which, 0, 0)
    row = lambda width: pl.BlockSpec((None, tm, width), lambda i, j: (i, j, 0))
    return pl.pallas_call(
        kern,
        out_shape=jax.ShapeDtypeStruct(x.shape, F32),
        grid=(b, s // tm),
        in_specs=[
            row(d),
            pl.BlockSpec((None, N_ADA, d), lambda i, j: (i, 0, 0)),
            _const_spec((1, d)),
        ] + [row(p.shape[-1]) for p in parts] + [_const_spec(w.shape) for w in w_parts] + [
            pl.BlockSpec((None, None, d, 2 * d_ff), pick, pipeline_mode=pl.Buffered(1)),
            pl.BlockSpec((None, None, d_ff, d), pick, pipeline_mode=pl.Buffered(1)),
            _const_spec((1, d)),
        ],
        out_specs=row(d),
        compiler_params=_cparams(("arbitrary", "arbitrary")),
        name="ffn",
    )(x, mod_l, g.reshape(1, d), *parts, *w_parts, w13, w2, final_g.reshape(1, d))


def _rope_swap(t):
    width = t.shape[-1]
    half = MLA_ROPE // 2
    lane = lax.broadcasted_iota(jnp.int32, (1, width), 1) & (HEAD_PAD - 1)
    return jnp.where(lane < MLA_NOPE + half, pltpu.roll(t, width - half, 1), pltpu.roll(t, half, 1))


def _hy_proj_kernel(x_ref, mod_ref, g_ref, win_ref, convw_ref, qn_ref, kvn_ref, wq_ref,
                    wk_ref, wv_ref, rc_ref, rs_ref,
                    yconv_ref, q_ref, k_ref, v_ref, carry_ref, *, q_scale):
    tm = x_ref.shape[0]
    cw = CONV_WIDTH

    @pl.when(pl.program_id(1) == 0)
    def _():
        carry_ref[...] = jnp.zeros_like(carry_ref)

    sub = tm // PROJ_SUBTILES
    prev = carry_ref[...]
    for part in range(PROJ_SUBTILES):
        rows = slice(part * sub, (part + 1) * sub)
        h = _modulate(x_ref[rows, :], g_ref[...], mod_ref, 3).astype(BF16)
        z = _dot(h, win_ref[...])
        u, gate_c, gate_b = z[:, :cw], z[:, cw:2 * cw], z[:, 2 * cw:3 * cw]
        o = 3 * cw
        cq = z[:, o:o + Q_LORA]
        ckv = z[:, o + Q_LORA:o + Q_LORA + KV_LORA]
        kr_tile = z[:, o + Q_LORA + KV_LORA:]

        v = gate_c * u
        row = lax.broadcasted_iota(jnp.int32, (sub, 1), 0)
        v1 = jnp.where(row == 0, prev[7:8, :], pltpu.roll(v, 1, 0))
        v2 = jnp.where(row == 0, prev[6:7, :], jnp.where(row == 1, prev[7:8, :], pltpu.roll(v, 2, 0)))
        w = convw_ref[...]
        yconv_ref[rows, :] = (gate_b * (w[0:1, :] * v2 + w[1:2, :] * v1 + w[2:3, :] * v)).astype(BF16)
        prev = v[sub - 8:, :]

        qn = _rms(cq, qn_ref[...]).astype(BF16)
        kvn = _rms(ckv, kvn_ref[...]).astype(BF16)
        n_rep = q_ref.shape[-1] // HEAD_PAD
        rc = rc_ref[rows, :]
        rs = rs_ref[rows, :]
        q = _dot(qn, wq_ref[...])
        q = q * jnp.concatenate([rc] * n_rep, axis=1) + _rope_swap(q) * jnp.concatenate([rs] * n_rep, axis=1)
        kr = pltpu.roll(kr_tile, MLA_NOPE, 1)
        kr = kr * rc + _rope_swap(kr) * rs
        k = _dot(kvn, wk_ref[...]) + jnp.concatenate([kr] * n_rep, axis=1)
        q_ref[rows, :] = (q * q_scale).astype(BF16)
        k_ref[rows, :] = k.astype(BF16)
        lane = lax.broadcasted_iota(jnp.int32, (1, v_ref.shape[-1]), 1)
        ones_col = jnp.where((lane & (HEAD_PAD - 1)) == MLA_V, 1.0, 0.0)
        v_ref[rows, :] = (_dot(kvn, wv_ref[...]) + ones_col).astype(BF16)
    carry_ref[...] = prev


def _hy_weights(w_uq, w_ukv):
    hp = HEAD_PAD
    pad_heads = lambda w: jnp.pad(w, ((0, 0), (0, 0), (0, hp - w.shape[-1]))).reshape(w.shape[0], -1)
    wq = pad_heads(w_uq.reshape(Q_LORA, MLA_HEADS, MLA_NOPE + MLA_ROPE))
    wkv = w_ukv.reshape(KV_LORA, MLA_HEADS, MLA_NOPE + MLA_V)
    wk = pad_heads(wkv[..., :MLA_NOPE])
    wv = pad_heads(wkv[..., MLA_NOPE:])
    return wq.astype(BF16), wk.astype(BF16), wv.astype(BF16)


def _rope_tables(positions):
    half = MLA_ROPE // 2
    inv = ROPE_THETA ** (-jnp.arange(half, dtype=F32) / half)
    ang = positions.astype(F32)[..., None] * inv
    cos, sin = jnp.cos(ang), jnp.sin(ang)
    lead = positions.shape
    ones = jnp.ones(lead + (MLA_NOPE,), F32)
    ztail = jnp.zeros(lead + (HEAD_PAD - MLA_NOPE - MLA_ROPE,), F32)
    zhead = jnp.zeros(lead + (MLA_NOPE,), F32)
    rc = jnp.concatenate([ones, cos, cos, ztail], axis=-1)
    rs = jnp.concatenate([zhead, -sin, sin, ztail], axis=-1)
    return rc, rs


def _hy_proj(x, mod_l, g, w_in, conv_w, q_norm, kv_norm, hy_w, ropes):
    b, s, d = x.shape
    tm = min(ROW_TILE, s)
    hp = HEAD_PAD
    qw = MLA_HEADS * hp
    row = lambda width: pl.BlockSpec((None, tm, width), lambda i, j: (i, j, 0))
    q_scale = (MLA_NOPE + MLA_ROPE) ** -0.5 * LOG2E
    return pl.pallas_call(
        functools.partial(_hy_proj_kernel, q_scale=q_scale),
        out_shape=(
            jax.ShapeDtypeStruct((b, s, CONV_WIDTH), BF16),
            jax.ShapeDtypeStruct((b, s, qw), BF16),
            jax.ShapeDtypeStruct((b, s, qw), BF16),
            jax.ShapeDtypeStruct((b, s, qw), BF16),
        ),
        grid=(b, s // tm),
        in_specs=[
            row(d),
            pl.BlockSpec((None, N_ADA, d), lambda i, j: (i, 0, 0)),
            _const_spec((1, d)),
            _const_spec(w_in.shape),
            _const_spec(conv_w.shape),
            _const_spec((1, Q_LORA)),
            _const_spec((1, KV_LORA)),
        ] + [_const_spec(w.shape) for w in hy_w] + [row(hp), row(hp)],
        out_specs=(row(CONV_WIDTH), row(qw), row(qw), row(qw)),
        scratch_shapes=[pltpu.VMEM((8, CONV_WIDTH), F32)],
        compiler_params=_cparams(("arbitrary", "arbitrary")),
        name="hy_proj",
    )(x, mod_l, g.reshape(1, d), w_in, conv_w, q_norm.reshape(1, -1), kv_norm.reshape(1, -1),
      *hy_w, *ropes)


def _prefix_attention(q, k_ref, v_ref, cols, n_main, n_edge, edge_bias):
    edge = slice(n_main, n_main + n_edge)
    s_edge = edge_bias(_dot_nt(q, k_ref[edge, cols]))
    m = jnp.max(s_edge, axis=-1, keepdims=True)
    if n_main == 0:
        return _dot(jnp.exp2(s_edge - m).astype(BF16), v_ref[edge, cols])
    s_main = _dot_nt(q, k_ref[0:n_main, cols])
    m = jnp.maximum(m, jnp.max(s_main, axis=-1, keepdims=True))
    return (_dot(jnp.exp2(s_main - m).astype(BF16), v_ref[0:n_main, cols])
            + _dot(jnp.exp2(s_edge - m).astype(BF16), v_ref[edge, cols]))


def _mla_attn_body(q_ref, k_ref, v_ref, o_ref, n_main):
    tq = q_ref.shape[0]
    hp = HEAD_PAD
    n_heads = q_ref.shape[1] // hp
    rel = (lax.broadcasted_iota(jnp.int32, (1, tq), 1)
           <= lax.broadcasted_iota(jnp.int32, (tq, 1), 0))
    diag_bias = jnp.where(rel, 0.0, -MASK_BIG)
    outs = []
    for hh in range(n_heads):
        cols = slice(hh * hp, (hh + 1) * hp)
        acc = _prefix_attention(q_ref[:, cols], k_ref, v_ref, cols, n_main, tq, lambda s: s + diag_bias)
        outs.append(acc * _row_recip(acc[:, MLA_V:MLA_V + 1]))
    lane = lax.broadcasted_iota(jnp.int32, (1, hp), 1)
    for pair in range(n_heads // 2):
        both = jnp.where(lane < MLA_V, outs[2 * pair], pltpu.roll(outs[2 * pair + 1], MLA_V, 1))
        o_ref[:, pair * hp:(pair + 1) * hp] = both.astype(BF16)


def _mla_attn_kernel(q_ref, k_ref, v_ref, o_ref):
    tq = q_ref.shape[0]
    qi = pl.program_id(2)
    for j in range(k_ref.shape[0] // tq):
        pl.when(qi == j)(functools.partial(_mla_attn_body, q_ref, k_ref, v_ref, o_ref, j * tq))


def _mla_attn(q, k, v):
    b, s, _ = q.shape
    tq = min(MLA_TQ, s)
    qw = MLA_HEADS_PER_STEP * HEAD_PAD
    ow = MLA_HEADS_PER_STEP * MLA_V
    return pl.pallas_call(
        _mla_attn_kernel,
        out_shape=jax.ShapeDtypeStruct((b, s, MLA_HEADS * MLA_V), BF16),
        grid=(b, MLA_HEADS // MLA_HEADS_PER_STEP, s // tq),
        in_specs=[
            pl.BlockSpec((None, tq, qw), lambda i, h, j: (i, j, h)),
            pl.BlockSpec((None, s, qw), lambda i, h, j: (i, 0, h)),
            pl.BlockSpec((None, s, qw), lambda i, h, j: (i, 0, h)),
        ],
        out_specs=pl.BlockSpec((None, tq, ow), lambda i, h, j: (i, j, h)),
        compiler_params=_cparams(("arbitrary", "arbitrary", "arbitrary")),
        name="mla_attn",
    )(q, k, v)


def _nsa_proj_kernel(x_ref, mod_ref, g_ref, win_ref, gb_ref, qt_ref, kvc_ref, ks_ref, vst_ref,
                     kw_ref, vw_ref, gt_ref, *, q_scale):
    tm = x_ref.shape[0]
    hp, dk = HEAD_PAD, NSA_DK
    pad = hp - dk
    h = _modulate(x_ref[...], g_ref[...], mod_ref, 3).astype(BF16)
    z = _dot(h, win_ref[...])
    qd = NSA_HEADS * dk
    kw = NSA_KV_W
    zeros = jnp.zeros((tm, pad), BF16)
    zero_rows = jnp.zeros((pad, tm), BF16)
    for pair in range(NSA_HEADS // 2):
        qt = (z[:, pair * hp:(pair + 1) * hp] * q_scale).T.astype(BF16)
        for half in range(2):
            hd = 2 * pair + half
            qt_ref[hd * hp:hd * hp + dk, :] = qt[half * dk:(half + 1) * dk, :]
            qt_ref[hd * hp + dk:(hd + 1) * hp, :] = zero_rows
    kvc_ref[...] = z[:, qd:qd + kw].astype(BF16)
    pos = pl.program_id(1) * tm + lax.broadcasted_iota(jnp.int32, (tm, 1), 0)
    lane = lax.broadcasted_iota(jnp.int32, (1, pad), 1)
    blk_onehot = jnp.where(lax.shift_right_logical(pos, int(np.log2(SLC_BLOCK))) == lane, 1.0, 0.0)
    blk_onehot = blk_onehot.astype(BF16)
    ones_col = jnp.broadcast_to(jnp.where(lane == 0, 1.0, 0.0), (tm, pad)).astype(BF16)
    kvs = z[:, qd + kw:qd + 2 * kw].astype(BF16)
    kvw = z[:, qd + 2 * kw:qd + 3 * kw].astype(BF16)
    for g in range(NSA_KV_HEADS):
        lo, mid, hi = g * hp, g * hp + dk, (g + 1) * hp
        kc, vc = slice(g * dk, (g + 1) * dk), slice((NSA_KV_HEADS + g) * dk, (NSA_KV_HEADS + g + 1) * dk)
        ks_ref[:, lo:mid] = kvs[:, kc]
        ks_ref[:, mid:hi] = blk_onehot
        kw_ref[:, lo:mid] = kvw[:, kc]
        kw_ref[:, mid:hi] = zeros
        vw_ref[:, lo:mid] = kvw[:, vc]
        vw_ref[:, mid:hi] = ones_col
    v_off = qd + kw + NSA_KV_HEADS * dk
    vt = z[:, v_off:v_off + NSA_KV_HEADS * dk].T.astype(BF16)
    ones_row = jnp.where(lax.broadcasted_iota(jnp.int32, (pad, 1), 0) == 0, 1.0, 0.0)
    ones_row = jnp.broadcast_to(ones_row, (pad, tm)).astype(BF16)
    for g in range(NSA_KV_HEADS):
        vst_ref[g * hp:g * hp + dk, :] = vt[g * dk:(g + 1) * dk, :]
        vst_ref[g * hp + dk:(g + 1) * hp, :] = ones_row
    gt_ref[...] = _sigmoid(z[:, qd + 3 * kw:] + gb_ref[...]).T


def _nsa_proj(x, mod_l, g, w_in, gate_b):
    b, s, d = x.shape
    tm = min(ROW_TILE, s)
    assert s // SLC_BLOCK <= HEAD_PAD - NSA_DK
    qw = NSA_HEADS * HEAD_PAD
    kvw = NSA_KV_HEADS * HEAD_PAD
    ng = gate_b.shape[-1]
    row = lambda width: pl.BlockSpec((None, tm, width), lambda i, j: (i, j, 0))
    col = lambda height: pl.BlockSpec((None, height, tm), lambda i, j: (i, 0, j))
    sds = lambda width, dt=BF16: jax.ShapeDtypeStruct((b, s, width), dt)
    sds_t = lambda height, dt=BF16: jax.ShapeDtypeStruct((b, height, s), dt)
    return pl.pallas_call(
        functools.partial(_nsa_proj_kernel, q_scale=NSA_DK ** -0.5 * LOG2E),
        out_shape=(sds_t(qw), sds(NSA_KV_W), sds(kvw), sds_t(kvw), sds(kvw), sds(kvw), sds_t(ng, F32)),
        grid=(b, s // tm),
        in_specs=[
            row(d),
            pl.BlockSpec((None, N_ADA, d), lambda i, j: (i, 0, 0)),
            _const_spec((1, d)),
            _const_spec(w_in.shape),
            _const_spec((1, ng)),
        ],
        out_specs=(col(qw), row(NSA_KV_W), row(kvw), col(kvw), row(kvw), row(kvw), col(ng)),
        compiler_params=_cparams(("arbitrary", "arbitrary")),
        name="nsa_proj",
    )(x, mod_l, g.reshape(1, d), w_in, gate_b.reshape(1, ng))


def _nsa_cmp_kernel(ch_ref, w1a_ref, w1b_ref, pea_ref, peb_ref, w2_ref, k_ref, vt_ref):
    ch = ch_ref[...]
    n = ch.shape[0]
    first = _dot(ch, w1a_ref[...])
    second = _dot(ch, w1b_ref[...])
    bias = _dot(pea_ref[...], w1a_ref[...]) + _dot(peb_ref[...], w1b_ref[...])
    hid = first + pltpu.roll(second, n - 1, 0) + bias[0:1, :]
    act = (hid * _sigmoid(hid)).astype(BF16)
    kv = _dot(act, w2_ref[...])
    half = kv.shape[1] // 2
    k_ref[...] = kv[:, :half].astype(BF16)
    vt_ref[...] = kv[:, half:].T.astype(BF16)


def _nsa_cmp_weights(cmp_pe, cmp_w1, cmp_w2):
    ncomp = 2 * NSA_KV_HEADS
    half = CMP_BLOCK // 2
    kv_of = np.arange(ncomp) // NSA_KV_HEADS
    eye = jnp.eye(ncomp, dtype=F32)
    w1 = cmp_w1.reshape(2, 2, half, NSA_DK, CMP_HID)[kv_of]
    big1 = jnp.einsum("chldj,ce->hlcdej", w1, eye).reshape(2, half * ncomp * NSA_DK, ncomp * CMP_HID)
    w2 = jnp.pad(cmp_w2[kv_of], ((0, 0), (0, 0), (0, HEAD_PAD - NSA_DK)))
    big2 = jnp.einsum("cjd,ce->cjed", w2, eye).reshape(ncomp * CMP_HID, ncomp * HEAD_PAD)
    pe = cmp_pe.reshape(2, 2, half, NSA_DK)[kv_of]
    pe = jnp.transpose(pe, (1, 2, 0, 3)).reshape(2, 1, half * ncomp * NSA_DK)
    pe = jnp.broadcast_to(pe, (2, 8, half * ncomp * NSA_DK))
    return (big1[0].astype(BF16), big1[1].astype(BF16), pe[0].astype(BF16), pe[1].astype(BF16),
            big2.astype(BF16))


def _nsa_cmp(kvc, w1a, w1b, pea, peb, w2):
    b, s, kw = kvc.shape
    nchunk = s // CMP_STRIDE
    ow = w2.shape[1]
    chunks = kvc.reshape(b, nchunk, CMP_STRIDE * kw)
    half = ow // 2
    return pl.pallas_call(
        _nsa_cmp_kernel,
        out_shape=(jax.ShapeDtypeStruct((b, nchunk, half), BF16),
                   jax.ShapeDtypeStruct((b, half, nchunk), BF16)),
        grid=(b,),
        in_specs=[
            pl.BlockSpec((None, nchunk, CMP_STRIDE * kw), lambda i: (i, 0, 0)),
            _const_spec(w1a.shape), _const_spec(w1b.shape),
            _const_spec(pea.shape), _const_spec(peb.shape), _const_spec(w2.shape),
        ],
        out_specs=(pl.BlockSpec((None, nchunk, half), lambda i: (i, 0, 0)),
                   pl.BlockSpec((None, half, nchunk), lambda i: (i, 0, 0))),
        compiler_params=_cparams(("arbitrary",)),
        name="nsa_cmp",
    )(chunks, w1a, w1b, pea, peb, w2)


def _add_per_head(s, bias, rep):
    tq = bias.shape[1]
    return jnp.concatenate([s[:, r * tq:(r + 1) * tq] + bias for r in range(rep)], axis=1)


def _prefix_attention_t(q_t, k_ref, vt_ref, cols, n_main, n_edge, edge_bias):
    edge = slice(n_main, n_main + n_edge)
    s_edge = edge_bias(_dot(k_ref[edge, cols], q_t))
    m = jnp.max(s_edge, axis=0, keepdims=True)
    if n_main == 0:
        return _dot(vt_ref[cols, edge], jnp.exp2(s_edge - m).astype(BF16))
    s_main = _dot(k_ref[0:n_main, cols], q_t)
    m = jnp.maximum(m, jnp.max(s_main, axis=0, keepdims=True))
    return (_dot(vt_ref[cols, 0:n_main], jnp.exp2(s_main - m).astype(BF16))
            + _dot(vt_ref[cols, edge], jnp.exp2(s_edge - m).astype(BF16)))


def _nsa_attn_kernel(qt_ref, kc_ref, vct_ref, ks_ref, vst_ref, kw_ref, vw_ref, gt_ref, aggt_ref,
                     placet_ref, o_ref, *, tk, n_sel):
    tq = qt_ref.shape[1]
    per = tk // tq
    qi = pl.program_id(1)
    for j in range(ks_ref.shape[0] // tk):
        in_class = (qi >= j * per) & (qi < (j + 1) * per)
        pl.when(in_class)(functools.partial(
            _nsa_attn_body, qt_ref, kc_ref, vct_ref, ks_ref, vst_ref, kw_ref, vw_ref, gt_ref,
            aggt_ref, placet_ref, o_ref, tk=tk, n_sel=n_sel, n_main=j * tk))


def _nsa_attn_body(qt_ref, kc_ref, vct_ref, ks_ref, vst_ref, kw_ref, vw_ref, gt_ref, aggt_ref,
                   placet_ref, o_ref, *, tk, n_sel, n_main):
    tq = qt_ref.shape[1]
    seq = ks_ref.shape[0]
    ncp = kc_ref.shape[0]
    ns = aggt_ref.shape[0]
    hp, dk, rep = HEAD_PAD, NSA_DK, NSA_GROUP
    q0 = pl.program_id(1) * tq
    t = q0 + lax.broadcasted_iota(jnp.int32, (1, tq), 1)
    hpc = NSA_HEADS_PER_CHAIN
    n_chains = rep // hpc
    t_heads = jnp.concatenate([t] * hpc, axis=1)

    blk = lax.broadcasted_iota(jnp.int32, (ns, 1), 0)
    cur = lax.shift_right_logical(t, int(np.log2(SLC_BLOCK)))
    forced = (blk == 0) | (blk == cur) | (blk == cur - 1)
    causal_blk = blk * SLC_BLOCK <= t
    cmp_end = lax.broadcasted_iota(jnp.int32, (ncp, 1), 0) * CMP_STRIDE + (CMP_BLOCK - 1)
    cmp_mask = (cmp_end <= t_heads) & (cmp_end < seq)
    win_len = WINDOW + tq
    w_start = pl.multiple_of(jnp.maximum(q0 - WINDOW, 0), tq)
    wpos = w_start + lax.broadcasted_iota(jnp.int32, (win_len, 1), 0)
    win_bias = jnp.where((wpos <= t) & (wpos > t - WINDOW), 0.0, -MASK_BIG)
    last_pos = n_main + lax.broadcasted_iota(jnp.int32, (tk, 1), 0)
    last_bias = jnp.where(last_pos <= t, 0.0, -MASK_BIG)
    pad_row = lax.broadcasted_iota(jnp.int32, (hp, 1), 0)
    sel_offset = jnp.where((pad_row >= dk) & (pad_row < dk + ns), MASK_BIG, 0.0)
    gates = gt_ref[...]

    head_out = []
    for g in range(NSA_KV_HEADS):
        kv_cols = slice(g * hp, (g + 1) * hp)
        q_heads = [qt_ref[(g * rep + r) * hp:(g * rep + r + 1) * hp, :] for r in range(rep)]
        q_chains = [jnp.concatenate(q_heads[c * hpc:(c + 1) * hpc], axis=1) for c in range(n_chains)]

        o_c, p_sum = [], None
        for qc in q_chains:
            s = jnp.where(cmp_mask, _dot(kc_ref[:, kv_cols], qc), NEG_INF)
            p = jnp.where(cmp_mask, jnp.exp2(s - jnp.max(s, axis=0, keepdims=True)), 0.0)
            p_c = p * _row_recip(jnp.sum(p, axis=0, keepdims=True))
            o_c.append(_dot(vct_ref[kv_cols, :], p_c.astype(BF16)))
            for r in range(hpc):
                part = p_c[:, r * tq:(r + 1) * tq]
                p_sum = part if p_sum is None else p_sum + part

        vw_t = vw_ref[pl.ds(w_start, win_len), kv_cols].astype(F32).T.astype(BF16)
        acc_w = []
        for qc in q_chains:
            s = _add_per_head(_dot(kw_ref[pl.ds(w_start, win_len), kv_cols], qc), win_bias, hpc)
            p = jnp.exp2(s - jnp.max(s, axis=0, keepdims=True))
            acc_w.append(_dot(vw_t, p.astype(BF16)))

        agg_t = aggt_ref[...]
        hi, mid, lo = _split3(p_sum)
        imp = _dot(agg_t, hi) + _dot(agg_t, mid) + _dot(agg_t, lo)
        val = jnp.where(forced, FORCE_SCORE, jnp.where(causal_blk, imp, NEG_INF))
        rank = jnp.zeros((ns, tq), F32)
        for i in range(ns):
            other = val[i:i + 1, :]
            beats = (other > val) | ((other == val) & (blk > i))
            rank = rank + jnp.where(beats, 1.0, 0.0)
        sel = jnp.where((rank < n_sel) & causal_blk, 1.0, 0.0).astype(BF16)
        sel_pad = _dot(placet_ref[...], sel) - sel_offset

        for c in range(n_chains):
            q_sel = jnp.concatenate([(qh.astype(F32) + sel_pad).astype(BF16)
                                     for qh in q_heads[c * hpc:(c + 1) * hpc]], axis=1)
            acc_s = _prefix_attention_t(q_sel, ks_ref, vst_ref, kv_cols, n_main, tk,
                                        lambda sc: _add_per_head(sc, last_bias, hpc))
            r_s = _row_recip(acc_s[dk:dk + 1, :])
            r_w = _row_recip(acc_w[c][dk:dk + 1, :])
            for r in range(hpc):
                cols = slice(r * tq, (r + 1) * tq)
                gc = g * 3 * rep + c * hpc + r
                o_h = (gates[gc:gc + 1, :] * o_c[c][:, cols]
                       + (gates[gc + rep:gc + rep + 1, :] * r_s[:, cols]) * acc_s[:, cols]
                       + (gates[gc + 2 * rep:gc + 2 * rep + 1, :] * r_w[:, cols]) * acc_w[c][:, cols])
                head_out.append(o_h[0:dk, :])
    o_ref[...] = jnp.concatenate(head_out, axis=0).T.astype(BF16)


def _nsa_tables(seq):
    nchunk = seq // CMP_STRIDE
    ns = seq // SLC_BLOCK
    cs = np.arange(nchunk)[:, None] * CMP_STRIDE
    ss = np.arange(ns)[None, :] * SLC_BLOCK
    overlap = np.clip(np.minimum(cs + CMP_BLOCK, ss + SLC_BLOCK) - np.maximum(cs, ss), 0, None)
    agg_t = (overlap.astype(np.float32) / CMP_BLOCK).T
    place_t = np.zeros((HEAD_PAD, ns), np.float32)
    place_t[NSA_DK + np.arange(ns), np.arange(ns)] = MASK_BIG
    return jnp.asarray(agg_t, BF16), jnp.asarray(place_t, BF16)


def _nsa_attn(qt, kc, vct, ks, vst, kw, vw, gt):
    b, _, s = qt.shape
    tq = min(NSA_TQ, s)
    tk = min(NSA_TK, s)
    assert s >= WINDOW + tq and s % tk == 0 and WINDOW % tq == 0
    agg_t, place_t = _nsa_tables(s)
    whole = lambda a: pl.BlockSpec((None,) + a.shape[1:], lambda i, j: (i, 0, 0))
    col = lambda a: pl.BlockSpec((None, a.shape[1], tq), lambda i, j: (i, 0, j))
    ow = NSA_HEADS * NSA_DK
    return pl.pallas_call(
        functools.partial(_nsa_attn_kernel, tk=tk, n_sel=min(N_SEL, s // SLC_BLOCK)),
        out_shape=jax.ShapeDtypeStruct((b, s, ow), BF16),
        grid=(b, s // tq),
        in_specs=[
            col(qt), whole(kc), whole(vct), whole(ks), whole(vst), whole(kw), whole(vw), col(gt),
            _const_spec(agg_t.shape), _const_spec(place_t.shape),
        ],
        out_specs=pl.BlockSpec((None, tq, ow), lambda i, j: (i, j, 0)),
        compiler_params=_cparams(("arbitrary", "arbitrary")),
        name="nsa_attn",
    )(qt, kc, vct, ks, vst, kw, vw, gt, agg_t, place_t)


def _nsa_gate_layout(w_in, gate_b):
    qd = NSA_HEADS * NSA_DK
    o = qd + 3 * NSA_KV_W
    perm = np.arange(3 * NSA_HEADS).reshape(NSA_KV_HEADS, NSA_GROUP, 3).transpose(0, 2, 1).reshape(-1)
    extra = HEAD_PAD - perm.size
    w_in = jnp.concatenate([w_in[:, :o], jnp.pad(w_in[:, o:][:, perm], ((0, 0), (0, extra)))], axis=1)
    return w_in, jnp.pad(gate_b[perm], (0, extra))


def kernel(x, c, positions, ada_w, ada_b, norm_g, final_g, ff_w13, ff_w2, hy_w_in, hy_conv_w,
           hy_q_norm, hy_kv_norm, hy_w_uq, hy_w_ukv, hy_w_out, nsa_w_in, nsa_cmp_pe, nsa_cmp_w1,
           nsa_cmp_w2, nsa_gate_b, nsa_w_out):
    depth = ada_w.shape[0]
    mod = _ada_mod(c, ada_w, ada_b)
    ropes = _rope_tables(positions)
    w13 = ff_w13.astype(BF16)
    w2 = ff_w2.astype(BF16)
    for l in range(depth):
        m = l // 2
        x = _ffn(x, mod[l], norm_g[l, 0], w13, w2, l, 0, final_g, k0=0, final=False)
        if l % 2 == 0:
            w_in = hy_w_in[m].astype(BF16)
            w_in = jnp.pad(w_in, ((0, 0), (0, -w_in.shape[1] % HEAD_PAD)))
            y_conv, q, k, v = _hy_proj(x, mod[l], norm_g[l, 1], w_in,
                                       hy_conv_w[m], hy_q_norm[m], hy_kv_norm[m],
                                       _hy_weights(hy_w_uq[m], hy_w_ukv[m]), ropes)
            y_att = _mla_attn(q, k, v)
            w_out = hy_w_out[m].astype(BF16)
            parts, w_parts = [y_conv, y_att], [w_out[:CONV_WIDTH], w_out[CONV_WIDTH:]]
        else:
            w_in, gate_b = _nsa_gate_layout(nsa_w_in[m], nsa_gate_b[m])
            qt, kvc, ks, vst, kw, vw, gt = _nsa_proj(x, mod[l], norm_g[l, 1], w_in.astype(BF16), gate_b)
            kc, vct = _nsa_cmp(kvc, *_nsa_cmp_weights(nsa_cmp_pe[m], nsa_cmp_w1[m], nsa_cmp_w2[m]))
            o = _nsa_attn(qt, kc, vct, ks, vst, kw, vw, gt)
            parts, w_parts = [o], [nsa_w_out[m].astype(BF16)]
        x = _ffn(x, mod[l], norm_g[l, 2], w13, w2, l, 1, final_g, k0=6, final=(l == depth - 1),
                 parts=parts, w_parts=w_parts)
    return x
```

```python
import functools

import jax
import jax.numpy as jnp
import numpy as np
from jax import lax
from jax.experimental import pallas as pl
from jax.experimental.pallas import tpu as pltpu

F32 = jnp.float32
BF16 = jnp.bfloat16

N_ADA = 9
EPS = 1e-6
NEG_INF = -1e30
CONV_WIDTH = 512
CONV_TAPS = 3
MLA_HEADS = 8
MLA_NOPE = 64
MLA_ROPE = 32
MLA_V = 64
Q_LORA = 256
KV_LORA = 128
ROPE_THETA = 10000.0
NSA_HEADS = 16
NSA_KV_HEADS = 2
NSA_GROUP = NSA_HEADS // NSA_KV_HEADS
NSA_DK = 64
CMP_BLOCK = 32
CMP_STRIDE = 16
CMP_HID = 128
SLC_BLOCK = 64
N_SEL = 8
WINDOW = 512
FORCE_SCORE = 1e4
NSA_KV_W = 2 * NSA_KV_HEADS * NSA_DK

HEAD_PAD = 128
VMEM_LIMIT = 56 * 1024 * 1024
ROW_TILE = 512
PROJ_SUBTILES = 4
MLA_TQ = 512
MLA_HEADS_PER_STEP = 4
NSA_TQ = 128
NSA_TK = 512

LOG2E = float(np.log2(np.e))
MASK_BIG = float(2.0 ** 100)


def _cparams(sem):
    return pltpu.CompilerParams(dimension_semantics=sem, vmem_limit_bytes=VMEM_LIMIT)


def _const_spec(shape):
    nd = len(shape)
    return pl.BlockSpec(shape, lambda *_: (0,) * nd, pipeline_mode=pl.Buffered(1))


def _sigmoid(v):
    return 1.0 / (1.0 + jnp.exp(-v))


def _rms(v, g):
    return v * lax.rsqrt(jnp.mean(v * v, axis=-1, keepdims=True) + EPS) * g


def _modulate(x, g, mod_ref, k0):
    shift = mod_ref[k0:k0 + 1, :]
    scale = mod_ref[k0 + 1:k0 + 2, :]
    return _rms(x, g) * (1.0 + scale) + shift


def _dot(a, b):
    return jnp.dot(a, b, preferred_element_type=F32)


def _dot_nt(a, b):
    return lax.dot_general(a, b, (((1,), (1,)), ((), ())), preferred_element_type=F32)


def _split3(a):
    hi = a.astype(BF16)
    r1 = a - hi.astype(F32)
    mid = r1.astype(BF16)
    lo = (r1 - mid.astype(F32)).astype(BF16)
    return hi, mid, lo


def _row_recip(v):
    return 1.0 / jnp.maximum(v, 1e-20)


def _ada_kernel(c_ref, w_ref, b_ref, o_ref):
    c = c_ref[...]
    ca = (c * _sigmoid(c)).astype(BF16)
    o_ref[...] = _dot(ca, w_ref[...].astype(BF16)) + b_ref[...]


def _ada_mod(c, ada_w, ada_b):
    depth, d, n = ada_w.shape
    b = c.shape[0]
    tn = n // 8
    out = pl.pallas_call(
        _ada_kernel,
        out_shape=jax.ShapeDtypeStruct((depth, b, n), F32),
        grid=(depth, n // tn),
        in_specs=[
            pl.BlockSpec((b, d), lambda l, j: (0, 0)),
            pl.BlockSpec((None, d, tn), lambda l, j: (l, 0, j)),
            pl.BlockSpec((None, 1, tn), lambda l, j: (l, 0, j)),
        ],
        out_specs=pl.BlockSpec((None, b, tn), lambda l, j: (l, 0, j)),
        compiler_params=_cparams(("arbitrary", "arbitrary")),
        name="ada_mod",
    )(c, ada_w, ada_b.reshape(depth, 1, n))
    return out.reshape(depth, b, N_ADA, d)


def _ffn_kernel(*refs, k0, d_ff, final, n_parts):
    x_ref, mod_ref, g_ref = refs[:3]
    part_refs = refs[3:3 + n_parts]
    wout_refs = refs[3 + n_parts:3 + 2 * n_parts]
    w13_ref, w2_ref, fg_ref, o_ref = refs[3 + 2 * n_parts:]
    x = x_ref[...]
    if n_parts:
        y = _dot(part_refs[0][...], wout_refs[0][...])
        for p_ref, w_ref in zip(part_refs[1:], wout_refs[1:]):
            y = y + _dot(p_ref[...], w_ref[...])
        x = x + mod_ref[5:6, :] * y
    h = _modulate(x, g_ref[...], mod_ref, k0).astype(BF16)
    ab = _dot(h, w13_ref[...])
    a = ab[:, :d_ff]
    b = ab[:, d_ff:]
    u = (a * _sigmoid(a) * b).astype(BF16)
    y = _dot(u, w2_ref[...])
    out = x + (0.5 * mod_ref[k0 + 2:k0 + 3, :]) * y
    if final:
        out = _rms(out, fg_ref[...])
    o_ref[...] = out


def _ffn(x, mod_l, g, w13, w2, layer, which, final_g, *, k0, final, parts=(), w_parts=()):
    b, s, d = x.shape
    d_ff = w2.shape[2]
    tm = min(ROW_TILE, s)
    kern = functools.partial(_ffn_kernel, k0=k0, d_ff=d_ff, final=final, n_parts=len(parts))
    pick = lambda *_: (layer, which, 0, 0)
    row = lambda width: pl.BlockSpec((None, tm, width), lambda i, j: (i, j, 0))
    return pl.pallas_call(
        kern,
        out_shape=jax.ShapeDtypeStruct(x.shape, F32),
        grid=(b, s // tm),
        in_specs=[
            row(d),
            pl.BlockSpec((None, N_ADA, d), lambda i, j: (i, 0, 0)),
            _const_spec((1, d)),
        ] + [row(p.shape[-1]) for p in parts] + [_const_spec(w.shape) for w in w_parts] + [
            pl.BlockSpec((None, None, d, 2 * d_ff), pick, pipeline_mode=pl.Buffered(1)),
            pl.BlockSpec((None, None, d_ff, d), pick, pipeline_mode=pl.Buffered(1)),
            _const_spec((1, d)),
        ],
        out_specs=row(d),
        compiler_params=_cparams(("arbitrary", "arbitrary")),
        name="ffn",
    )(x, mod_l, g.reshape(1, d), *parts, *w_parts, w13, w2, final_g.reshape(1, d))


def _rope_swap(t):
    width = t.shape[-1]
    half = MLA_ROPE // 2
    lane = lax.broadcasted_iota(jnp.int32, (1, width), 1) & (HEAD_PAD - 1)
    return jnp.where(lane < MLA_NOPE + half, pltpu.roll(t, width - half, 1), pltpu.roll(t, half, 1))


def _hy_proj_kernel(x_ref, mod_ref, g_ref, win_ref, convw_ref, qn_ref, kvn_ref, wq_ref,
                    wk_ref, wv_ref, rc_ref, rs_ref,
                    yconv_ref, q_ref, k_ref, v_ref, carry_ref, *, q_scale):
    tm = x_ref.shape[0]
    cw = CONV_WIDTH

    @pl.when(pl.program_id(1) == 0)
    def _():
        carry_ref[...] = jnp.zeros_like(carry_ref)

    sub = tm // PROJ_SUBTILES
    rows = [slice(p * sub, (p + 1) * sub) for p in range(PROJ_SUBTILES)]
    st = [{} for _ in range(PROJ_SUBTILES)]
    o = 3 * cw
    n_rep = q_ref.shape[-1] // HEAD_PAD

    def project_in(p):
        h = _modulate(x_ref[rows[p], :], g_ref[...], mod_ref, 3).astype(BF16)
        st[p]["z"] = _dot(h, win_ref[...])

    def conv(p):
        z = st[p]["z"]
        u, gate_c, gate_b = z[:, :cw], z[:, cw:2 * cw], z[:, 2 * cw:3 * cw]
        prev = st[p - 1]["tail"] if p else carry_ref[...]
        v = gate_c * u
        row = lax.broadcasted_iota(jnp.int32, (sub, 1), 0)
        v1 = jnp.where(row == 0, prev[7:8, :], pltpu.roll(v, 1, 0))
        v2 = jnp.where(row == 0, prev[6:7, :], jnp.where(row == 1, prev[7:8, :], pltpu.roll(v, 2, 0)))
        w = convw_ref[...]
        yconv_ref[rows[p], :] = (gate_b * (w[0:1, :] * v2 + w[1:2, :] * v1 + w[2:3, :] * v)).astype(BF16)
        st[p]["tail"] = v[sub - 8:, :]

    def latent_norms(p):
        z = st[p]["z"]
        st[p]["qn"] = _rms(z[:, o:o + Q_LORA], qn_ref[...]).astype(BF16)
        st[p]["kvn"] = _rms(z[:, o + Q_LORA:o + Q_LORA + KV_LORA], kvn_ref[...]).astype(BF16)
        st[p]["kr"] = z[:, o + Q_LORA + KV_LORA:]

    def project_heads(p):
        st[p]["q"] = _dot(st[p].pop("qn"), wq_ref[...])
        kvn = st[p].pop("kvn")
        st[p]["k"] = _dot(kvn, wk_ref[...])
        st[p]["v"] = _dot(kvn, wv_ref[...])

    def rotary_store(p):
        rc = rc_ref[rows[p], :]
        rs = rs_ref[rows[p], :]
        q = st[p].pop("q")
        q = q * jnp.concatenate([rc] * n_rep, axis=1) + _rope_swap(q) * jnp.concatenate([rs] * n_rep, axis=1)
        kr = pltpu.roll(st[p].pop("kr"), MLA_NOPE, 1)
        kr = kr * rc + _rope_swap(kr) * rs
        k = st[p].pop("k") + jnp.concatenate([kr] * n_rep, axis=1)
        q_ref[rows[p], :] = (q * q_scale).astype(BF16)
        k_ref[rows[p], :] = k.astype(BF16)
        lane = lax.broadcasted_iota(jnp.int32, (1, v_ref.shape[-1]), 1)
        ones_col = jnp.where((lane & (HEAD_PAD - 1)) == MLA_V, 1.0, 0.0)
        v_ref[rows[p], :] = (st[p].pop("v") + ones_col).astype(BF16)

    project_in(0)
    for p in range(PROJ_SUBTILES):
        if p + 1 < PROJ_SUBTILES:
            project_in(p + 1)
        latent_norms(p)
        conv(p)
        project_heads(p)
        if p:
            rotary_store(p - 1)
    rotary_store(PROJ_SUBTILES - 1)
    carry_ref[...] = st[PROJ_SUBTILES - 1]["tail"]


def _hy_weights(w_uq, w_ukv):
    hp = HEAD_PAD
    pad_heads = lambda w: jnp.pad(w, ((0, 0), (0, 0), (0, hp - w.shape[-1]))).reshape(w.shape[0], -1)
    wq = pad_heads(w_uq.reshape(Q_LORA, MLA_HEADS, MLA_NOPE + MLA_ROPE))
    wkv = w_ukv.reshape(KV_LORA, MLA_HEADS, MLA_NOPE + MLA_V)
    wk = pad_heads(wkv[..., :MLA_NOPE])
    wv = pad_heads(wkv[..., MLA_NOPE:])
    return wq.astype(BF16), wk.astype(BF16), wv.astype(BF16)


def _rope_tables(positions):
    half = MLA_ROPE // 2
    inv = ROPE_THETA ** (-jnp.arange(half, dtype=F32) / half)
    ang = positions.astype(F32)[..., None] * inv
    cos, sin = jnp.cos(ang), jnp.sin(ang)
    lead = positions.shape
    ones = jnp.ones(lead + (MLA_NOPE,), F32)
    ztail = jnp.zeros(lead + (HEAD_PAD - MLA_NOPE - MLA_ROPE,), F32)
    zhead = jnp.zeros(lead + (MLA_NOPE,), F32)
    rc = jnp.concatenate([ones, cos, cos, ztail], axis=-1)
    rs = jnp.concatenate([zhead, -sin, sin, ztail], axis=-1)
    return rc, rs


def _hy_proj(x, mod_l, g, w_in, conv_w, q_norm, kv_norm, hy_w, ropes):
    b, s, d = x.shape
    tm = min(ROW_TILE, s)
    hp = HEAD_PAD
    qw = MLA_HEADS * hp
    row = lambda width: pl.BlockSpec((None, tm, width), lambda i, j: (i, j, 0))
    q_scale = (MLA_NOPE + MLA_ROPE) ** -0.5 * LOG2E
    return pl.pallas_call(
        functools.partial(_hy_proj_kernel, q_scale=q_scale),
        out_shape=(
            jax.ShapeDtypeStruct((b, s, CONV_WIDTH), BF16),
            jax.ShapeDtypeStruct((b, s, qw), BF16),
            jax.ShapeDtypeStruct((b, s, qw), BF16),
            jax.ShapeDtypeStruct((b, s, qw), BF16),
        ),
        grid=(b, s // tm),
        in_specs=[
            row(d),
            pl.BlockSpec((None, N_ADA, d), lambda i, j: (i, 0, 0)),
            _const_spec((1, d)),
            _const_spec(w_in.shape),
            _const_spec(conv_w.shape),
            _const_spec((1, Q_LORA)),
            _const_spec((1, KV_LORA)),
        ] + [_const_spec(w.shape) for w in hy_w] + [row(hp), row(hp)],
        out_specs=(row(CONV_WIDTH), row(qw), row(qw), row(qw)),
        scratch_shapes=[pltpu.VMEM((8, CONV_WIDTH), F32)],
        compiler_params=_cparams(("arbitrary", "arbitrary")),
        name="hy_proj",
    )(x, mod_l, g.reshape(1, d), w_in, conv_w, q_norm.reshape(1, -1), kv_norm.reshape(1, -1),
      *hy_w, *ropes)


def _mla_attn_body(q_ref, k_ref, v_ref, o_ref, n_main):
    tq = q_ref.shape[0]
    hp = HEAD_PAD
    n_heads = q_ref.shape[1] // hp
    rel = (lax.broadcasted_iota(jnp.int32, (1, tq), 1)
           <= lax.broadcasted_iota(jnp.int32, (tq, 1), 0))
    diag_bias = jnp.where(rel, 0.0, -MASK_BIG)
    edge = slice(n_main, n_main + tq)
    cols = [slice(hh * hp, (hh + 1) * hp) for hh in range(n_heads)]
    st = [{} for _ in range(n_heads)]

    def scores(hh):
        q = q_ref[:, cols[hh]]
        st[hh]["s_e"] = _dot_nt(q, k_ref[edge, cols[hh]]) + diag_bias
        if n_main:
            st[hh]["s_m"] = _dot_nt(q, k_ref[0:n_main, cols[hh]])

    def softmax(hh):
        s_e = st[hh].pop("s_e")
        m = jnp.max(s_e, axis=-1, keepdims=True)
        if n_main:
            s_m = st[hh].pop("s_m")
            m = jnp.maximum(m, jnp.max(s_m, axis=-1, keepdims=True))
            st[hh]["p_m"] = jnp.exp2(s_m - m).astype(BF16)
        st[hh]["p_e"] = jnp.exp2(s_e - m).astype(BF16)

    def values(hh):
        acc = _dot(st[hh].pop("p_e"), v_ref[edge, cols[hh]])
        if n_main:
            acc = acc + _dot(st[hh].pop("p_m"), v_ref[0:n_main, cols[hh]])
        st[hh]["out"] = acc * _row_recip(acc[:, MLA_V:MLA_V + 1])

    scores(0)
    for hh in range(n_heads):
        if hh + 1 < n_heads:
            scores(hh + 1)
        softmax(hh)
        if hh:
            values(hh - 1)
    values(n_heads - 1)
    outs = [st[hh].pop("out") for hh in range(n_heads)]
    lane = lax.broadcasted_iota(jnp.int32, (1, hp), 1)
    for pair in range(n_heads // 2):
        both = jnp.where(lane < MLA_V, outs[2 * pair], pltpu.roll(outs[2 * pair + 1], MLA_V, 1))
        o_ref[:, pair * hp:(pair + 1) * hp] = both.astype(BF16)


def _mla_attn_kernel(q_ref, k_ref, v_ref, o_ref):
    tq = q_ref.shape[0]
    qi = pl.program_id(2)
    for j in range(k_ref.shape[0] // tq):
        pl.when(qi == j)(functools.partial(_mla_attn_body, q_ref, k_ref, v_ref, o_ref, j * tq))


def _mla_attn(q, k, v):
    b, s, _ = q.shape
    tq = min(MLA_TQ, s)
    qw = MLA_HEADS_PER_STEP * HEAD_PAD
    ow = MLA_HEADS_PER_STEP * MLA_V
    return pl.pallas_call(
        _mla_attn_kernel,
        out_shape=jax.ShapeDtypeStruct((b, s, MLA_HEADS * MLA_V), BF16),
        grid=(b, MLA_HEADS // MLA_HEADS_PER_STEP, s // tq),
        in_specs=[
            pl.BlockSpec((None, tq, qw), lambda i, h, j: (i, j, h)),
            pl.BlockSpec((None, s, qw), lambda i, h, j: (i, 0, h)),
            pl.BlockSpec((None, s, qw), lambda i, h, j: (i, 0, h)),
        ],
        out_specs=pl.BlockSpec((None, tq, ow), lambda i, h, j: (i, j, h)),
        compiler_params=_cparams(("arbitrary", "arbitrary", "arbitrary")),
        name="mla_attn",
    )(q, k, v)


def _nsa_proj_kernel(x_ref, mod_ref, g_ref, win_ref, gb_ref, qt_ref, kvc_ref, ks_ref, vst_ref,
                     kw_ref, vw_ref, gt_ref, *, q_scale):
    tm = x_ref.shape[0]
    hp, dk = HEAD_PAD, NSA_DK
    pad = hp - dk
    h = _modulate(x_ref[...], g_ref[...], mod_ref, 3).astype(BF16)
    z = _dot(h, win_ref[...])
    qd = NSA_HEADS * dk
    kw = NSA_KV_W
    zeros = jnp.zeros((tm, pad), BF16)
    zero_rows = jnp.zeros((pad, tm), BF16)
    for pair in range(NSA_HEADS // 2):
        qt = (z[:, pair * hp:(pair + 1) * hp] * q_scale).T.astype(BF16)
        for half in range(2):
            hd = 2 * pair + half
            qt_ref[hd * hp:hd * hp + dk, :] = qt[half * dk:(half + 1) * dk, :]
            qt_ref[hd * hp + dk:(hd + 1) * hp, :] = zero_rows
    kvc_ref[...] = z[:, qd:qd + kw].astype(BF16)
    pos = pl.program_id(1) * tm + lax.broadcasted_iota(jnp.int32, (tm, 1), 0)
    lane = lax.broadcasted_iota(jnp.int32, (1, pad), 1)
    blk_onehot = jnp.where(lax.shift_right_logical(pos, int(np.log2(SLC_BLOCK))) == lane, 1.0, 0.0)
    blk_onehot = blk_onehot.astype(BF16)
    ones_col = jnp.broadcast_to(jnp.where(lane == 0, 1.0, 0.0), (tm, pad)).astype(BF16)
    kvs = z[:, qd + kw:qd + 2 * kw].astype(BF16)
    kvw = z[:, qd + 2 * kw:qd + 3 * kw].astype(BF16)
    for g in range(NSA_KV_HEADS):
        lo, mid, hi = g * hp, g * hp + dk, (g + 1) * hp
        kc, vc = slice(g * dk, (g + 1) * dk), slice((NSA_KV_HEADS + g) * dk, (NSA_KV_HEADS + g + 1) * dk)
        ks_ref[:, lo:mid] = kvs[:, kc]
        ks_ref[:, mid:hi] = blk_onehot
        kw_ref[:, lo:mid] = kvw[:, kc]
        kw_ref[:, mid:hi] = zeros
        vw_ref[:, lo:mid] = kvw[:, vc]
        vw_ref[:, mid:hi] = ones_col
    v_off = qd + kw + NSA_KV_HEADS * dk
    vt = z[:, v_off:v_off + NSA_KV_HEADS * dk].T.astype(BF16)
    ones_row = jnp.where(lax.broadcasted_iota(jnp.int32, (pad, 1), 0) == 0, 1.0, 0.0)
    ones_row = jnp.broadcast_to(ones_row, (pad, tm)).astype(BF16)
    for g in range(NSA_KV_HEADS):
        vst_ref[g * hp:g * hp + dk, :] = vt[g * dk:(g + 1) * dk, :]
        vst_ref[g * hp + dk:(g + 1) * hp, :] = ones_row
    gt_ref[...] = _sigmoid(z[:, qd + 3 * kw:] + gb_ref[...]).T


def _nsa_proj(x, mod_l, g, w_in, gate_b):
    b, s, d = x.shape
    tm = min(ROW_TILE, s)
    assert s // SLC_BLOCK <= HEAD_PAD - NSA_DK
    qw = NSA_HEADS * HEAD_PAD
    kvw = NSA_KV_HEADS * HEAD_PAD
    ng = gate_b.shape[-1]
    row = lambda width: pl.BlockSpec((None, tm, width), lambda i, j: (i, j, 0))
    col = lambda height: pl.BlockSpec((None, height, tm), lambda i, j: (i, 0, j))
    sds = lambda width, dt=BF16: jax.ShapeDtypeStruct((b, s, width), dt)
    sds_t = lambda height, dt=BF16: jax.ShapeDtypeStruct((b, height, s), dt)
    return pl.pallas_call(
        functools.partial(_nsa_proj_kernel, q_scale=NSA_DK ** -0.5 * LOG2E),
        out_shape=(sds_t(qw), sds(NSA_KV_W), sds(kvw), sds_t(kvw), sds(kvw), sds(kvw), sds_t(ng, F32)),
        grid=(b, s // tm),
        in_specs=[
            row(d),
            pl.BlockSpec((None, N_ADA, d), lambda i, j: (i, 0, 0)),
            _const_spec((1, d)),
            _const_spec(w_in.shape),
            _const_spec((1, ng)),
        ],
        out_specs=(col(qw), row(NSA_KV_W), row(kvw), col(kvw), row(kvw), row(kvw), col(ng)),
        compiler_params=_cparams(("arbitrary", "arbitrary")),
        name="nsa_proj",
    )(x, mod_l, g.reshape(1, d), w_in, gate_b.reshape(1, ng))


def _nsa_cmp_kernel(ch_ref, w1a_ref, w1b_ref, pea_ref, peb_ref, w2_ref, k_ref, vt_ref):
    ch = ch_ref[...]
    n = ch.shape[0]
    first = _dot(ch, w1a_ref[...])
    second = _dot(ch, w1b_ref[...])
    bias = _dot(pea_ref[...], w1a_ref[...]) + _dot(peb_ref[...], w1b_ref[...])
    hid = first + pltpu.roll(second, n - 1, 0) + bias[0:1, :]
    act = (hid * _sigmoid(hid)).astype(BF16)
    kv = _dot(act, w2_ref[...])
    half = kv.shape[1] // 2
    k_ref[...] = kv[:, :half].astype(BF16)
    vt_ref[...] = kv[:, half:].T.astype(BF16)


def _nsa_cmp_weights(cmp_pe, cmp_w1, cmp_w2):
    ncomp = 2 * NSA_KV_HEADS
    half = CMP_BLOCK // 2
    kv_of = np.arange(ncomp) // NSA_KV_HEADS
    eye = jnp.eye(ncomp, dtype=F32)
    w1 = cmp_w1.reshape(2, 2, half, NSA_DK, CMP_HID)[kv_of]
    big1 = jnp.einsum("chldj,ce->hlcdej", w1, eye).reshape(2, half * ncomp * NSA_DK, ncomp * CMP_HID)
    w2 = jnp.pad(cmp_w2[kv_of], ((0, 0), (0, 0), (0, HEAD_PAD - NSA_DK)))
    big2 = jnp.einsum("cjd,ce->cjed", w2, eye).reshape(ncomp * CMP_HID, ncomp * HEAD_PAD)
    pe = cmp_pe.reshape(2, 2, half, NSA_DK)[kv_of]
    pe = jnp.transpose(pe, (1, 2, 0, 3)).reshape(2, 1, half * ncomp * NSA_DK)
    pe = jnp.broadcast_to(pe, (2, 8, half * ncomp * NSA_DK))
    return (big1[0].astype(BF16), big1[1].astype(BF16), pe[0].astype(BF16), pe[1].astype(BF16),
            big2.astype(BF16))


def _nsa_cmp(kvc, w1a, w1b, pea, peb, w2):
    b, s, kw = kvc.shape
    nchunk = s // CMP_STRIDE
    ow = w2.shape[1]
    chunks = kvc.reshape(b, nchunk, CMP_STRIDE * kw)
    half = ow // 2
    return pl.pallas_call(
        _nsa_cmp_kernel,
        out_shape=(jax.ShapeDtypeStruct((b, nchunk, half), BF16),
                   jax.ShapeDtypeStruct((b, half, nchunk), BF16)),
        grid=(b,),
        in_specs=[
            pl.BlockSpec((None, nchunk, CMP_STRIDE * kw), lambda i: (i, 0, 0)),
            _const_spec(w1a.shape), _const_spec(w1b.shape),
            _const_spec(pea.shape), _const_spec(peb.shape), _const_spec(w2.shape),
        ],
        out_specs=(pl.BlockSpec((None, nchunk, half), lambda i: (i, 0, 0)),
                   pl.BlockSpec((None, half, nchunk), lambda i: (i, 0, 0))),
        compiler_params=_cparams(("arbitrary",)),
        name="nsa_cmp",
    )(chunks, w1a, w1b, pea, peb, w2)


def _add_per_head(s, bias, rep):
    tq = bias.shape[1]
    return jnp.concatenate([s[:, r * tq:(r + 1) * tq] + bias for r in range(rep)], axis=1)


def _nsa_attn_kernel(qt_ref, kc_ref, vct_ref, ks_ref, vst_ref, kw_ref, vw_ref, gt_ref, aggt_ref,
                     placet_ref, o_ref, *, tk, n_sel):
    tq = qt_ref.shape[1]
    per = tk // tq
    qi = pl.program_id(1)
    for j in range(ks_ref.shape[0] // tk):
        in_class = (qi >= j * per) & (qi < (j + 1) * per)
        pl.when(in_class)(functools.partial(
            _nsa_attn_body, qt_ref, kc_ref, vct_ref, ks_ref, vst_ref, kw_ref, vw_ref, gt_ref,
            aggt_ref, placet_ref, o_ref, tk=tk, n_sel=n_sel, n_main=j * tk))


def _nsa_attn_body(qt_ref, kc_ref, vct_ref, ks_ref, vst_ref, kw_ref, vw_ref, gt_ref, aggt_ref,
                   placet_ref, o_ref, *, tk, n_sel, n_main):
    tq = qt_ref.shape[1]
    seq = ks_ref.shape[0]
    ncp = kc_ref.shape[0]
    ns = aggt_ref.shape[0]
    hp, dk, rep = HEAD_PAD, NSA_DK, NSA_GROUP
    q0 = pl.program_id(1) * tq
    t = q0 + lax.broadcasted_iota(jnp.int32, (1, tq), 1)
    t_heads = jnp.concatenate([t] * rep, axis=1)

    blk = lax.broadcasted_iota(jnp.int32, (ns, 1), 0)
    cur = lax.shift_right_logical(t, int(np.log2(SLC_BLOCK)))
    forced = (blk == 0) | (blk == cur) | (blk == cur - 1)
    causal_blk = blk * SLC_BLOCK <= t
    cmp_end = lax.broadcasted_iota(jnp.int32, (ncp, 1), 0) * CMP_STRIDE + (CMP_BLOCK - 1)
    cmp_mask = (cmp_end <= t_heads) & (cmp_end < seq)
    win_len = WINDOW + tq
    w_start = pl.multiple_of(jnp.maximum(q0 - WINDOW, 0), tq)
    wpos = w_start + lax.broadcasted_iota(jnp.int32, (win_len, 1), 0)
    win_bias = jnp.where((wpos <= t) & (wpos > t - WINDOW), 0.0, -MASK_BIG)
    last_pos = n_main + lax.broadcasted_iota(jnp.int32, (tk, 1), 0)
    last_bias = jnp.where(last_pos <= t, 0.0, -MASK_BIG)
    pad_row = lax.broadcasted_iota(jnp.int32, (hp, 1), 0)
    sel_offset = jnp.where((pad_row >= dk) & (pad_row < dk + ns), MASK_BIG, 0.0)
    gates = gt_ref[...]

    n_groups = NSA_KV_HEADS
    kv_cols = [slice(g * hp, (g + 1) * hp) for g in range(n_groups)]
    q_heads = [[qt_ref[(g * rep + r) * hp:(g * rep + r + 1) * hp, :] for r in range(rep)]
               for g in range(n_groups)]
    qs = [jnp.concatenate(q_heads[g], axis=1) for g in range(n_groups)]
    st = [{} for _ in range(n_groups)]

    def cmp_scores(g):
        st[g]["s_c"] = jnp.where(cmp_mask, _dot(kc_ref[:, kv_cols[g]], qs[g]), NEG_INF)

    def cmp_softmax(g):
        s = st[g].pop("s_c")
        p = jnp.where(cmp_mask, jnp.exp2(s - jnp.max(s, axis=0, keepdims=True)), 0.0)
        st[g]["p_c"] = p * _row_recip(jnp.sum(p, axis=0, keepdims=True))

    def cmp_values(g):
        p_c = st[g].pop("p_c")
        st[g]["o_c"] = _dot(vct_ref[kv_cols[g], :], p_c.astype(BF16))
        p_sum = p_c[:, 0:tq]
        for r in range(1, rep):
            p_sum = p_sum + p_c[:, r * tq:(r + 1) * tq]
        agg_t = aggt_ref[...]
        hi, mid, lo = _split3(p_sum)
        st[g]["imp"] = _dot(agg_t, hi) + _dot(agg_t, mid) + _dot(agg_t, lo)

    def select(g):
        val = jnp.where(forced, FORCE_SCORE, jnp.where(causal_blk, st[g].pop("imp"), NEG_INF))
        rank = jnp.zeros((ns, tq), F32)
        for i in range(ns):
            other = val[i:i + 1, :]
            beats = (other > val) | ((other == val) & (blk > i))
            rank = rank + jnp.where(beats, 1.0, 0.0)
        sel = jnp.where((rank < n_sel) & causal_blk, 1.0, 0.0).astype(BF16)
        sel_pad = _dot(placet_ref[...], sel) - sel_offset
        st[g]["q_sel"] = jnp.concatenate(
            [(qh.astype(F32) + sel_pad).astype(BF16) for qh in q_heads[g]], axis=1)

    def win_scores(g):
        s = _dot(kw_ref[pl.ds(w_start, win_len), kv_cols[g]], qs[g])
        st[g]["s_w"] = _add_per_head(s, win_bias, rep)

    def win_softmax(g):
        s = st[g].pop("s_w")
        st[g]["p_w"] = jnp.exp2(s - jnp.max(s, axis=0, keepdims=True)).astype(BF16)

    def win_values(g):
        vw_t = vw_ref[pl.ds(w_start, win_len), kv_cols[g]].astype(F32).T.astype(BF16)
        st[g]["acc_w"] = _dot(vw_t, st[g].pop("p_w"))

    edge = slice(n_main, n_main + tk)

    def sel_scores(g):
        q_sel = st[g].pop("q_sel")
        st[g]["s_e"] = _add_per_head(_dot(ks_ref[edge, kv_cols[g]], q_sel), last_bias, rep)
        if n_main:
            st[g]["s_m"] = _dot(ks_ref[0:n_main, kv_cols[g]], q_sel)

    def sel_softmax(g):
        s_e = st[g].pop("s_e")
        m = jnp.max(s_e, axis=0, keepdims=True)
        if n_main:
            s_m = st[g].pop("s_m")
            m = jnp.maximum(m, jnp.max(s_m, axis=0, keepdims=True))
            st[g]["p_m"] = jnp.exp2(s_m - m).astype(BF16)
        st[g]["p_e"] = jnp.exp2(s_e - m).astype(BF16)

    def sel_values(g):
        acc = _dot(vst_ref[kv_cols[g], edge], st[g].pop("p_e"))
        if n_main:
            acc = acc + _dot(vst_ref[kv_cols[g], 0:n_main], st[g].pop("p_m"))
        st[g]["acc_s"] = acc

    def merge(g):
        o_c, acc_s, acc_w = st[g].pop("o_c"), st[g].pop("acc_s"), st[g].pop("acc_w")
        r_s = _row_recip(acc_s[dk:dk + 1, :])
        r_w = _row_recip(acc_w[dk:dk + 1, :])
        out = []
        for r in range(rep):
            cols = slice(r * tq, (r + 1) * tq)
            gc = g * 3 * rep + r
            o_h = (gates[gc:gc + 1, :] * o_c[:, cols]
                   + (gates[gc + rep:gc + rep + 1, :] * r_s[:, cols]) * acc_s[:, cols]
                   + (gates[gc + 2 * rep:gc + 2 * rep + 1, :] * r_w[:, cols]) * acc_w[:, cols])
            out.append(o_h[0:dk, :])
        return out

    a, b = 0, n_groups - 1
    for stage, g in ((cmp_scores, a), (cmp_scores, b), (win_scores, a), (cmp_softmax, a),
                     (win_scores, b), (cmp_softmax, b), (cmp_values, a), (win_softmax, a),
                     (cmp_values, b), (select, a), (win_values, a), (win_softmax, b), (select, b),
                     (win_values, b), (sel_scores, a), (sel_scores, b), (sel_softmax, a),
                     (sel_values, a), (sel_softmax, b), (sel_values, b)):
        stage(g)
    head_out = merge(a) + merge(b)
    o_ref[...] = jnp.concatenate(head_out, axis=0).T.astype(BF16)


def _nsa_tables(seq):
    nchunk = seq // CMP_STRIDE
    ns = seq // SLC_BLOCK
    cs = np.arange(nchunk)[:, None] * CMP_STRIDE
    ss = np.arange(ns)[None, :] * SLC_BLOCK
    overlap = np.clip(np.minimum(cs + CMP_BLOCK, ss + SLC_BLOCK) - np.maximum(cs, ss), 0, None)
    agg_t = (overlap.astype(np.float32) / CMP_BLOCK).T
    place_t = np.zeros((HEAD_PAD, ns), np.float32)
    place_t[NSA_DK + np.arange(ns), np.arange(ns)] = MASK_BIG
    return jnp.asarray(agg_t, BF16), jnp.asarray(place_t, BF16)


def _nsa_attn(qt, kc, vct, ks, vst, kw, vw, gt):
    b, _, s = qt.shape
    tq = min(NSA_TQ, s)
    tk = min(NSA_TK, s)
    assert s >= WINDOW + tq and s % tk == 0 and WINDOW % tq == 0
    agg_t, place_t = _nsa_tables(s)
    whole = lambda a: pl.BlockSpec((None,) + a.shape[1:], lambda i, j: (i, 0, 0))
    col = lambda a: pl.BlockSpec((None, a.shape[1], tq), lambda i, j: (i, 0, j))
    ow = NSA_HEADS * NSA_DK
    return pl.pallas_call(
        functools.partial(_nsa_attn_kernel, tk=tk, n_sel=min(N_SEL, s // SLC_BLOCK)),
        out_shape=jax.ShapeDtypeStruct((b, s, ow), BF16),
        grid=(b, s // tq),
        in_specs=[
            col(qt), whole(kc), whole(vct), whole(ks), whole(vst), whole(kw), whole(vw), col(gt),
            _const_spec(agg_t.shape), _const_spec(place_t.shape),
        ],
        out_specs=pl.BlockSpec((None, tq, ow), lambda i, j: (i, j, 0)),
        compiler_params=_cparams(("arbitrary", "arbitrary")),
        name="nsa_attn",
    )(qt, kc, vct, ks, vst, kw, vw, gt, agg_t, place_t)


def _nsa_gate_layout(w_in, gate_b):
    qd = NSA_HEADS * NSA_DK
    o = qd + 3 * NSA_KV_W
    perm = np.arange(3 * NSA_HEADS).reshape(NSA_KV_HEADS, NSA_GROUP, 3).transpose(0, 2, 1).reshape(-1)
    extra = HEAD_PAD - perm.size
    w_in = jnp.concatenate([w_in[:, :o], jnp.pad(w_in[:, o:][:, perm], ((0, 0), (0, extra)))], axis=1)
    return w_in, jnp.pad(gate_b[perm], (0, extra))


def kernel(x, c, positions, ada_w, ada_b, norm_g, final_g, ff_w13, ff_w2, hy_w_in, hy_conv_w,
           hy_q_norm, hy_kv_norm, hy_w_uq, hy_w_ukv, hy_w_out, nsa_w_in, nsa_cmp_pe, nsa_cmp_w1,
           nsa_cmp_w2, nsa_gate_b, nsa_w_out):
    depth = ada_w.shape[0]
    mod = _ada_mod(c, ada_w, ada_b)
    ropes = _rope_tables(positions)
    w13 = ff_w13.astype(BF16)
    w2 = ff_w2.astype(BF16)
    for l in range(depth):
        m = l // 2
        x = _ffn(x, mod[l], norm_g[l, 0], w13, w2, l, 0, final_g, k0=0, final=False)
        if l % 2 == 0:
            w_in = hy_w_in[m].astype(BF16)
            w_in = jnp.pad(w_in, ((0, 0), (0, -w_in.shape[1] % HEAD_PAD)))
            y_conv, q, k, v = _hy_proj(x, mod[l], norm_g[l, 1], w_in,
                                       hy_conv_w[m], hy_q_norm[m], hy_kv_norm[m],
                                       _hy_weights(hy_w_uq[m], hy_w_ukv[m]), ropes)
            y_att = _mla_attn(q, k, v)
            w_out = hy_w_out[m].astype(BF16)
            parts, w_parts = [y_conv, y_att], [w_out[:CONV_WIDTH], w_out[CONV_WIDTH:]]
        else:
            w_in, gate_b = _nsa_gate_layout(nsa_w_in[m], nsa_gate_b[m])
            qt, kvc, ks, vst, kw, vw, gt = _nsa_proj(x, mod[l], norm_g[l, 1], w_in.astype(BF16), gate_b)
            kc, vct = _nsa_cmp(kvc, *_nsa_cmp_weights(nsa_cmp_pe[m], nsa_cmp_w1[m], nsa_cmp_w2[m]))
            o = _nsa_attn(qt, kc, vct, ks, vst, kw, vw, gt)
            parts, w_parts = [o], [nsa_w_out[m].astype(BF16)]
        x = _ffn(x, mod[l], norm_g[l, 2], w13, w2, l, 1, final_g, k0=6, final=(l == depth - 1),
                 parts=parts, w_parts=w_parts)
    return x
```

```python
import functools

import jax
import jax.numpy as jnp
import numpy as np
from jax import lax
from jax.experimental import pallas as pl
from jax.experimental.pallas import tpu as pltpu

F32 = jnp.float32
BF16 = jnp.bfloat16

N_ADA = 9
EPS = 1e-6
NEG_INF = -1e30
CONV_WIDTH = 512
CONV_TAPS = 3
MLA_HEADS = 8
MLA_NOPE = 64
MLA_ROPE = 32
MLA_V = 64
Q_LORA = 256
KV_LORA = 128
ROPE_THETA = 10000.0
NSA_HEADS = 16
NSA_KV_HEADS = 2
NSA_GROUP = NSA_HEADS // NSA_KV_HEADS
NSA_DK = 64
CMP_BLOCK = 32
CMP_STRIDE = 16
CMP_HID = 128
SLC_BLOCK = 64
N_SEL = 8
WINDOW = 512
FORCE_SCORE = 1e4
NSA_KV_W = 2 * NSA_KV_HEADS * NSA_DK

HEAD_PAD = 128
VMEM_LIMIT = 56 * 1024 * 1024
ROW_TILE = 512
PROJ_SUBTILES = 4
FFN_SUBTILES = 1
MLA_TQ = 512
MLA_HEADS_PER_STEP = 4
NSA_TQ = 128
NSA_TK = 512

LOG2E = float(np.log2(np.e))
MASK_BIG = float(2.0 ** 100)


def _cparams(sem):
    return pltpu.CompilerParams(dimension_semantics=sem, vmem_limit_bytes=VMEM_LIMIT)


def _const_spec(shape):
    nd = len(shape)
    return pl.BlockSpec(shape, lambda *_: (0,) * nd, pipeline_mode=pl.Buffered(1))


def _sigmoid(v):
    return 1.0 / (1.0 + jnp.exp(-v))


def _rms(v, g):
    return v * lax.rsqrt(jnp.mean(v * v, axis=-1, keepdims=True) + EPS) * g


def _modulate(x, g, mod_ref, k0):
    shift = mod_ref[k0:k0 + 1, :]
    scale = mod_ref[k0 + 1:k0 + 2, :]
    return _rms(x, g) * (1.0 + scale) + shift


def _dot(a, b):
    return jnp.dot(a, b, preferred_element_type=F32)


def _dot_nt(a, b):
    return lax.dot_general(a, b, (((1,), (1,)), ((), ())), preferred_element_type=F32)


def _split3(a):
    hi = a.astype(BF16)
    r1 = a - hi.astype(F32)
    mid = r1.astype(BF16)
    lo = (r1 - mid.astype(F32)).astype(BF16)
    return hi, mid, lo


def _row_recip(v):
    return 1.0 / jnp.maximum(v, 1e-20)


def _ada_kernel(c_ref, w_ref, b_ref, o_ref):
    c = c_ref[...]
    ca = (c * _sigmoid(c)).astype(BF16)
    o_ref[...] = _dot(ca, w_ref[...].astype(BF16)) + b_ref[...]


def _ada_mod(c, ada_w, ada_b):
    depth, d, n = ada_w.shape
    b = c.shape[0]
    tn = n // 8
    out = pl.pallas_call(
        _ada_kernel,
        out_shape=jax.ShapeDtypeStruct((depth, b, n), F32),
        grid=(depth, n // tn),
        in_specs=[
            pl.BlockSpec((b, d), lambda l, j: (0, 0)),
            pl.BlockSpec((None, d, tn), lambda l, j: (l, 0, j)),
            pl.BlockSpec((None, 1, tn), lambda l, j: (l, 0, j)),
        ],
        out_specs=pl.BlockSpec((None, b, tn), lambda l, j: (l, 0, j)),
        compiler_params=_cparams(("arbitrary", "arbitrary")),
        name="ada_mod",
    )(c, ada_w, ada_b.reshape(depth, 1, n))
    return out.reshape(depth, b, N_ADA, d)


def _ffn_kernel(*refs, k0, d_ff, final, n_parts):
    x_ref, mod_ref, g_ref = refs[:3]
    part_refs = refs[3:3 + n_parts]
    wout_refs = refs[3 + n_parts:3 + 2 * n_parts]
    w13_ref, w2_ref, fg_ref, o_ref = refs[3 + 2 * n_parts:]
    sub = x_ref.shape[0] // FFN_SUBTILES
    rows = [slice(p * sub, (p + 1) * sub) for p in range(FFN_SUBTILES)]
    st = [{} for _ in range(FFN_SUBTILES)]

    def pre(p):
        x = x_ref[rows[p], :]
        if n_parts:
            y = _dot(part_refs[0][rows[p], :], wout_refs[0][...])
            for p_ref, w_ref in zip(part_refs[1:], wout_refs[1:]):
                y = y + _dot(p_ref[rows[p], :], w_ref[...])
            x = x + mod_ref[5:6, :] * y
        st[p]["x"] = x
        st[p]["h"] = _modulate(x, g_ref[...], mod_ref, k0).astype(BF16)

    def up(p):
        st[p]["ab"] = _dot(st[p].pop("h"), w13_ref[...])

    def act(p):
        ab = st[p].pop("ab")
        a = ab[:, :d_ff]
        b = ab[:, d_ff:]
        st[p]["u"] = (a * _sigmoid(a) * b).astype(BF16)

    def down(p):
        st[p]["y"] = _dot(st[p].pop("u"), w2_ref[...])

    def post(p):
        out = st[p].pop("x") + (0.5 * mod_ref[k0 + 2:k0 + 3, :]) * st[p].pop("y")
        if final:
            out = _rms(out, fg_ref[...])
        o_ref[rows[p], :] = out

    pre(0)
    up(0)
    for p in range(FFN_SUBTILES):
        if p + 1 < FFN_SUBTILES:
            pre(p + 1)
            up(p + 1)
        act(p)
        down(p)
        if p:
            post(p - 1)
    post(FFN_SUBTILES - 1)


def _ffn(x, mod_l, g, w13, w2, layer, which, final_g, *, k0, final, parts=(), w_parts=()):
    b, s, d = x.shape
    d_ff = w2.shape[2]
    tm = min(ROW_TILE, s)
    kern = functools.partial(_ffn_kernel, k0=k0, d_ff=d_ff, final=final, n_parts=len(parts))
    pick = lambda *_: (layer, which, 0, 0)
    row = lambda width: pl.BlockSpec((None, tm, width), lambda i, j: (i, j, 0))
    return pl.pallas_call(
        kern,
        out_shape=jax.ShapeDtypeStruct(x.shape, F32),
        grid=(b, s // tm),
        in_specs=[
            row(d),
            pl.BlockSpec((None, N_ADA, d), lambda i, j: (i, 0, 0)),
            _const_spec((1, d)),
        ] + [row(p.shape[-1]) for p in parts] + [_const_spec(w.shape) for w in w_parts] + [
            pl.BlockSpec((None, None, d, 2 * d_ff), pick, pipeline_mode=pl.Buffered(1)),
            pl.BlockSpec((None, None, d_ff, d), pick, pipeline_mode=pl.Buffered(1)),
            _const_spec((1, d)),
        ],
        out_specs=row(d),
        compiler_params=_cparams(("arbitrary", "arbitrary")),
        name="ffn",
    )(x, mod_l, g.reshape(1, d), *parts, *w_parts, w13, w2, final_g.reshape(1, d))


def _rope_swap(t):
    width = t.shape[-1]
    half = MLA_ROPE // 2
    lane = lax.broadcasted_iota(jnp.int32, (1, width), 1) & (HEAD_PAD - 1)
    return jnp.where(lane < MLA_NOPE + half, pltpu.roll(t, width - half, 1), pltpu.roll(t, half, 1))


def _hy_proj_kernel(x_ref, mod_ref, g_ref, win_ref, convw_ref, qn_ref, kvn_ref, wq_ref,
                    wk_ref, wv_ref, rc_ref, rs_ref,
                    yconv_ref, q_ref, k_ref, v_ref, carry_ref, *, q_scale):
    tm = x_ref.shape[0]
    cw = CONV_WIDTH

    @pl.when(pl.program_id(1) == 0)
    def _():
        carry_ref[...] = jnp.zeros_like(carry_ref)

    sub = tm // PROJ_SUBTILES
    rows = [slice(p * sub, (p + 1) * sub) for p in range(PROJ_SUBTILES)]
    st = [{} for _ in range(PROJ_SUBTILES)]
    o = 3 * cw
    n_rep = q_ref.shape[-1] // HEAD_PAD

    def project_in(p):
        h = _modulate(x_ref[rows[p], :], g_ref[...], mod_ref, 3).astype(BF16)
        st[p]["z"] = _dot(h, win_ref[...])

    def conv(p):
        z = st[p]["z"]
        u, gate_c, gate_b = z[:, :cw], z[:, cw:2 * cw], z[:, 2 * cw:3 * cw]
        prev = st[p - 1]["tail"] if p else carry_ref[...]
        v = gate_c * u
        row = lax.broadcasted_iota(jnp.int32, (sub, 1), 0)
        v1 = jnp.where(row == 0, prev[7:8, :], pltpu.roll(v, 1, 0))
        v2 = jnp.where(row == 0, prev[6:7, :], jnp.where(row == 1, prev[7:8, :], pltpu.roll(v, 2, 0)))
        w = convw_ref[...]
        yconv_ref[rows[p], :] = (gate_b * (w[0:1, :] * v2 + w[1:2, :] * v1 + w[2:3, :] * v)).astype(BF16)
        st[p]["tail"] = v[sub - 8:, :]

    def latent_norms(p):
        z = st[p]["z"]
        st[p]["qn"] = _rms(z[:, o:o + Q_LORA], qn_ref[...]).astype(BF16)
        st[p]["kvn"] = _rms(z[:, o + Q_LORA:o + Q_LORA + KV_LORA], kvn_ref[...]).astype(BF16)
        st[p]["kr"] = z[:, o + Q_LORA + KV_LORA:]

    def project_heads(p):
        st[p]["q"] = _dot(st[p].pop("qn"), wq_ref[...])
        kvn = st[p].pop("kvn")
        st[p]["k"] = _dot(kvn, wk_ref[...])
        st[p]["v"] = _dot(kvn, wv_ref[...])

    def rotary_store(p):
        rc = rc_ref[rows[p], :]
        rs = rs_ref[rows[p], :]
        q = st[p].pop("q")
        q = q * jnp.concatenate([rc] * n_rep, axis=1) + _rope_swap(q) * jnp.concatenate([rs] * n_rep, axis=1)
        kr = pltpu.roll(st[p].pop("kr"), MLA_NOPE, 1)
        kr = kr * rc + _rope_swap(kr) * rs
        k = st[p].pop("k") + jnp.concatenate([kr] * n_rep, axis=1)
        q_ref[rows[p], :] = (q * q_scale).astype(BF16)
        k_ref[rows[p], :] = k.astype(BF16)
        lane = lax.broadcasted_iota(jnp.int32, (1, v_ref.shape[-1]), 1)
        ones_col = jnp.where((lane & (HEAD_PAD - 1)) == MLA_V, 1.0, 0.0)
        v_ref[rows[p], :] = (st[p].pop("v") + ones_col).astype(BF16)

    project_in(0)
    for p in range(PROJ_SUBTILES):
        if p + 1 < PROJ_SUBTILES:
            project_in(p + 1)
        latent_norms(p)
        conv(p)
        project_heads(p)
        if p:
            rotary_store(p - 1)
    rotary_store(PROJ_SUBTILES - 1)
    carry_ref[...] = st[PROJ_SUBTILES - 1]["tail"]


def _hy_weights(w_uq, w_ukv):
    hp = HEAD_PAD
    pad_heads = lambda w: jnp.pad(w, ((0, 0), (0, 0), (0, hp - w.shape[-1]))).reshape(w.shape[0], -1)
    wq = pad_heads(w_uq.reshape(Q_LORA, MLA_HEADS, MLA_NOPE + MLA_ROPE))
    wkv = w_ukv.reshape(KV_LORA, MLA_HEADS, MLA_NOPE + MLA_V)
    wk = pad_heads(wkv[..., :MLA_NOPE])
    wv = pad_heads(wkv[..., MLA_NOPE:])
    return wq.astype(BF16), wk.astype(BF16), wv.astype(BF16)


def _rope_tables(positions):
    half = MLA_ROPE // 2
    inv = ROPE_THETA ** (-jnp.arange(half, dtype=F32) / half)
    ang = positions.astype(F32)[..., None] * inv
    cos, sin = jnp.cos(ang), jnp.sin(ang)
    lead = positions.shape
    ones = jnp.ones(lead + (MLA_NOPE,), F32)
    ztail = jnp.zeros(lead + (HEAD_PAD - MLA_NOPE - MLA_ROPE,), F32)
    zhead = jnp.zeros(lead + (MLA_NOPE,), F32)
    rc = jnp.concatenate([ones, cos, cos, ztail], axis=-1)
    rs = jnp.concatenate([zhead, -sin, sin, ztail], axis=-1)
    return rc, rs


def _hy_proj(x, mod_l, g, w_in, conv_w, q_norm, kv_norm, hy_w, ropes):
    b, s, d = x.shape
    tm = min(ROW_TILE, s)
    hp = HEAD_PAD
    qw = MLA_HEADS * hp
    row = lambda width: pl.BlockSpec((None, tm, width), lambda i, j: (i, j, 0))
    q_scale = (MLA_NOPE + MLA_ROPE) ** -0.5 * LOG2E
    return pl.pallas_call(
        functools.partial(_hy_proj_kernel, q_scale=q_scale),
        out_shape=(
            jax.ShapeDtypeStruct((b, s, CONV_WIDTH), BF16),
            jax.ShapeDtypeStruct((b, s, qw), BF16),
            jax.ShapeDtypeStruct((b, s, qw), BF16),
            jax.ShapeDtypeStruct((b, s, qw), BF16),
        ),
        grid=(b, s // tm),
        in_specs=[
            row(d),
            pl.BlockSpec((None, N_ADA, d), lambda i, j: (i, 0, 0)),
            _const_spec((1, d)),
            _const_spec(w_in.shape),
            _const_spec(conv_w.shape),
            _const_spec((1, Q_LORA)),
            _const_spec((1, KV_LORA)),
        ] + [_const_spec(w.shape) for w in hy_w] + [row(hp), row(hp)],
        out_specs=(row(CONV_WIDTH), row(qw), row(qw), row(qw)),
        scratch_shapes=[pltpu.VMEM((8, CONV_WIDTH), F32)],
        compiler_params=_cparams(("arbitrary", "arbitrary")),
        name="hy_proj",
    )(x, mod_l, g.reshape(1, d), w_in, conv_w, q_norm.reshape(1, -1), kv_norm.reshape(1, -1),
      *hy_w, *ropes)


def _mla_attn_body(q_ref, k_ref, v_ref, o_ref, n_main):
    tq = q_ref.shape[0]
    hp = HEAD_PAD
    n_heads = q_ref.shape[1] // hp
    rel = (lax.broadcasted_iota(jnp.int32, (1, tq), 1)
           <= lax.broadcasted_iota(jnp.int32, (tq, 1), 0))
    diag_bias = jnp.where(rel, 0.0, -MASK_BIG)
    edge = slice(n_main, n_main + tq)
    cols = [slice(hh * hp, (hh + 1) * hp) for hh in range(n_heads)]
    st = [{} for _ in range(n_heads)]

    def scores(hh):
        q = q_ref[:, cols[hh]]
        st[hh]["s_e"] = _dot_nt(q, k_ref[edge, cols[hh]]) + diag_bias
        if n_main:
            st[hh]["s_m"] = _dot_nt(q, k_ref[0:n_main, cols[hh]])

    def softmax(hh):
        s_e = st[hh].pop("s_e")
        m = jnp.max(s_e, axis=-1, keepdims=True)
        if n_main:
            s_m = st[hh].pop("s_m")
            m = jnp.maximum(m, jnp.max(s_m, axis=-1, keepdims=True))
            st[hh]["p_m"] = jnp.exp2(s_m - m).astype(BF16)
        st[hh]["p_e"] = jnp.exp2(s_e - m).astype(BF16)

    def values(hh):
        acc = _dot(st[hh].pop("p_e"), v_ref[edge, cols[hh]])
        if n_main:
            acc = acc + _dot(st[hh].pop("p_m"), v_ref[0:n_main, cols[hh]])
        st[hh]["out"] = acc * _row_recip(acc[:, MLA_V:MLA_V + 1])

    scores(0)
    for hh in range(n_heads):
        if hh + 1 < n_heads:
            scores(hh + 1)
        softmax(hh)
        if hh:
            values(hh - 1)
    values(n_heads - 1)
    outs = [st[hh].pop("out") for hh in range(n_heads)]
    lane = lax.broadcasted_iota(jnp.int32, (1, hp), 1)
    for pair in range(n_heads // 2):
        both = jnp.where(lane < MLA_V, outs[2 * pair], pltpu.roll(outs[2 * pair + 1], MLA_V, 1))
        o_ref[:, pair * hp:(pair + 1) * hp] = both.astype(BF16)


def _mla_attn_kernel(q_ref, k_ref, v_ref, o_ref):
    tq = q_ref.shape[0]
    qi = pl.program_id(2)
    for j in range(k_ref.shape[0] // tq):
        pl.when(qi == j)(functools.partial(_mla_attn_body, q_ref, k_ref, v_ref, o_ref, j * tq))


def _mla_attn(q, k, v):
    b, s, _ = q.shape
    tq = min(MLA_TQ, s)
    qw = MLA_HEADS_PER_STEP * HEAD_PAD
    ow = MLA_HEADS_PER_STEP * MLA_V
    return pl.pallas_call(
        _mla_attn_kernel,
        out_shape=jax.ShapeDtypeStruct((b, s, MLA_HEADS * MLA_V), BF16),
        grid=(b, MLA_HEADS // MLA_HEADS_PER_STEP, s // tq),
        in_specs=[
            pl.BlockSpec((None, tq, qw), lambda i, h, j: (i, j, h)),
            pl.BlockSpec((None, s, qw), lambda i, h, j: (i, 0, h)),
            pl.BlockSpec((None, s, qw), lambda i, h, j: (i, 0, h)),
        ],
        out_specs=pl.BlockSpec((None, tq, ow), lambda i, h, j: (i, j, h)),
        compiler_params=_cparams(("arbitrary", "arbitrary", "arbitrary")),
        name="mla_attn",
    )(q, k, v)


def _nsa_proj_kernel(x_ref, mod_ref, g_ref, win_ref, gb_ref, qt_ref, kvc_ref, ks_ref, vst_ref,
                     kw_ref, vw_ref, gt_ref, *, q_scale):
    tm = x_ref.shape[0]
    hp, dk = HEAD_PAD, NSA_DK
    pad = hp - dk
    h = _modulate(x_ref[...], g_ref[...], mod_ref, 3).astype(BF16)
    z = _dot(h, win_ref[...])
    qd = NSA_HEADS * dk
    kw = NSA_KV_W
    zeros = jnp.zeros((tm, pad), BF16)
    zero_rows = jnp.zeros((pad, tm), BF16)
    for pair in range(NSA_HEADS // 2):
        qt = (z[:, pair * hp:(pair + 1) * hp] * q_scale).T.astype(BF16)
        for half in range(2):
            hd = 2 * pair + half
            qt_ref[hd * hp:hd * hp + dk, :] = qt[half * dk:(half + 1) * dk, :]
            qt_ref[hd * hp + dk:(hd + 1) * hp, :] = zero_rows
    kvc_ref[...] = z[:, qd:qd + kw].astype(BF16)
    pos = pl.program_id(1) * tm + lax.broadcasted_iota(jnp.int32, (tm, 1), 0)
    lane = lax.broadcasted_iota(jnp.int32, (1, pad), 1)
    blk_onehot = jnp.where(lax.shift_right_logical(pos, int(np.log2(SLC_BLOCK))) == lane, 1.0, 0.0)
    blk_onehot = blk_onehot.astype(BF16)
    ones_col = jnp.broadcast_to(jnp.where(lane == 0, 1.0, 0.0), (tm, pad)).astype(BF16)
    kvs = z[:, qd + kw:qd + 2 * kw].astype(BF16)
    kvw = z[:, qd + 2 * kw:qd + 3 * kw].astype(BF16)
    for g in range(NSA_KV_HEADS):
        lo, mid, hi = g * hp, g * hp + dk, (g + 1) * hp
        kc, vc = slice(g * dk, (g + 1) * dk), slice((NSA_KV_HEADS + g) * dk, (NSA_KV_HEADS + g + 1) * dk)
        ks_ref[:, lo:mid] = kvs[:, kc]
        ks_ref[:, mid:hi] = blk_onehot
        kw_ref[:, lo:mid] = kvw[:, kc]
        kw_ref[:, mid:hi] = zeros
        vw_ref[:, lo:mid] = kvw[:, vc]
        vw_ref[:, mid:hi] = ones_col
    v_off = qd + kw + NSA_KV_HEADS * dk
    vt = z[:, v_off:v_off + NSA_KV_HEADS * dk].T.astype(BF16)
    ones_row = jnp.where(lax.broadcasted_iota(jnp.int32, (pad, 1), 0) == 0, 1.0, 0.0)
    ones_row = jnp.broadcast_to(ones_row, (pad, tm)).astype(BF16)
    for g in range(NSA_KV_HEADS):
        vst_ref[g * hp:g * hp + dk, :] = vt[g * dk:(g + 1) * dk, :]
        vst_ref[g * hp + dk:(g + 1) * hp, :] = ones_row
    gt_ref[...] = _sigmoid(z[:, qd + 3 * kw:] + gb_ref[...]).T


def _nsa_proj(x, mod_l, g, w_in, gate_b):
    b, s, d = x.shape
    tm = min(ROW_TILE, s)
    assert s // SLC_BLOCK <= HEAD_PAD - NSA_DK
    qw = NSA_HEADS * HEAD_PAD
    kvw = NSA_KV_HEADS * HEAD_PAD
    ng = gate_b.shape[-1]
    row = lambda width: pl.BlockSpec((None, tm, width), lambda i, j: (i, j, 0))
    col = lambda height: pl.BlockSpec((None, height, tm), lambda i, j: (i, 0, j))
    sds = lambda width, dt=BF16: jax.ShapeDtypeStruct((b, s, width), dt)
    sds_t = lambda height, dt=BF16: jax.ShapeDtypeStruct((b, height, s), dt)
    return pl.pallas_call(
        functools.partial(_nsa_proj_kernel, q_scale=NSA_DK ** -0.5 * LOG2E),
        out_shape=(sds_t(qw), sds(NSA_KV_W), sds(kvw), sds_t(kvw), sds(kvw), sds(kvw), sds_t(ng, F32)),
        grid=(b, s // tm),
        in_specs=[
            row(d),
            pl.BlockSpec((None, N_ADA, d), lambda i, j: (i, 0, 0)),
            _const_spec((1, d)),
            _const_spec(w_in.shape),
            _const_spec((1, ng)),
        ],
        out_specs=(col(qw), row(NSA_KV_W), row(kvw), col(kvw), row(kvw), row(kvw), col(ng)),
        compiler_params=_cparams(("arbitrary", "arbitrary")),
        name="nsa_proj",
    )(x, mod_l, g.reshape(1, d), w_in, gate_b.reshape(1, ng))


def _nsa_cmp_kernel(ch_ref, w1a_ref, w1b_ref, pea_ref, peb_ref, w2_ref, k_ref, vt_ref):
    ch = ch_ref[...]
    n = ch.shape[0]
    first = _dot(ch, w1a_ref[...])
    second = _dot(ch, w1b_ref[...])
    bias = _dot(pea_ref[...], w1a_ref[...]) + _dot(peb_ref[...], w1b_ref[...])
    hid = first + pltpu.roll(second, n - 1, 0) + bias[0:1, :]
    act = (hid * _sigmoid(hid)).astype(BF16)
    kv = _dot(act, w2_ref[...])
    half = kv.shape[1] // 2
    k_ref[...] = kv[:, :half].astype(BF16)
    vt_ref[...] = kv[:, half:].T.astype(BF16)


def _nsa_cmp_weights(cmp_pe, cmp_w1, cmp_w2):
    ncomp = 2 * NSA_KV_HEADS
    half = CMP_BLOCK // 2
    kv_of = np.arange(ncomp) // NSA_KV_HEADS
    eye = jnp.eye(ncomp, dtype=F32)
    w1 = cmp_w1.reshape(2, 2, half, NSA_DK, CMP_HID)[kv_of]
    big1 = jnp.einsum("chldj,ce->hlcdej", w1, eye).reshape(2, half * ncomp * NSA_DK, ncomp * CMP_HID)
    w2 = jnp.pad(cmp_w2[kv_of], ((0, 0), (0, 0), (0, HEAD_PAD - NSA_DK)))
    big2 = jnp.einsum("cjd,ce->cjed", w2, eye).reshape(ncomp * CMP_HID, ncomp * HEAD_PAD)
    pe = cmp_pe.reshape(2, 2, half, NSA_DK)[kv_of]
    pe = jnp.transpose(pe, (1, 2, 0, 3)).reshape(2, 1, half * ncomp * NSA_DK)
    pe = jnp.broadcast_to(pe, (2, 8, half * ncomp * NSA_DK))
    return (big1[0].astype(BF16), big1[1].astype(BF16), pe[0].astype(BF16), pe[1].astype(BF16),
            big2.astype(BF16))


def _nsa_cmp(kvc, w1a, w1b, pea, peb, w2):
    b, s, kw = kvc.shape
    nchunk = s // CMP_STRIDE
    ow = w2.shape[1]
    chunks = kvc.reshape(b, nchunk, CMP_STRIDE * kw)
    half = ow // 2
    return pl.pallas_call(
        _nsa_cmp_kernel,
        out_shape=(jax.ShapeDtypeStruct((b, nchunk, half), BF16),
                   jax.ShapeDtypeStruct((b, half, nchunk), BF16)),
        grid=(b,),
        in_specs=[
            pl.BlockSpec((None, nchunk, CMP_STRIDE * kw), lambda i: (i, 0, 0)),
            _const_spec(w1a.shape), _const_spec(w1b.shape),
            _const_spec(pea.shape), _const_spec(peb.shape), _const_spec(w2.shape),
        ],
        out_specs=(pl.BlockSpec((None, nchunk, half), lambda i: (i, 0, 0)),
                   pl.BlockSpec((None, half, nchunk), lambda i: (i, 0, 0))),
        compiler_params=_cparams(("arbitrary",)),
        name="nsa_cmp",
    )(chunks, w1a, w1b, pea, peb, w2)


def _add_per_head(s, bias, rep):
    tq = bias.shape[1]
    return jnp.concatenate([s[:, r * tq:(r + 1) * tq] + bias for r in range(rep)], axis=1)


def _nsa_attn_kernel(qt_ref, kc_ref, vct_ref, ks_ref, vst_ref, kw_ref, vw_ref, gt_ref, aggt_ref,
                     placet_ref, o_ref, *, tk, n_sel):
    tq = qt_ref.shape[1]
    per = tk // tq
    qi = pl.program_id(1)
    for j in range(ks_ref.shape[0] // tk):
        in_class = (qi >= j * per) & (qi < (j + 1) * per)
        pl.when(in_class)(functools.partial(
            _nsa_attn_body, qt_ref, kc_ref, vct_ref, ks_ref, vst_ref, kw_ref, vw_ref, gt_ref,
            aggt_ref, placet_ref, o_ref, tk=tk, n_sel=n_sel, n_main=j * tk))


def _nsa_attn_body(qt_ref, kc_ref, vct_ref, ks_ref, vst_ref, kw_ref, vw_ref, gt_ref, aggt_ref,
                   placet_ref, o_ref, *, tk, n_sel, n_main):
    tq = qt_ref.shape[1]
    seq = ks_ref.shape[0]
    ncp = kc_ref.shape[0]
    ns = aggt_ref.shape[0]
    hp, dk, rep = HEAD_PAD, NSA_DK, NSA_GROUP
    q0 = pl.program_id(1) * tq
    t = q0 + lax.broadcasted_iota(jnp.int32, (1, tq), 1)
    t_heads = jnp.concatenate([t] * rep, axis=1)

    blk = lax.broadcasted_iota(jnp.int32, (ns, 1), 0)
    cur = lax.shift_right_logical(t, int(np.log2(SLC_BLOCK)))
    forced = (blk == 0) | (blk == cur) | (blk == cur - 1)
    causal_blk = blk * SLC_BLOCK <= t
    cmp_end = lax.broadcasted_iota(jnp.int32, (ncp, 1), 0) * CMP_STRIDE + (CMP_BLOCK - 1)
    cmp_mask = (cmp_end <= t_heads) & (cmp_end < seq)
    win_len = WINDOW + tq
    w_start = pl.multiple_of(jnp.maximum(q0 - WINDOW, 0), tq)
    wpos = w_start + lax.broadcasted_iota(jnp.int32, (win_len, 1), 0)
    win_bias = jnp.where((wpos <= t) & (wpos > t - WINDOW), 0.0, -MASK_BIG)
    last_pos = n_main + lax.broadcasted_iota(jnp.int32, (tk, 1), 0)
    last_bias = jnp.where(last_pos <= t, 0.0, -MASK_BIG)
    pad_row = lax.broadcasted_iota(jnp.int32, (hp, 1), 0)
    sel_offset = jnp.where((pad_row >= dk) & (pad_row < dk + ns), MASK_BIG, 0.0)
    gates = gt_ref[...]

    n_groups = NSA_KV_HEADS
    kv_cols = [slice(g * hp, (g + 1) * hp) for g in range(n_groups)]
    q_heads = [[qt_ref[(g * rep + r) * hp:(g * rep + r + 1) * hp, :] for r in range(rep)]
               for g in range(n_groups)]
    qs = [jnp.concatenate(q_heads[g], axis=1) for g in range(n_groups)]
    st = [{} for _ in range(n_groups)]

    def cmp_scores(g):
        st[g]["s_c"] = jnp.where(cmp_mask, _dot(kc_ref[:, kv_cols[g]], qs[g]), NEG_INF)

    def cmp_softmax(g):
        s = st[g].pop("s_c")
        p = jnp.where(cmp_mask, jnp.exp2(s - jnp.max(s, axis=0, keepdims=True)), 0.0)
        st[g]["p_c"] = p * _row_recip(jnp.sum(p, axis=0, keepdims=True))

    def cmp_values(g):
        p_c = st[g].pop("p_c")
        st[g]["o_c"] = _dot(vct_ref[kv_cols[g], :], p_c.astype(BF16))
        p_sum = p_c[:, 0:tq]
        for r in range(1, rep):
            p_sum = p_sum + p_c[:, r * tq:(r + 1) * tq]
        agg_t = aggt_ref[...]
        hi, mid, lo = _split3(p_sum)
        st[g]["imp"] = _dot(agg_t, hi) + _dot(agg_t, mid) + _dot(agg_t, lo)

    def select(g):
        val = jnp.where(forced, FORCE_SCORE, jnp.where(causal_blk, st[g].pop("imp"), NEG_INF))
        rank = jnp.zeros((ns, tq), F32)
        for i in range(ns):
            other = val[i:i + 1, :]
            beats = (other > val) | ((other == val) & (blk > i))
            rank = rank + jnp.where(beats, 1.0, 0.0)
        sel = jnp.where((rank < n_sel) & causal_blk, 1.0, 0.0).astype(BF16)
        sel_pad = _dot(placet_ref[...], sel) - sel_offset
        st[g]["q_sel"] = jnp.concatenate(
            [(qh.astype(F32) + sel_pad).astype(BF16) for qh in q_heads[g]], axis=1)

    def win_scores(g):
        s = _dot(kw_ref[pl.ds(w_start, win_len), kv_cols[g]], qs[g])
        st[g]["s_w"] = _add_per_head(s, win_bias, rep)

    def win_softmax(g):
        s = st[g].pop("s_w")
        st[g]["p_w"] = jnp.exp2(s - jnp.max(s, axis=0, keepdims=True)).astype(BF16)

    def win_values(g):
        vw_t = vw_ref[pl.ds(w_start, win_len), kv_cols[g]].astype(F32).T.astype(BF16)
        st[g]["acc_w"] = _dot(vw_t, st[g].pop("p_w"))

    n_sel_chunks = n_main // tk + 1

    def sel_scores(g, c):
        keys = slice(c * tk, (c + 1) * tk)
        s = _dot(ks_ref[keys, kv_cols[g]], st[g]["q_sel"])
        st[g]["s_s", c] = _add_per_head(s, last_bias, rep) if c == n_sel_chunks - 1 else s

    def sel_softmax(g, c):
        s = st[g].pop(("s_s", c))
        m = jnp.max(s, axis=0, keepdims=True)
        st[g]["m_s", c] = m
        st[g]["p_s", c] = jnp.exp2(s - m).astype(BF16)

    def sel_values(g, c):
        keys = slice(c * tk, (c + 1) * tk)
        st[g]["acc_s", c] = _dot(vst_ref[kv_cols[g], keys], st[g].pop(("p_s", c)))

    def merge(g):
        o_c, acc_w = st[g].pop("o_c"), st[g].pop("acc_w")
        m_all = st[g]["m_s", 0]
        for c in range(1, n_sel_chunks):
            m_all = jnp.maximum(m_all, st[g]["m_s", c])
        acc_s = None
        for c in range(n_sel_chunks):
            part = st[g].pop(("acc_s", c))
            if n_sel_chunks > 1:
                part = part * jnp.exp2(st[g].pop(("m_s", c)) - m_all)
            acc_s = part if acc_s is None else acc_s + part
        r_s = _row_recip(acc_s[dk:dk + 1, :])
        r_w = _row_recip(acc_w[dk:dk + 1, :])
        out = []
        for r in range(rep):
            cols = slice(r * tq, (r + 1) * tq)
            gc = g * 3 * rep + r
            o_h = (gates[gc:gc + 1, :] * o_c[:, cols]
                   + (gates[gc + rep:gc + rep + 1, :] * r_s[:, cols]) * acc_s[:, cols]
                   + (gates[gc + 2 * rep:gc + 2 * rep + 1, :] * r_w[:, cols]) * acc_w[:, cols])
            out.append(o_h[0:dk, :])
        return out

    groups = range(n_groups)
    units = [(cmp_scores, cmp_softmax, cmp_values, (g,)) for g in groups]
    units += [(win_scores, win_softmax, win_values, (g,)) for g in groups]
    units += [(sel_scores, sel_softmax, sel_values, (g, c))
              for c in range(n_sel_chunks) for g in groups]
    for i in range(len(units) + 2):
        if i < len(units):
            units[i][0](*units[i][3])
        if 0 <= i - 1 < len(units):
            units[i - 1][1](*units[i - 1][3])
        if 0 <= i - 2 < len(units):
            units[i - 2][2](*units[i - 2][3])
            if units[i - 2][2] is cmp_values:
                select(*units[i - 2][3])
    head_out = [piece for g in groups for piece in merge(g)]
    o_ref[...] = jnp.concatenate(head_out, axis=0).T.astype(BF16)


def _nsa_tables(seq):
    nchunk = seq // CMP_STRIDE
    ns = seq // SLC_BLOCK
    cs = np.arange(nchunk)[:, None] * CMP_STRIDE
    ss = np.arange(ns)[None, :] * SLC_BLOCK
    overlap = np.clip(np.minimum(cs + CMP_BLOCK, ss + SLC_BLOCK) - np.maximum(cs, ss), 0, None)
    agg_t = (overlap.astype(np.float32) / CMP_BLOCK).T
    place_t = np.zeros((HEAD_PAD, ns), np.float32)
    place_t[NSA_DK + np.arange(ns), np.arange(ns)] = MASK_BIG
    return jnp.asarray(agg_t, BF16), jnp.asarray(place_t, BF16)


def _nsa_attn(qt, kc, vct, ks, vst, kw, vw, gt):
    b, _, s = qt.shape
    tq = min(NSA_TQ, s)
    tk = min(NSA_TK, s)
    assert s >= WINDOW + tq and s % tk == 0 and WINDOW % tq == 0
    agg_t, place_t = _nsa_tables(s)
    whole = lambda a: pl.BlockSpec((None,) + a.shape[1:], lambda i, j: (i, 0, 0))
    col = lambda a: pl.BlockSpec((None, a.shape[1], tq), lambda i, j: (i, 0, j))
    ow = NSA_HEADS * NSA_DK
    return pl.pallas_call(
        functools.partial(_nsa_attn_kernel, tk=tk, n_sel=min(N_SEL, s // SLC_BLOCK)),
        out_shape=jax.ShapeDtypeStruct((b, s, ow), BF16),
        grid=(b, s // tq),
        in_specs=[
            col(qt), whole(kc), whole(vct), whole(ks), whole(vst), whole(kw), whole(vw), col(gt),
            _const_spec(agg_t.shape), _const_spec(place_t.shape),
        ],
        out_specs=pl.BlockSpec((None, tq, ow), lambda i, j: (i, j, 0)),
        compiler_params=_cparams(("arbitrary", "arbitrary")),
        name="nsa_attn",
    )(qt, kc, vct, ks, vst, kw, vw, gt, agg_t, place_t)


def _nsa_gate_layout(w_in, gate_b):
    qd = NSA_HEADS * NSA_DK
    o = qd + 3 * NSA_KV_W
    perm = np.arange(3 * NSA_HEADS).reshape(NSA_KV_HEADS, NSA_GROUP, 3).transpose(0, 2, 1).reshape(-1)
    extra = HEAD_PAD - perm.size
    w_in = jnp.concatenate([w_in[:, :o], jnp.pad(w_in[:, o:][:, perm], ((0, 0), (0, extra)))], axis=1)
    return w_in, jnp.pad(gate_b[perm], (0, extra))


def kernel(x, c, positions, ada_w, ada_b, norm_g, final_g, ff_w13, ff_w2, hy_w_in, hy_conv_w,
           hy_q_norm, hy_kv_norm, hy_w_uq, hy_w_ukv, hy_w_out, nsa_w_in, nsa_cmp_pe, nsa_cmp_w1,
           nsa_cmp_w2, nsa_gate_b, nsa_w_out):
    depth = ada_w.shape[0]
    mod = _ada_mod(c, ada_w, ada_b)
    ropes = _rope_tables(positions)
    w13 = ff_w13.astype(BF16)
    w2 = ff_w2.astype(BF16)
    for l in range(depth):
        m = l // 2
        x = _ffn(x, mod[l], norm_g[l, 0], w13, w2, l, 0, final_g, k0=0, final=False)
        if l % 2 == 0:
            w_in = hy_w_in[m].astype(BF16)
            w_in = jnp.pad(w_in, ((0, 0), (0, -w_in.shape[1] % HEAD_PAD)))
            y_conv, q, k, v = _hy_proj(x, mod[l], norm_g[l, 1], w_in,
                                       hy_conv_w[m], hy_q_norm[m], hy_kv_norm[m],
                                       _hy_weights(hy_w_uq[m], hy_w_ukv[m]), ropes)
            y_att = _mla_attn(q, k, v)
            w_out = hy_w_out[m].astype(BF16)
            parts, w_parts = [y_conv, y_att], [w_out[:CONV_WIDTH], w_out[CONV_WIDTH:]]
        else:
            w_in, gate_b = _nsa_gate_layout(nsa_w_in[m], nsa_gate_b[m])
            qt, kvc, ks, vst, kw, vw, gt = _nsa_proj(x, mod[l], norm_g[l, 1], w_in.astype(BF16), gate_b)
            kc, vct = _nsa_cmp(kvc, *_nsa_cmp_weights(nsa_cmp_pe[m], nsa_cmp_w1[m], nsa_cmp_w2[m]))
            o = _nsa_attn(qt, kc, vct, ks, vst, kw, vw, gt)
            parts, w_parts = [o], [nsa_w_out[m].astype(BF16)]
        x = _ffn(x, mod[l], norm_g[l, 2], w13, w2, l, 1, final_g, k0=6, final=(l == depth - 1),
                 parts=parts, w_parts=w_parts)
    return x
```

```python
import functools

import jax
import jax.numpy as jnp
import numpy as np
from jax import lax
from jax.experimental import pallas as pl
from jax.experimental.pallas import tpu as pltpu

F32 = jnp.float32
BF16 = jnp.bfloat16

N_ADA = 9
EPS = 1e-6
NEG_INF = -1e30
CONV_WIDTH = 512
CONV_TAPS = 3
MLA_HEADS = 8
MLA_NOPE = 64
MLA_ROPE = 32
MLA_V = 64
Q_LORA = 256
KV_LORA = 128
ROPE_THETA = 10000.0
NSA_HEADS = 16
NSA_KV_HEADS = 2
NSA_GROUP = NSA_HEADS // NSA_KV_HEADS
NSA_DK = 64
CMP_BLOCK = 32
CMP_STRIDE = 16
CMP_HID = 128
SLC_BLOCK = 64
N_SEL = 8
WINDOW = 512
FORCE_SCORE = 1e4
NSA_KV_W = 2 * NSA_KV_HEADS * NSA_DK

HEAD_PAD = 128
VMEM_LIMIT = 56 * 1024 * 1024
ROW_TILE = 512
HY_ROW_TILE = 1024
PROJ_SUBTILES = 4
NSA_PROJ_SUBTILES = 1
FFN_SUBTILES = 1
MLA_TQ = 512
MLA_HEADS_PER_STEP = 4
NSA_TQ = 128
NSA_TK = 512

LOG2E = float(np.log2(np.e))
MASK_BIG = float(2.0 ** 100)


def _cparams(sem):
    return pltpu.CompilerParams(dimension_semantics=sem, vmem_limit_bytes=VMEM_LIMIT)


def _const_spec(shape):
    nd = len(shape)
    return pl.BlockSpec(shape, lambda *_: (0,) * nd, pipeline_mode=pl.Buffered(1))


def _sigmoid(v):
    return 1.0 / (1.0 + jnp.exp(-v))


def _rms(v, g):
    return v * lax.rsqrt(jnp.mean(v * v, axis=-1, keepdims=True) + EPS) * g


def _modulate(x, g, mod_ref, k0):
    shift = mod_ref[k0:k0 + 1, :]
    scale = mod_ref[k0 + 1:k0 + 2, :]
    return _rms(x, g) * (1.0 + scale) + shift


def _dot(a, b):
    return jnp.dot(a, b, preferred_element_type=F32)


def _dot_nt(a, b):
    return lax.dot_general(a, b, (((1,), (1,)), ((), ())), preferred_element_type=F32)


def _split3(a):
    hi = a.astype(BF16)
    r1 = a - hi.astype(F32)
    mid = r1.astype(BF16)
    lo = (r1 - mid.astype(F32)).astype(BF16)
    return hi, mid, lo


def _row_recip(v):
    return 1.0 / jnp.maximum(v, 1e-20)


def _ada_kernel(c_ref, w_ref, b_ref, o_ref):
    c = c_ref[...]
    ca = (c * _sigmoid(c)).astype(BF16)
    o_ref[...] = _dot(ca, w_ref[...].astype(BF16)) + b_ref[...]


def _ada_mod(c, ada_w, ada_b):
    depth, d, n = ada_w.shape
    b = c.shape[0]
    tn = n // 8
    out = pl.pallas_call(
        _ada_kernel,
        out_shape=jax.ShapeDtypeStruct((depth, b, n), F32),
        grid=(depth, n // tn),
        in_specs=[
            pl.BlockSpec((b, d), lambda l, j: (0, 0)),
            pl.BlockSpec((None, d, tn), lambda l, j: (l, 0, j)),
            pl.BlockSpec((None, 1, tn), lambda l, j: (l, 0, j)),
        ],
        out_specs=pl.BlockSpec((None, b, tn), lambda l, j: (l, 0, j)),
        compiler_params=_cparams(("arbitrary", "arbitrary")),
        name="ada_mod",
    )(c, ada_w, ada_b.reshape(depth, 1, n))
    return out.reshape(depth, b, N_ADA, d)


def _ffn_kernel(*refs, k0, d_ff, final, n_parts):
    x_ref, mod_ref, g_ref = refs[:3]
    part_refs = refs[3:3 + n_parts]
    wout_refs = refs[3 + n_parts:3 + 2 * n_parts]
    w13_ref, w2_ref, fg_ref, o_ref = refs[3 + 2 * n_parts:]
    sub = x_ref.shape[0] // FFN_SUBTILES
    rows = [slice(p * sub, (p + 1) * sub) for p in range(FFN_SUBTILES)]
    st = [{} for _ in range(FFN_SUBTILES)]

    def pre(p):
        x = x_ref[rows[p], :]
        if n_parts:
            y = _dot(part_refs[0][rows[p], :], wout_refs[0][...])
            for p_ref, w_ref in zip(part_refs[1:], wout_refs[1:]):
                y = y + _dot(p_ref[rows[p], :], w_ref[...])
            x = x + mod_ref[5:6, :] * y
        st[p]["x"] = x
        st[p]["h"] = _modulate(x, g_ref[...], mod_ref, k0).astype(BF16)

    def up(p):
        st[p]["ab"] = _dot(st[p].pop("h"), w13_ref[...])

    def act(p):
        ab = st[p].pop("ab")
        a = ab[:, :d_ff]
        b = ab[:, d_ff:]
        st[p]["u"] = (a * _sigmoid(a) * b).astype(BF16)

    def down(p):
        st[p]["y"] = _dot(st[p].pop("u"), w2_ref[...])

    def post(p):
        out = st[p].pop("x") + (0.5 * mod_ref[k0 + 2:k0 + 3, :]) * st[p].pop("y")
        if final:
            out = _rms(out, fg_ref[...])
        o_ref[rows[p], :] = out

    pre(0)
    up(0)
    for p in range(FFN_SUBTILES):
        if p + 1 < FFN_SUBTILES:
            pre(p + 1)
            up(p + 1)
        act(p)
        down(p)
        if p:
            post(p - 1)
    post(FFN_SUBTILES - 1)


def _ffn(x, mod_l, g, w13, w2, layer, which, final_g, *, k0, final, parts=(), w_parts=()):
    b, s, d = x.shape
    d_ff = w2.shape[2]
    tm = min(ROW_TILE, s)
    kern = functools.partial(_ffn_kernel, k0=k0, d_ff=d_ff, final=final, n_parts=len(parts))
    pick = lambda *_: (layer, which, 0, 0)
    row = lambda width: pl.BlockSpec((None, tm, width), lambda i, j: (i, j, 0))
    return pl.pallas_call(
        kern,
        out_shape=jax.ShapeDtypeStruct(x.shape, F32),
        grid=(b, s // tm),
        in_specs=[
            row(d),
            pl.BlockSpec((None, N_ADA, d), lambda i, j: (i, 0, 0)),
            _const_spec((1, d)),
        ] + [row(p.shape[-1]) for p in parts] + [_const_spec(w.shape) for w in w_parts] + [
            pl.BlockSpec((None, None, d, 2 * d_ff), pick, pipeline_mode=pl.Buffered(1)),
            pl.BlockSpec((None, None, d_ff, d), pick, pipeline_mode=pl.Buffered(1)),
            _const_spec((1, d)),
        ],
        out_specs=row(d),
        compiler_params=_cparams(("arbitrary", "arbitrary")),
        name="ffn",
    )(x, mod_l, g.reshape(1, d), *parts, *w_parts, w13, w2, final_g.reshape(1, d))


def _rope_swap(t):
    width = t.shape[-1]
    half = MLA_ROPE // 2
    lane = lax.broadcasted_iota(jnp.int32, (1, width), 1) & (HEAD_PAD - 1)
    return jnp.where(lane < MLA_NOPE + half, pltpu.roll(t, width - half, 1), pltpu.roll(t, half, 1))


def _hy_proj_kernel(x_ref, mod_ref, g_ref, win_ref, convw_ref, qn_ref, kvn_ref, wq_ref,
                    wk_ref, wv_ref, rc_ref, rs_ref,
                    yconv_ref, q_ref, k_ref, v_ref, carry_ref, *, q_scale):
    tm = x_ref.shape[0]
    cw = CONV_WIDTH

    @pl.when(pl.program_id(1) == 0)
    def _():
        carry_ref[...] = jnp.zeros_like(carry_ref)

    sub = tm // PROJ_SUBTILES
    rows = [slice(p * sub, (p + 1) * sub) for p in range(PROJ_SUBTILES)]
    st = [{} for _ in range(PROJ_SUBTILES)]
    o = 3 * cw
    n_rep = q_ref.shape[-1] // HEAD_PAD

    def project_in(p):
        h = _modulate(x_ref[rows[p], :], g_ref[...], mod_ref, 3).astype(BF16)
        st[p]["z"] = _dot(h, win_ref[...])

    def conv(p):
        z = st[p]["z"]
        u, gate_c, gate_b = z[:, :cw], z[:, cw:2 * cw], z[:, 2 * cw:3 * cw]
        prev = st[p - 1]["tail"] if p else carry_ref[...]
        v = gate_c * u
        row = lax.broadcasted_iota(jnp.int32, (sub, 1), 0)
        v1 = jnp.where(row == 0, prev[7:8, :], pltpu.roll(v, 1, 0))
        v2 = jnp.where(row == 0, prev[6:7, :], jnp.where(row == 1, prev[7:8, :], pltpu.roll(v, 2, 0)))
        w = convw_ref[...]
        yconv_ref[rows[p], :] = (gate_b * (w[0:1, :] * v2 + w[1:2, :] * v1 + w[2:3, :] * v)).astype(BF16)
        st[p]["tail"] = v[sub - 8:, :]

    def latent_norms(p):
        z = st[p]["z"]
        st[p]["qn"] = _rms(z[:, o:o + Q_LORA], qn_ref[...]).astype(BF16)
        st[p]["kvn"] = _rms(z[:, o + Q_LORA:o + Q_LORA + KV_LORA], kvn_ref[...]).astype(BF16)
        st[p]["kr"] = z[:, o + Q_LORA + KV_LORA:]

    def project_heads(p):
        st[p]["q"] = _dot(st[p].pop("qn"), wq_ref[...])
        kvn = st[p].pop("kvn")
        st[p]["k"] = _dot(kvn, wk_ref[...])
        st[p]["v"] = _dot(kvn, wv_ref[...])

    def rotary_store(p):
        rc = rc_ref[rows[p], :]
        rs = rs_ref[rows[p], :]
        q = st[p].pop("q")
        q = q * jnp.concatenate([rc] * n_rep, axis=1) + _rope_swap(q) * jnp.concatenate([rs] * n_rep, axis=1)
        kr = pltpu.roll(st[p].pop("kr"), MLA_NOPE, 1)
        kr = kr * rc + _rope_swap(kr) * rs
        k = st[p].pop("k") + jnp.concatenate([kr] * n_rep, axis=1)
        q_ref[rows[p], :] = (q * q_scale).astype(BF16)
        k_ref[rows[p], :] = k.astype(BF16)
        lane = lax.broadcasted_iota(jnp.int32, (1, v_ref.shape[-1]), 1)
        ones_col = jnp.where((lane & (HEAD_PAD - 1)) == MLA_V, 1.0, 0.0)
        v_ref[rows[p], :] = (st[p].pop("v") + ones_col).astype(BF16)

    project_in(0)
    for p in range(PROJ_SUBTILES):
        if p + 1 < PROJ_SUBTILES:
            project_in(p + 1)
        latent_norms(p)
        conv(p)
        project_heads(p)
        if p:
            rotary_store(p - 1)
    rotary_store(PROJ_SUBTILES - 1)
    carry_ref[...] = st[PROJ_SUBTILES - 1]["tail"]


def _hy_weights(w_uq, w_ukv):
    hp = HEAD_PAD
    pad_heads = lambda w: jnp.pad(w, ((0, 0), (0, 0), (0, hp - w.shape[-1]))).reshape(w.shape[0], -1)
    wq = pad_heads(w_uq.reshape(Q_LORA, MLA_HEADS, MLA_NOPE + MLA_ROPE))
    wkv = w_ukv.reshape(KV_LORA, MLA_HEADS, MLA_NOPE + MLA_V)
    wk = pad_heads(wkv[..., :MLA_NOPE])
    wv = pad_heads(wkv[..., MLA_NOPE:])
    return wq.astype(BF16), wk.astype(BF16), wv.astype(BF16)


def _rope_tables(positions):
    half = MLA_ROPE // 2
    inv = ROPE_THETA ** (-jnp.arange(half, dtype=F32) / half)
    ang = positions.astype(F32)[..., None] * inv
    cos, sin = jnp.cos(ang), jnp.sin(ang)
    lead = positions.shape
    ones = jnp.ones(lead + (MLA_NOPE,), F32)
    ztail = jnp.zeros(lead + (HEAD_PAD - MLA_NOPE - MLA_ROPE,), F32)
    zhead = jnp.zeros(lead + (MLA_NOPE,), F32)
    rc = jnp.concatenate([ones, cos, cos, ztail], axis=-1)
    rs = jnp.concatenate([zhead, -sin, sin, ztail], axis=-1)
    return rc, rs


def _hy_proj(x, mod_l, g, w_in, conv_w, q_norm, kv_norm, hy_w, ropes):
    b, s, d = x.shape
    tm = min(HY_ROW_TILE, s)
    hp = HEAD_PAD
    qw = MLA_HEADS * hp
    row = lambda width: pl.BlockSpec((None, tm, width), lambda i, j: (i, j, 0))
    q_scale = (MLA_NOPE + MLA_ROPE) ** -0.5 * LOG2E
    return pl.pallas_call(
        functools.partial(_hy_proj_kernel, q_scale=q_scale),
        out_shape=(
            jax.ShapeDtypeStruct((b, s, CONV_WIDTH), BF16),
            jax.ShapeDtypeStruct((b, s, qw), BF16),
            jax.ShapeDtypeStruct((b, s, qw), BF16),
            jax.ShapeDtypeStruct((b, s, qw), BF16),
        ),
        grid=(b, s // tm),
        in_specs=[
            row(d),
            pl.BlockSpec((None, N_ADA, d), lambda i, j: (i, 0, 0)),
            _const_spec((1, d)),
            _const_spec(w_in.shape),
            _const_spec(conv_w.shape),
            _const_spec((1, Q_LORA)),
            _const_spec((1, KV_LORA)),
        ] + [_const_spec(w.shape) for w in hy_w] + [row(hp), row(hp)],
        out_specs=(row(CONV_WIDTH), row(qw), row(qw), row(qw)),
        scratch_shapes=[pltpu.VMEM((8, CONV_WIDTH), F32)],
        compiler_params=_cparams(("arbitrary", "arbitrary")),
        name="hy_proj",
    )(x, mod_l, g.reshape(1, d), w_in, conv_w, q_norm.reshape(1, -1), kv_norm.reshape(1, -1),
      *hy_w, *ropes)


def _mla_attn_body(q_ref, k_ref, v_ref, o_ref, n_main):
    tq = q_ref.shape[0]
    hp = HEAD_PAD
    n_heads = q_ref.shape[1] // hp
    rel = (lax.broadcasted_iota(jnp.int32, (1, tq), 1)
           <= lax.broadcasted_iota(jnp.int32, (tq, 1), 0))
    diag_bias = jnp.where(rel, 0.0, -MASK_BIG)
    edge = slice(n_main, n_main + tq)
    cols = [slice(hh * hp, (hh + 1) * hp) for hh in range(n_heads)]
    st = [{} for _ in range(n_heads)]

    def scores(hh):
        q = q_ref[:, cols[hh]]
        st[hh]["s_e"] = _dot_nt(q, k_ref[edge, cols[hh]]) + diag_bias
        if n_main:
            st[hh]["s_m"] = _dot_nt(q, k_ref[0:n_main, cols[hh]])

    def softmax(hh):
        s_e = st[hh].pop("s_e")
        m = jnp.max(s_e, axis=-1, keepdims=True)
        if n_main:
            s_m = st[hh].pop("s_m")
            m = jnp.maximum(m, jnp.max(s_m, axis=-1, keepdims=True))
            st[hh]["p_m"] = jnp.exp2(s_m - m).astype(BF16)
        st[hh]["p_e"] = jnp.exp2(s_e - m).astype(BF16)

    def values(hh):
        acc = _dot(st[hh].pop("p_e"), v_ref[edge, cols[hh]])
        if n_main:
            acc = acc + _dot(st[hh].pop("p_m"), v_ref[0:n_main, cols[hh]])
        st[hh]["out"] = acc * _row_recip(acc[:, MLA_V:MLA_V + 1])

    scores(0)
    for hh in range(n_heads):
        if hh + 1 < n_heads:
            scores(hh + 1)
        softmax(hh)
        if hh:
            values(hh - 1)
    values(n_heads - 1)
    outs = [st[hh].pop("out") for hh in range(n_heads)]
    lane = lax.broadcasted_iota(jnp.int32, (1, hp), 1)
    for pair in range(n_heads // 2):
        both = jnp.where(lane < MLA_V, outs[2 * pair], pltpu.roll(outs[2 * pair + 1], MLA_V, 1))
        o_ref[:, pair * hp:(pair + 1) * hp] = both.astype(BF16)


def _mla_attn_kernel(q_ref, k_ref, v_ref, o_ref):
    tq = q_ref.shape[0]
    qi = pl.program_id(2)
    for j in range(k_ref.shape[0] // tq):
        pl.when(qi == j)(functools.partial(_mla_attn_body, q_ref, k_ref, v_ref, o_ref, j * tq))


def _mla_attn(q, k, v):
    b, s, _ = q.shape
    tq = min(MLA_TQ, s)
    qw = MLA_HEADS_PER_STEP * HEAD_PAD
    ow = MLA_HEADS_PER_STEP * MLA_V
    return pl.pallas_call(
        _mla_attn_kernel,
        out_shape=jax.ShapeDtypeStruct((b, s, MLA_HEADS * MLA_V), BF16),
        grid=(b, MLA_HEADS // MLA_HEADS_PER_STEP, s // tq),
        in_specs=[
            pl.BlockSpec((None, tq, qw), lambda i, h, j: (i, j, h)),
            pl.BlockSpec((None, s, qw), lambda i, h, j: (i, 0, h)),
            pl.BlockSpec((None, s, qw), lambda i, h, j: (i, 0, h)),
        ],
        out_specs=pl.BlockSpec((None, tq, ow), lambda i, h, j: (i, j, h)),
        compiler_params=_cparams(("arbitrary", "arbitrary", "arbitrary")),
        name="mla_attn",
    )(q, k, v)


def _nsa_proj_kernel(x_ref, mod_ref, g_ref, win_ref, gb_ref, qt_ref, kvc_ref, ks_ref, vst_ref,
                     kw_ref, vw_ref, gt_ref, *, q_scale):
    tm = x_ref.shape[0]
    hp, dk = HEAD_PAD, NSA_DK
    pad = hp - dk
    qd = NSA_HEADS * dk
    kw = NSA_KV_W
    sub = tm // NSA_PROJ_SUBTILES
    zs = {}

    def project_in(p):
        h = _modulate(x_ref[p * sub:(p + 1) * sub, :], g_ref[...], mod_ref, 3).astype(BF16)
        zs[p] = _dot(h, win_ref[...])

    def emit(p):
        z = zs.pop(p)
        at = slice(p * sub, (p + 1) * sub)
        zeros = jnp.zeros((sub, pad), BF16)
        zero_rows = jnp.zeros((pad, sub), BF16)
        for pair in range(NSA_HEADS // 2):
            qt = (z[:, pair * hp:(pair + 1) * hp] * q_scale).T.astype(BF16)
            for half in range(2):
                hd = 2 * pair + half
                qt_ref[hd * hp:hd * hp + dk, at] = qt[half * dk:(half + 1) * dk, :]
                qt_ref[hd * hp + dk:(hd + 1) * hp, at] = zero_rows
        kvc_ref[at, :] = z[:, qd:qd + kw].astype(BF16)
        pos = pl.program_id(1) * tm + p * sub + lax.broadcasted_iota(jnp.int32, (sub, 1), 0)
        lane = lax.broadcasted_iota(jnp.int32, (1, pad), 1)
        blk_onehot = jnp.where(lax.shift_right_logical(pos, int(np.log2(SLC_BLOCK))) == lane, 1.0, 0.0)
        blk_onehot = blk_onehot.astype(BF16)
        ones_col = jnp.broadcast_to(jnp.where(lane == 0, 1.0, 0.0), (sub, pad)).astype(BF16)
        kvs = z[:, qd + kw:qd + 2 * kw].astype(BF16)
        kvw = z[:, qd + 2 * kw:qd + 3 * kw].astype(BF16)
        for g in range(NSA_KV_HEADS):
            lo, mid, hi = g * hp, g * hp + dk, (g + 1) * hp
            kc = slice(g * dk, (g + 1) * dk)
            vc = slice((NSA_KV_HEADS + g) * dk, (NSA_KV_HEADS + g + 1) * dk)
            ks_ref[at, lo:mid] = kvs[:, kc]
            ks_ref[at, mid:hi] = blk_onehot
            kw_ref[at, lo:mid] = kvw[:, kc]
            kw_ref[at, mid:hi] = zeros
            vw_ref[at, lo:mid] = kvw[:, vc]
            vw_ref[at, mid:hi] = ones_col
        v_off = qd + kw + NSA_KV_HEADS * dk
        vt = z[:, v_off:v_off + NSA_KV_HEADS * dk].T.astype(BF16)
        ones_row = jnp.where(lax.broadcasted_iota(jnp.int32, (pad, 1), 0) == 0, 1.0, 0.0)
        ones_row = jnp.broadcast_to(ones_row, (pad, sub)).astype(BF16)
        for g in range(NSA_KV_HEADS):
            vst_ref[g * hp:g * hp + dk, at] = vt[g * dk:(g + 1) * dk, :]
            vst_ref[g * hp + dk:(g + 1) * hp, at] = ones_row
        gt_ref[:, at] = _sigmoid(z[:, qd + 3 * kw:] + gb_ref[...]).T

    project_in(0)
    for p in range(NSA_PROJ_SUBTILES):
        if p + 1 < NSA_PROJ_SUBTILES:
            project_in(p + 1)
        emit(p)


def _nsa_proj(x, mod_l, g, w_in, gate_b):
    b, s, d = x.shape
    tm = min(ROW_TILE, s)
    assert s // SLC_BLOCK <= HEAD_PAD - NSA_DK
    qw = NSA_HEADS * HEAD_PAD
    kvw = NSA_KV_HEADS * HEAD_PAD
    ng = gate_b.shape[-1]
    row = lambda width: pl.BlockSpec((None, tm, width), lambda i, j: (i, j, 0))
    col = lambda height: pl.BlockSpec((None, height, tm), lambda i, j: (i, 0, j))
    sds = lambda width, dt=BF16: jax.ShapeDtypeStruct((b, s, width), dt)
    sds_t = lambda height, dt=BF16: jax.ShapeDtypeStruct((b, height, s), dt)
    return pl.pallas_call(
        functools.partial(_nsa_proj_kernel, q_scale=NSA_DK ** -0.5 * LOG2E),
        out_shape=(sds_t(qw), sds(NSA_KV_W), sds(kvw), sds_t(kvw), sds(kvw), sds(kvw), sds_t(ng, F32)),
        grid=(b, s // tm),
        in_specs=[
            row(d),
            pl.BlockSpec((None, N_ADA, d), lambda i, j: (i, 0, 0)),
            _const_spec((1, d)),
            _const_spec(w_in.shape),
            _const_spec((1, ng)),
        ],
        out_specs=(col(qw), row(NSA_KV_W), row(kvw), col(kvw), row(kvw), row(kvw), col(ng)),
        compiler_params=_cparams(("arbitrary", "arbitrary")),
        name="nsa_proj",
    )(x, mod_l, g.reshape(1, d), w_in, gate_b.reshape(1, ng))


def _nsa_cmp_kernel(ch_ref, w1a_ref, w1b_ref, pea_ref, peb_ref, w2_ref, k_ref, vt_ref):
    ch = ch_ref[...]
    n = ch.shape[0]
    first = _dot(ch, w1a_ref[...])
    second = _dot(ch, w1b_ref[...])
    bias = _dot(pea_ref[...], w1a_ref[...]) + _dot(peb_ref[...], w1b_ref[...])
    hid = first + pltpu.roll(second, n - 1, 0) + bias[0:1, :]
    act = (hid * _sigmoid(hid)).astype(BF16)
    kv = _dot(act, w2_ref[...])
    half = kv.shape[1] // 2
    k_ref[...] = kv[:, :half].astype(BF16)
    vt_ref[...] = kv[:, half:].T.astype(BF16)


def _nsa_cmp_weights(cmp_pe, cmp_w1, cmp_w2):
    ncomp = 2 * NSA_KV_HEADS
    half = CMP_BLOCK // 2
    kv_of = np.arange(ncomp) // NSA_KV_HEADS
    eye = jnp.eye(ncomp, dtype=BF16)
    w1 = cmp_w1.astype(BF16).reshape(2, 2, half, NSA_DK, CMP_HID)[kv_of]
    big1 = jnp.einsum("chldj,ce->hlcdej", w1, eye).reshape(2, half * ncomp * NSA_DK, ncomp * CMP_HID)
    w2 = jnp.pad(cmp_w2.astype(BF16)[kv_of], ((0, 0), (0, 0), (0, HEAD_PAD - NSA_DK)))
    big2 = jnp.einsum("cjd,ce->cjed", w2, eye).reshape(ncomp * CMP_HID, ncomp * HEAD_PAD)
    pe = cmp_pe.astype(BF16).reshape(2, 2, half, NSA_DK)[kv_of]
    pe = jnp.transpose(pe, (1, 2, 0, 3)).reshape(2, 1, half * ncomp * NSA_DK)
    pe = jnp.broadcast_to(pe, (2, 8, half * ncomp * NSA_DK))
    return big1[0], big1[1], pe[0], pe[1], big2


def _nsa_cmp(kvc, w1a, w1b, pea, peb, w2):
    b, s, kw = kvc.shape
    nchunk = s // CMP_STRIDE
    ow = w2.shape[1]
    chunks = kvc.reshape(b, nchunk, CMP_STRIDE * kw)
    half = ow // 2
    return pl.pallas_call(
        _nsa_cmp_kernel,
        out_shape=(jax.ShapeDtypeStruct((b, nchunk, half), BF16),
                   jax.ShapeDtypeStruct((b, half, nchunk), BF16)),
        grid=(b,),
        in_specs=[
            pl.BlockSpec((None, nchunk, CMP_STRIDE * kw), lambda i: (i, 0, 0)),
            _const_spec(w1a.shape), _const_spec(w1b.shape),
            _const_spec(pea.shape), _const_spec(peb.shape), _const_spec(w2.shape),
        ],
        out_specs=(pl.BlockSpec((None, nchunk, half), lambda i: (i, 0, 0)),
                   pl.BlockSpec((None, half, nchunk), lambda i: (i, 0, 0))),
        compiler_params=_cparams(("arbitrary",)),
        name="nsa_cmp",
    )(chunks, w1a, w1b, pea, peb, w2)


def _add_per_head(s, bias, rep):
    tq = bias.shape[1]
    return jnp.concatenate([s[:, r * tq:(r + 1) * tq] + bias for r in range(rep)], axis=1)


def _nsa_attn_kernel(qt_ref, kc_ref, vct_ref, ks_ref, vst_ref, kw_ref, vw_ref, gt_ref, aggt_ref,
                     placet_ref, o_ref, *, tk, n_sel):
    tq = qt_ref.shape[1]
    per = tk // tq
    qi = pl.program_id(1)
    for j in range(ks_ref.shape[0] // tk):
        in_class = (qi >= j * per) & (qi < (j + 1) * per)
        pl.when(in_class)(functools.partial(
            _nsa_attn_body, qt_ref, kc_ref, vct_ref, ks_ref, vst_ref, kw_ref, vw_ref, gt_ref,
            aggt_ref, placet_ref, o_ref, tk=tk, n_sel=n_sel, n_main=j * tk))


def _nsa_attn_body(qt_ref, kc_ref, vct_ref, ks_ref, vst_ref, kw_ref, vw_ref, gt_ref, aggt_ref,
                   placet_ref, o_ref, *, tk, n_sel, n_main):
    tq = qt_ref.shape[1]
    seq = ks_ref.shape[0]
    ncp = kc_ref.shape[0]
    ns = aggt_ref.shape[0]
    hp, dk, rep = HEAD_PAD, NSA_DK, NSA_GROUP
    q0 = pl.program_id(1) * tq
    t = q0 + lax.broadcasted_iota(jnp.int32, (1, tq), 1)
    t_heads = jnp.concatenate([t] * rep, axis=1)

    blk = lax.broadcasted_iota(jnp.int32, (ns, 1), 0)
    cur = lax.shift_right_logical(t, int(np.log2(SLC_BLOCK)))
    forced = (blk == 0) | (blk == cur) | (blk == cur - 1)
    causal_blk = blk * SLC_BLOCK <= t
    cmp_end = lax.broadcasted_iota(jnp.int32, (ncp, 1), 0) * CMP_STRIDE + (CMP_BLOCK - 1)
    cmp_mask = (cmp_end <= t_heads) & (cmp_end < seq)
    win_len = WINDOW + tq
    w_start = pl.multiple_of(jnp.maximum(q0 - WINDOW, 0), tq)
    wpos = w_start + lax.broadcasted_iota(jnp.int32, (win_len, 1), 0)
    win_bias = jnp.where((wpos <= t) & (wpos > t - WINDOW), 0.0, -MASK_BIG)
    last_pos = n_main + lax.broadcasted_iota(jnp.int32, (tk, 1), 0)
    last_bias = jnp.where(last_pos <= t, 0.0, -MASK_BIG)
    pad_row = lax.broadcasted_iota(jnp.int32, (hp, 1), 0)
    sel_offset = jnp.where((pad_row >= dk) & (pad_row < dk + ns), MASK_BIG, 0.0)
    gates = gt_ref[...]

    n_groups = NSA_KV_HEADS
    kv_cols = [slice(g * hp, (g + 1) * hp) for g in range(n_groups)]
    q_heads = [[qt_ref[(g * rep + r) * hp:(g * rep + r + 1) * hp, :] for r in range(rep)]
               for g in range(n_groups)]
    qs = [jnp.concatenate(q_heads[g], axis=1) for g in range(n_groups)]
    st = [{} for _ in range(n_groups)]

    def cmp_scores(g):
        st[g]["s_c"] = jnp.where(cmp_mask, _dot(kc_ref[:, kv_cols[g]], qs[g]), NEG_INF)

    def cmp_softmax(g):
        s = st[g].pop("s_c")
        p = jnp.where(cmp_mask, jnp.exp2(s - jnp.max(s, axis=0, keepdims=True)), 0.0)
        st[g]["p_c"] = p * _row_recip(jnp.sum(p, axis=0, keepdims=True))

    def cmp_values(g):
        p_c = st[g].pop("p_c")
        st[g]["o_c"] = _dot(vct_ref[kv_cols[g], :], p_c.astype(BF16))
        p_sum = p_c[:, 0:tq]
        for r in range(1, rep):
            p_sum = p_sum + p_c[:, r * tq:(r + 1) * tq]
        agg_t = aggt_ref[...]
        hi, mid, lo = _split3(p_sum)
        st[g]["imp"] = _dot(agg_t, hi) + _dot(agg_t, mid) + _dot(agg_t, lo)

    def select(g):
        val = jnp.where(forced, FORCE_SCORE, jnp.where(causal_blk, st[g].pop("imp"), NEG_INF))
        rank = jnp.zeros((ns, tq), F32)
        for i in range(ns):
            other = val[i:i + 1, :]
            beats = (other > val) | ((other == val) & (blk > i))
            rank = rank + jnp.where(beats, 1.0, 0.0)
        sel = jnp.where((rank < n_sel) & causal_blk, 1.0, 0.0).astype(BF16)
        sel_pad = _dot(placet_ref[...], sel) - sel_offset
        st[g]["q_sel"] = jnp.concatenate(
            [(qh.astype(F32) + sel_pad).astype(BF16) for qh in q_heads[g]], axis=1)

    def win_scores(g):
        s = _dot(kw_ref[pl.ds(w_start, win_len), kv_cols[g]], qs[g])
        st[g]["s_w"] = _add_per_head(s, win_bias, rep)

    def win_softmax(g):
        s = st[g].pop("s_w")
        st[g]["p_w"] = jnp.exp2(s - jnp.max(s, axis=0, keepdims=True)).astype(BF16)

    def win_values(g):
        vw_t = vw_ref[pl.ds(w_start, win_len), kv_cols[g]].astype(F32).T.astype(BF16)
        st[g]["acc_w"] = _dot(vw_t, st[g].pop("p_w"))

    n_sel_chunks = n_main // tk + 1

    def sel_scores(g, c):
        keys = slice(c * tk, (c + 1) * tk)
        s = _dot(ks_ref[keys, kv_cols[g]], st[g]["q_sel"])
        st[g]["s_s", c] = _add_per_head(s, last_bias, rep) if c == n_sel_chunks - 1 else s

    def sel_softmax(g, c):
        s = st[g].pop(("s_s", c))
        m = jnp.max(s, axis=0, keepdims=True)
        st[g]["m_s", c] = m
        st[g]["p_s", c] = jnp.exp2(s - m).astype(BF16)

    def sel_values(g, c):
        keys = slice(c * tk, (c + 1) * tk)
        st[g]["acc_s", c] = _dot(vst_ref[kv_cols[g], keys], st[g].pop(("p_s", c)))

    def merge(g):
        o_c, acc_w = st[g].pop("o_c"), st[g].pop("acc_w")
        m_all = st[g]["m_s", 0]
        for c in range(1, n_sel_chunks):
            m_all = jnp.maximum(m_all, st[g]["m_s", c])
        acc_s = None
        for c in range(n_sel_chunks):
            part = st[g].pop(("acc_s", c))
            if n_sel_chunks > 1:
                part = part * jnp.exp2(st[g].pop(("m_s", c)) - m_all)
            acc_s = part if acc_s is None else acc_s + part
        r_s = _row_recip(acc_s[dk:dk + 1, :])
        r_w = _row_recip(acc_w[dk:dk + 1, :])
        out = []
        for r in range(rep):
            cols = slice(r * tq, (r + 1) * tq)
            gc = g * 3 * rep + r
            o_h = (gates[gc:gc + 1, :] * o_c[:, cols]
                   + (gates[gc + rep:gc + rep + 1, :] * r_s[:, cols]) * acc_s[:, cols]
                   + (gates[gc + 2 * rep:gc + 2 * rep + 1, :] * r_w[:, cols]) * acc_w[:, cols])
            out.append(o_h[0:dk, :])
        return out

    groups = range(n_groups)
    units = [(cmp_scores, cmp_softmax, cmp_values, (g,)) for g in groups]
    units += [(win_scores, win_softmax, win_values, (g,)) for g in groups]
    units += [(sel_scores, sel_softmax, sel_values, (g, c))
              for c in range(n_sel_chunks) for g in groups]
    for i in range(len(units) + 2):
        if i < len(units):
            units[i][0](*units[i][3])
        if 0 <= i - 1 < len(units):
            units[i - 1][1](*units[i - 1][3])
        if 0 <= i - 2 < len(units):
            units[i - 2][2](*units[i - 2][3])
            if units[i - 2][2] is cmp_values:
                select(*units[i - 2][3])
    head_out = [piece for g in groups for piece in merge(g)]
    o_ref[...] = jnp.concatenate(head_out, axis=0).T.astype(BF16)


def _nsa_tables(seq):
    nchunk = seq // CMP_STRIDE
    ns = seq // SLC_BLOCK
    cs = np.arange(nchunk)[:, None] * CMP_STRIDE
    ss = np.arange(ns)[None, :] * SLC_BLOCK
    overlap = np.clip(np.minimum(cs + CMP_BLOCK, ss + SLC_BLOCK) - np.maximum(cs, ss), 0, None)
    agg_t = (overlap.astype(np.float32) / CMP_BLOCK).T
    place_t = np.zeros((HEAD_PAD, ns), np.float32)
    place_t[NSA_DK + np.arange(ns), np.arange(ns)] = MASK_BIG
    return jnp.asarray(agg_t, BF16), jnp.asarray(place_t, BF16)


def _nsa_attn(qt, kc, vct, ks, vst, kw, vw, gt):
    b, _, s = qt.shape
    tq = min(NSA_TQ, s)
    tk = min(NSA_TK, s)
    assert s >= WINDOW + tq and s % tk == 0 and WINDOW % tq == 0
    agg_t, place_t = _nsa_tables(s)
    whole = lambda a: pl.BlockSpec((None,) + a.shape[1:], lambda i, j: (i, 0, 0))
    col = lambda a: pl.BlockSpec((None, a.shape[1], tq), lambda i, j: (i, 0, j))
    ow = NSA_HEADS * NSA_DK
    return pl.pallas_call(
        functools.partial(_nsa_attn_kernel, tk=tk, n_sel=min(N_SEL, s // SLC_BLOCK)),
        out_shape=jax.ShapeDtypeStruct((b, s, ow), BF16),
        grid=(b, s // tq),
        in_specs=[
            col(qt), whole(kc), whole(vct), whole(ks), whole(vst), whole(kw), whole(vw), col(gt),
            _const_spec(agg_t.shape), _const_spec(place_t.shape),
        ],
        out_specs=pl.BlockSpec((None, tq, ow), lambda i, j: (i, j, 0)),
        compiler_params=_cparams(("arbitrary", "arbitrary")),
        name="nsa_attn",
    )(qt, kc, vct, ks, vst, kw, vw, gt, agg_t, place_t)


def _nsa_gate_layout(w_in, gate_b):
    qd = NSA_HEADS * NSA_DK
    o = qd + 3 * NSA_KV_W
    perm = np.arange(3 * NSA_HEADS).reshape(NSA_KV_HEADS, NSA_GROUP, 3).transpose(0, 2, 1).reshape(-1)
    extra = HEAD_PAD - perm.size
    w_in = jnp.concatenate([w_in[:, :o], jnp.pad(w_in[:, o:][:, perm], ((0, 0), (0, extra)))], axis=1)
    return w_in, jnp.pad(gate_b[perm], (0, extra))


def kernel(x, c, positions, ada_w, ada_b, norm_g, final_g, ff_w13, ff_w2, hy_w_in, hy_conv_w,
           hy_q_norm, hy_kv_norm, hy_w_uq, hy_w_ukv, hy_w_out, nsa_w_in, nsa_cmp_pe, nsa_cmp_w1,
           nsa_cmp_w2, nsa_gate_b, nsa_w_out):
    depth = ada_w.shape[0]
    mod = _ada_mod(c, ada_w, ada_b)
    ropes = _rope_tables(positions)
    w13 = ff_w13.astype(BF16)
    w2 = ff_w2.astype(BF16)
    for l in range(depth):
        m = l // 2
        x = _ffn(x, mod[l], norm_g[l, 0], w13, w2, l, 0, final_g, k0=0, final=False)
        if l % 2 == 0:
            w_in = hy_w_in[m].astype(BF16)
            w_in = jnp.pad(w_in, ((0, 0), (0, -w_in.shape[1] % HEAD_PAD)))
            y_conv, q, k, v = _hy_proj(x, mod[l], norm_g[l, 1], w_in,
                                       hy_conv_w[m], hy_q_norm[m], hy_kv_norm[m],
                                       _hy_weights(hy_w_uq[m], hy_w_ukv[m]), ropes)
            y_att = _mla_attn(q, k, v)
            w_out = hy_w_out[m].astype(BF16)
            parts, w_parts = [y_conv, y_att], [w_out[:CONV_WIDTH], w_out[CONV_WIDTH:]]
        else:
            w_in, gate_b = _nsa_gate_layout(nsa_w_in[m].astype(BF16), nsa_gate_b[m])
            qt, kvc, ks, vst, kw, vw, gt = _nsa_proj(x, mod[l], norm_g[l, 1], w_in, gate_b)
            kc, vct = _nsa_cmp(kvc, *_nsa_cmp_weights(nsa_cmp_pe[m], nsa_cmp_w1[m], nsa_cmp_w2[m]))
            o = _nsa_attn(qt, kc, vct, ks, vst, kw, vw, gt)
            parts, w_parts = [o], [nsa_w_out[m].astype(BF16)]
        x = _ffn(x, mod[l], norm_g[l, 2], w13, w2, l, 1, final_g, k0=6, final=(l == depth - 1),
                 parts=parts, w_parts=w_parts)
    return x
```

```python
import functools

import jax
import jax.numpy as jnp
import numpy as np
from jax import lax
from jax.experimental import pallas as pl
from jax.experimental.pallas import tpu as pltpu

F32 = jnp.float32
BF16 = jnp.bfloat16

N_ADA = 9
EPS = 1e-6
NEG_INF = -1e30
CONV_WIDTH = 512
CONV_TAPS = 3
MLA_HEADS = 8
MLA_NOPE = 64
MLA_ROPE = 32
MLA_V = 64
Q_LORA = 256
KV_LORA = 128
ROPE_THETA = 10000.0
NSA_HEADS = 16
NSA_KV_HEADS = 2
NSA_GROUP = NSA_HEADS // NSA_KV_HEADS
NSA_DK = 64
CMP_BLOCK = 32
CMP_STRIDE = 16
CMP_HID = 128
SLC_BLOCK = 64
N_SEL = 8
WINDOW = 512
FORCE_SCORE = 1e4
NSA_KV_W = 2 * NSA_KV_HEADS * NSA_DK

HEAD_PAD = 128
VMEM_LIMIT = 56 * 1024 * 1024
ROW_TILE = 512
HY_ROW_TILE = 1024
PROJ_SUBTILES = 4
NSA_PROJ_SUBTILES = 1
FFN_SUBTILES = 1
MLA_TQ = 512
MLA_HEADS_PER_STEP = 4
NSA_TQ = 128
NSA_TK = 256

LOG2E = float(np.log2(np.e))
MASK_BIG = float(2.0 ** 100)


def _cparams(sem):
    return pltpu.CompilerParams(dimension_semantics=sem, vmem_limit_bytes=VMEM_LIMIT)


def _const_spec(shape):
    nd = len(shape)
    return pl.BlockSpec(shape, lambda *_: (0,) * nd, pipeline_mode=pl.Buffered(1))


def _sigmoid(v):
    return 1.0 / (1.0 + jnp.exp(-v))


def _rms(v, g):
    return v * lax.rsqrt(jnp.mean(v * v, axis=-1, keepdims=True) + EPS) * g


def _modulate(x, g, mod_ref, k0):
    shift = mod_ref[k0:k0 + 1, :]
    scale = mod_ref[k0 + 1:k0 + 2, :]
    return _rms(x, g) * (1.0 + scale) + shift


def _dot(a, b):
    return jnp.dot(a, b, preferred_element_type=F32)


def _dot_nt(a, b):
    return lax.dot_general(a, b, (((1,), (1,)), ((), ())), preferred_element_type=F32)


def _split3(a):
    hi = a.astype(BF16)
    r1 = a - hi.astype(F32)
    mid = r1.astype(BF16)
    lo = (r1 - mid.astype(F32)).astype(BF16)
    return hi, mid, lo


def _row_recip(v):
    return 1.0 / jnp.maximum(v, 1e-20)


def _ada_kernel(c_ref, w_ref, b_ref, o_ref):
    c = c_ref[...]
    ca = (c * _sigmoid(c)).astype(BF16)
    o_ref[...] = _dot(ca, w_ref[...].astype(BF16)) + b_ref[...]


def _ada_mod(c, ada_w, ada_b):
    depth, d, n = ada_w.shape
    b = c.shape[0]
    tn = n // 8
    out = pl.pallas_call(
        _ada_kernel,
        out_shape=jax.ShapeDtypeStruct((depth, b, n), F32),
        grid=(depth, n // tn),
        in_specs=[
            pl.BlockSpec((b, d), lambda l, j: (0, 0)),
            pl.BlockSpec((None, d, tn), lambda l, j: (l, 0, j)),
            pl.BlockSpec((None, 1, tn), lambda l, j: (l, 0, j)),
        ],
        out_specs=pl.BlockSpec((None, b, tn), lambda l, j: (l, 0, j)),
        compiler_params=_cparams(("arbitrary", "arbitrary")),
        name="ada_mod",
    )(c, ada_w, ada_b.reshape(depth, 1, n))
    return out.reshape(depth, b, N_ADA, d)


def _ffn_kernel(*refs, k0, d_ff, final, n_parts):
    x_ref, mod_ref, g_ref = refs[:3]
    part_refs = refs[3:3 + n_parts]
    wout_refs = refs[3 + n_parts:3 + 2 * n_parts]
    w13_ref, w2_ref, fg_ref, o_ref = refs[3 + 2 * n_parts:]
    sub = x_ref.shape[0] // FFN_SUBTILES
    rows = [slice(p * sub, (p + 1) * sub) for p in range(FFN_SUBTILES)]
    st = [{} for _ in range(FFN_SUBTILES)]

    def pre(p):
        x = x_ref[rows[p], :]
        if n_parts:
            y = _dot(part_refs[0][rows[p], :], wout_refs[0][...])
            for p_ref, w_ref in zip(part_refs[1:], wout_refs[1:]):
                y = y + _dot(p_ref[rows[p], :], w_ref[...])
            x = x + mod_ref[5:6, :] * y
        st[p]["x"] = x
        st[p]["h"] = _modulate(x, g_ref[...], mod_ref, k0).astype(BF16)

    def up(p):
        st[p]["ab"] = _dot(st[p].pop("h"), w13_ref[...])

    def act(p):
        ab = st[p].pop("ab")
        a = ab[:, :d_ff]
        b = ab[:, d_ff:]
        st[p]["u"] = (a * _sigmoid(a) * b).astype(BF16)

    def down(p):
        st[p]["y"] = _dot(st[p].pop("u"), w2_ref[...])

    def post(p):
        out = st[p].pop("x") + (0.5 * mod_ref[k0 + 2:k0 + 3, :]) * st[p].pop("y")
        if final:
            out = _rms(out, fg_ref[...])
        o_ref[rows[p], :] = out

    pre(0)
    up(0)
    for p in range(FFN_SUBTILES):
        if p + 1 < FFN_SUBTILES:
            pre(p + 1)
            up(p + 1)
        act(p)
        down(p)
        if p:
            post(p - 1)
    post(FFN_SUBTILES - 1)


def _ffn(x, mod_l, g, w13, w2, layer, which, final_g, *, k0, final, parts=(), w_parts=()):
    b, s, d = x.shape
    d_ff = w2.shape[2]
    tm = min(ROW_TILE, s)
    kern = functools.partial(_ffn_kernel, k0=k0, d_ff=d_ff, final=final, n_parts=len(parts))
    pick = lambda *_: (layer, which, 0, 0)
    row = lambda width: pl.BlockSpec((None, tm, width), lambda i, j: (i, j, 0))
    return pl.pallas_call(
        kern,
        out_shape=jax.ShapeDtypeStruct(x.shape, F32),
        grid=(b, s // tm),
        in_specs=[
            row(d),
            pl.BlockSpec((None, N_ADA, d), lambda i, j: (i, 0, 0)),
            _const_spec((1, d)),
        ] + [row(p.shape[-1]) for p in parts] + [_const_spec(w.shape) for w in w_parts] + [
            pl.BlockSpec((None, None, d, 2 * d_ff), pick, pipeline_mode=pl.Buffered(1)),
            pl.BlockSpec((None, None, d_ff, d), pick, pipeline_mode=pl.Buffered(1)),
            _const_spec((1, d)),
        ],
        out_specs=row(d),
        compiler_params=_cparams(("arbitrary", "arbitrary")),
        name="ffn",
    )(x, mod_l, g.reshape(1, d), *parts, *w_parts, w13, w2, final_g.reshape(1, d))


def _rope_swap(t):
    width = t.shape[-1]
    half = MLA_ROPE // 2
    lane = lax.broadcasted_iota(jnp.int32, (1, width), 1) & (HEAD_PAD - 1)
    return jnp.where(lane < MLA_NOPE + half, pltpu.roll(t, width - half, 1), pltpu.roll(t, half, 1))


def _hy_proj_kernel(x_ref, mod_ref, g_ref, win_ref, convw_ref, qn_ref, kvn_ref, wq_ref,
                    wk_ref, wv_ref, rc_ref, rs_ref,
                    yconv_ref, q_ref, k_ref, v_ref, carry_ref, *, q_scale):
    tm = x_ref.shape[0]
    cw = CONV_WIDTH

    @pl.when(pl.program_id(1) == 0)
    def _():
        carry_ref[...] = jnp.zeros_like(carry_ref)

    sub = tm // PROJ_SUBTILES
    rows = [slice(p * sub, (p + 1) * sub) for p in range(PROJ_SUBTILES)]
    st = [{} for _ in range(PROJ_SUBTILES)]
    o = 3 * cw
    n_rep = q_ref.shape[-1] // HEAD_PAD

    def project_in(p):
        h = _modulate(x_ref[rows[p], :], g_ref[...], mod_ref, 3).astype(BF16)
        st[p]["z"] = _dot(h, win_ref[...])

    def conv(p):
        z = st[p]["z"]
        u, gate_c, gate_b = z[:, :cw], z[:, cw:2 * cw], z[:, 2 * cw:3 * cw]
        prev = st[p - 1]["tail"] if p else carry_ref[...]
        v = gate_c * u
        row = lax.broadcasted_iota(jnp.int32, (sub, 1), 0)
        v1 = jnp.where(row == 0, prev[7:8, :], pltpu.roll(v, 1, 0))
        v2 = jnp.where(row == 0, prev[6:7, :], jnp.where(row == 1, prev[7:8, :], pltpu.roll(v, 2, 0)))
        w = convw_ref[...]
        yconv_ref[rows[p], :] = (gate_b * (w[0:1, :] * v2 + w[1:2, :] * v1 + w[2:3, :] * v)).astype(BF16)
        st[p]["tail"] = v[sub - 8:, :]

    def latent_norms(p):
        z = st[p]["z"]
        st[p]["qn"] = _rms(z[:, o:o + Q_LORA], qn_ref[...]).astype(BF16)
        st[p]["kvn"] = _rms(z[:, o + Q_LORA:o + Q_LORA + KV_LORA], kvn_ref[...]).astype(BF16)
        st[p]["kr"] = z[:, o + Q_LORA + KV_LORA:]

    def project_heads(p):
        st[p]["q"] = _dot(st[p].pop("qn"), wq_ref[...])
        kvn = st[p].pop("kvn")
        st[p]["k"] = _dot(kvn, wk_ref[...])
        st[p]["v"] = _dot(kvn, wv_ref[...])

    def rotary_store(p):
        rc = rc_ref[rows[p], :]
        rs = rs_ref[rows[p], :]
        q = st[p].pop("q")
        q = q * jnp.concatenate([rc] * n_rep, axis=1) + _rope_swap(q) * jnp.concatenate([rs] * n_rep, axis=1)
        kr = pltpu.roll(st[p].pop("kr"), MLA_NOPE, 1)
        kr = kr * rc + _rope_swap(kr) * rs
        k = st[p].pop("k") + jnp.concatenate([kr] * n_rep, axis=1)
        q_ref[rows[p], :] = (q * q_scale).astype(BF16)
        k_ref[rows[p], :] = k.astype(BF16)
        lane = lax.broadcasted_iota(jnp.int32, (1, v_ref.shape[-1]), 1)
        ones_col = jnp.where((lane & (HEAD_PAD - 1)) == MLA_V, 1.0, 0.0)
        v_ref[rows[p], :] = (st[p].pop("v") + ones_col).astype(BF16)

    project_in(0)
    for p in range(PROJ_SUBTILES):
        if p + 1 < PROJ_SUBTILES:
            project_in(p + 1)
        latent_norms(p)
        conv(p)
        project_heads(p)
        if p:
            rotary_store(p - 1)
    rotary_store(PROJ_SUBTILES - 1)
    carry_ref[...] = st[PROJ_SUBTILES - 1]["tail"]


def _hy_weights(w_uq, w_ukv):
    hp = HEAD_PAD
    pad_heads = lambda w: jnp.pad(w, ((0, 0), (0, 0), (0, hp - w.shape[-1]))).reshape(w.shape[0], -1)
    wq = pad_heads(w_uq.reshape(Q_LORA, MLA_HEADS, MLA_NOPE + MLA_ROPE))
    wkv = w_ukv.reshape(KV_LORA, MLA_HEADS, MLA_NOPE + MLA_V)
    wk = pad_heads(wkv[..., :MLA_NOPE])
    wv = pad_heads(wkv[..., MLA_NOPE:])
    return wq.astype(BF16), wk.astype(BF16), wv.astype(BF16)


def _rope_tables(positions):
    half = MLA_ROPE // 2
    inv = ROPE_THETA ** (-jnp.arange(half, dtype=F32) / half)
    ang = positions.astype(F32)[..., None] * inv
    cos, sin = jnp.cos(ang), jnp.sin(ang)
    lead = positions.shape
    ones = jnp.ones(lead + (MLA_NOPE,), F32)
    ztail = jnp.zeros(lead + (HEAD_PAD - MLA_NOPE - MLA_ROPE,), F32)
    zhead = jnp.zeros(lead + (MLA_NOPE,), F32)
    rc = jnp.concatenate([ones, cos, cos, ztail], axis=-1)
    rs = jnp.concatenate([zhead, -sin, sin, ztail], axis=-1)
    return rc, rs


def _hy_proj(x, mod_l, g, w_in, conv_w, q_norm, kv_norm, hy_w, ropes):
    b, s, d = x.shape
    tm = min(HY_ROW_TILE, s)
    hp = HEAD_PAD
    qw = MLA_HEADS * hp
    row = lambda width: pl.BlockSpec((None, tm, width), lambda i, j: (i, j, 0))
    q_scale = (MLA_NOPE + MLA_ROPE) ** -0.5 * LOG2E
    return pl.pallas_call(
        functools.partial(_hy_proj_kernel, q_scale=q_scale),
        out_shape=(
            jax.ShapeDtypeStruct((b, s, CONV_WIDTH), BF16),
            jax.ShapeDtypeStruct((b, s, qw), BF16),
            jax.ShapeDtypeStruct((b, s, qw), BF16),
            jax.ShapeDtypeStruct((b, s, qw), BF16),
        ),
        grid=(b, s // tm),
        in_specs=[
            row(d),
            pl.BlockSpec((None, N_ADA, d), lambda i, j: (i, 0, 0)),
            _const_spec((1, d)),
            _const_spec(w_in.shape),
            _const_spec(conv_w.shape),
            _const_spec((1, Q_LORA)),
            _const_spec((1, KV_LORA)),
        ] + [_const_spec(w.shape) for w in hy_w] + [row(hp), row(hp)],
        out_specs=(row(CONV_WIDTH), row(qw), row(qw), row(qw)),
        scratch_shapes=[pltpu.VMEM((8, CONV_WIDTH), F32)],
        compiler_params=_cparams(("arbitrary", "arbitrary")),
        name="hy_proj",
    )(x, mod_l, g.reshape(1, d), w_in, conv_w, q_norm.reshape(1, -1), kv_norm.reshape(1, -1),
      *hy_w, *ropes)


def _mla_attn_body(q_ref, k_ref, v_ref, o_ref, n_main):
    tq = q_ref.shape[0]
    hp = HEAD_PAD
    n_heads = q_ref.shape[1] // hp
    rel = (lax.broadcasted_iota(jnp.int32, (1, tq), 1)
           <= lax.broadcasted_iota(jnp.int32, (tq, 1), 0))
    diag_bias = jnp.where(rel, 0.0, -MASK_BIG)
    edge = slice(n_main, n_main + tq)
    cols = [slice(hh * hp, (hh + 1) * hp) for hh in range(n_heads)]
    st = [{} for _ in range(n_heads)]

    def scores(hh):
        q = q_ref[:, cols[hh]]
        st[hh]["s_e"] = _dot_nt(q, k_ref[edge, cols[hh]]) + diag_bias
        if n_main:
            st[hh]["s_m"] = _dot_nt(q, k_ref[0:n_main, cols[hh]])

    def softmax(hh):
        s_e = st[hh].pop("s_e")
        m = jnp.max(s_e, axis=-1, keepdims=True)
        if n_main:
            s_m = st[hh].pop("s_m")
            m = jnp.maximum(m, jnp.max(s_m, axis=-1, keepdims=True))
            st[hh]["p_m"] = jnp.exp2(s_m - m).astype(BF16)
        st[hh]["p_e"] = jnp.exp2(s_e - m).astype(BF16)

    def values(hh):
        acc = _dot(st[hh].pop("p_e"), v_ref[edge, cols[hh]])
        if n_main:
            acc = acc + _dot(st[hh].pop("p_m"), v_ref[0:n_main, cols[hh]])
        st[hh]["out"] = acc * _row_recip(acc[:, MLA_V:MLA_V + 1])

    scores(0)
    for hh in range(n_heads):
        if hh + 1 < n_heads:
            scores(hh + 1)
        softmax(hh)
        if hh:
            values(hh - 1)
    values(n_heads - 1)
    outs = [st[hh].pop("out") for hh in range(n_heads)]
    lane = lax.broadcasted_iota(jnp.int32, (1, hp), 1)
    for pair in range(n_heads // 2):
        both = jnp.where(lane < MLA_V, outs[2 * pair], pltpu.roll(outs[2 * pair + 1], MLA_V, 1))
        o_ref[:, pair * hp:(pair + 1) * hp] = both.astype(BF16)


def _mla_attn_kernel(q_ref, k_ref, v_ref, o_ref):
    tq = q_ref.shape[0]
    qi = pl.program_id(2)
    for j in range(k_ref.shape[0] // tq):
        pl.when(qi == j)(functools.partial(_mla_attn_body, q_ref, k_ref, v_ref, o_ref, j * tq))


def _mla_attn(q, k, v):
    b, s, _ = q.shape
    tq = min(MLA_TQ, s)
    qw = MLA_HEADS_PER_STEP * HEAD_PAD
    ow = MLA_HEADS_PER_STEP * MLA_V
    return pl.pallas_call(
        _mla_attn_kernel,
        out_shape=jax.ShapeDtypeStruct((b, s, MLA_HEADS * MLA_V), BF16),
        grid=(b, MLA_HEADS // MLA_HEADS_PER_STEP, s // tq),
        in_specs=[
            pl.BlockSpec((None, tq, qw), lambda i, h, j: (i, j, h)),
            pl.BlockSpec((None, s, qw), lambda i, h, j: (i, 0, h)),
            pl.BlockSpec((None, s, qw), lambda i, h, j: (i, 0, h)),
        ],
        out_specs=pl.BlockSpec((None, tq, ow), lambda i, h, j: (i, j, h)),
        compiler_params=_cparams(("arbitrary", "arbitrary", "arbitrary")),
        name="mla_attn",
    )(q, k, v)


def _nsa_proj_kernel(x_ref, mod_ref, g_ref, win_ref, gb_ref, qt_ref, kvc_ref, ks_ref, vst_ref,
                     kw_ref, vw_ref, gt_ref, *, q_scale):
    tm = x_ref.shape[0]
    hp, dk = HEAD_PAD, NSA_DK
    pad = hp - dk
    qd = NSA_HEADS * dk
    kw = NSA_KV_W
    sub = tm // NSA_PROJ_SUBTILES
    zs = {}

    def project_in(p):
        h = _modulate(x_ref[p * sub:(p + 1) * sub, :], g_ref[...], mod_ref, 3).astype(BF16)
        zs[p] = _dot(h, win_ref[...])

    def emit(p):
        z = zs.pop(p)
        at = slice(p * sub, (p + 1) * sub)
        zeros = jnp.zeros((sub, pad), BF16)
        zero_rows = jnp.zeros((pad, sub), BF16)
        for pair in range(NSA_HEADS // 2):
            qt = (z[:, pair * hp:(pair + 1) * hp] * q_scale).T.astype(BF16)
            for half in range(2):
                hd = 2 * pair + half
                qt_ref[hd * hp:hd * hp + dk, at] = qt[half * dk:(half + 1) * dk, :]
                qt_ref[hd * hp + dk:(hd + 1) * hp, at] = zero_rows
        kvc_ref[at, :] = z[:, qd:qd + kw].astype(BF16)
        pos = pl.program_id(1) * tm + p * sub + lax.broadcasted_iota(jnp.int32, (sub, 1), 0)
        lane = lax.broadcasted_iota(jnp.int32, (1, pad), 1)
        blk_onehot = jnp.where(lax.shift_right_logical(pos, int(np.log2(SLC_BLOCK))) == lane, 1.0, 0.0)
        blk_onehot = blk_onehot.astype(BF16)
        ones_col = jnp.broadcast_to(jnp.where(lane == 0, 1.0, 0.0), (sub, pad)).astype(BF16)
        kvs = z[:, qd + kw:qd + 2 * kw].astype(BF16)
        kvw = z[:, qd + 2 * kw:qd + 3 * kw].astype(BF16)
        for g in range(NSA_KV_HEADS):
            lo, mid, hi = g * hp, g * hp + dk, (g + 1) * hp
            kc = slice(g * dk, (g + 1) * dk)
            vc = slice((NSA_KV_HEADS + g) * dk, (NSA_KV_HEADS + g + 1) * dk)
            ks_ref[at, lo:mid] = kvs[:, kc]
            ks_ref[at, mid:hi] = blk_onehot
            kw_ref[at, lo:mid] = kvw[:, kc]
            kw_ref[at, mid:hi] = zeros
            vw_ref[at, lo:mid] = kvw[:, vc]
            vw_ref[at, mid:hi] = ones_col
        v_off = qd + kw + NSA_KV_HEADS * dk
        vt = z[:, v_off:v_off + NSA_KV_HEADS * dk].T.astype(BF16)
        ones_row = jnp.where(lax.broadcasted_iota(jnp.int32, (pad, 1), 0) == 0, 1.0, 0.0)
        ones_row = jnp.broadcast_to(ones_row, (pad, sub)).astype(BF16)
        for g in range(NSA_KV_HEADS):
            vst_ref[g * hp:g * hp + dk, at] = vt[g * dk:(g + 1) * dk, :]
            vst_ref[g * hp + dk:(g + 1) * hp, at] = ones_row
        gt_ref[:, at] = _sigmoid(z[:, qd + 3 * kw:] + gb_ref[...]).T

    project_in(0)
    for p in range(NSA_PROJ_SUBTILES):
        if p + 1 < NSA_PROJ_SUBTILES:
            project_in(p + 1)
        emit(p)


def _nsa_proj(x, mod_l, g, w_in, gate_b):
    b, s, d = x.shape
    tm = min(ROW_TILE, s)
    assert s // SLC_BLOCK <= HEAD_PAD - NSA_DK
    qw = NSA_HEADS * HEAD_PAD
    kvw = NSA_KV_HEADS * HEAD_PAD
    ng = gate_b.shape[-1]
    row = lambda width: pl.BlockSpec((None, tm, width), lambda i, j: (i, j, 0))
    col = lambda height: pl.BlockSpec((None, height, tm), lambda i, j: (i, 0, j))
    sds = lambda width, dt=BF16: jax.ShapeDtypeStruct((b, s, width), dt)
    sds_t = lambda height, dt=BF16: jax.ShapeDtypeStruct((b, height, s), dt)
    return pl.pallas_call(
        functools.partial(_nsa_proj_kernel, q_scale=NSA_DK ** -0.5 * LOG2E),
        out_shape=(sds_t(qw), sds(NSA_KV_W), sds(kvw), sds_t(kvw), sds(kvw), sds(kvw), sds_t(ng, F32)),
        grid=(b, s // tm),
        in_specs=[
            row(d),
            pl.BlockSpec((None, N_ADA, d), lambda i, j: (i, 0, 0)),
            _const_spec((1, d)),
            _const_spec(w_in.shape),
            _const_spec((1, ng)),
        ],
        out_specs=(col(qw), row(NSA_KV_W), row(kvw), col(kvw), row(kvw), row(kvw), col(ng)),
        compiler_params=_cparams(("arbitrary", "arbitrary")),
        name="nsa_proj",
    )(x, mod_l, g.reshape(1, d), w_in, gate_b.reshape(1, ng))


def _nsa_cmp_kernel(ch_ref, w1a_ref, w1b_ref, pea_ref, peb_ref, w2_ref, k_ref, vt_ref):
    ch = ch_ref[...]
    n = ch.shape[0]
    first = _dot(ch, w1a_ref[...])
    second = _dot(ch, w1b_ref[...])
    bias = _dot(pea_ref[...], w1a_ref[...]) + _dot(peb_ref[...], w1b_ref[...])
    hid = first + pltpu.roll(second, n - 1, 0) + bias[0:1, :]
    act = (hid * _sigmoid(hid)).astype(BF16)
    kv = _dot(act, w2_ref[...])
    half = kv.shape[1] // 2
    k_ref[...] = kv[:, :half].astype(BF16)
    vt_ref[...] = kv[:, half:].T.astype(BF16)


def _nsa_cmp_weights(cmp_pe, cmp_w1, cmp_w2):
    ncomp = 2 * NSA_KV_HEADS
    half = CMP_BLOCK // 2
    kv_of = np.arange(ncomp) // NSA_KV_HEADS
    eye = jnp.eye(ncomp, dtype=BF16)
    w1 = cmp_w1.astype(BF16).reshape(2, 2, half, NSA_DK, CMP_HID)[kv_of]
    big1 = jnp.einsum("chldj,ce->hlcdej", w1, eye).reshape(2, half * ncomp * NSA_DK, ncomp * CMP_HID)
    w2 = jnp.pad(cmp_w2.astype(BF16)[kv_of], ((0, 0), (0, 0), (0, HEAD_PAD - NSA_DK)))
    big2 = jnp.einsum("cjd,ce->cjed", w2, eye).reshape(ncomp * CMP_HID, ncomp * HEAD_PAD)
    pe = cmp_pe.astype(BF16).reshape(2, 2, half, NSA_DK)[kv_of]
    pe = jnp.transpose(pe, (1, 2, 0, 3)).reshape(2, 1, half * ncomp * NSA_DK)
    pe = jnp.broadcast_to(pe, (2, 8, half * ncomp * NSA_DK))
    return big1[0], big1[1], pe[0], pe[1], big2


def _nsa_cmp(kvc, w1a, w1b, pea, peb, w2):
    b, s, kw = kvc.shape
    nchunk = s // CMP_STRIDE
    ow = w2.shape[1]
    chunks = kvc.reshape(b, nchunk, CMP_STRIDE * kw)
    half = ow // 2
    return pl.pallas_call(
        _nsa_cmp_kernel,
        out_shape=(jax.ShapeDtypeStruct((b, nchunk, half), BF16),
                   jax.ShapeDtypeStruct((b, half, nchunk), BF16)),
        grid=(b,),
        in_specs=[
            pl.BlockSpec((None, nchunk, CMP_STRIDE * kw), lambda i: (i, 0, 0)),
            _const_spec(w1a.shape), _const_spec(w1b.shape),
            _const_spec(pea.shape), _const_spec(peb.shape), _const_spec(w2.shape),
        ],
        out_specs=(pl.BlockSpec((None, nchunk, half), lambda i: (i, 0, 0)),
                   pl.BlockSpec((None, half, nchunk), lambda i: (i, 0, 0))),
        compiler_params=_cparams(("arbitrary",)),
        name="nsa_cmp",
    )(chunks, w1a, w1b, pea, peb, w2)


def _add_per_head(s, bias, rep):
    tq = bias.shape[1]
    return jnp.concatenate([s[:, r * tq:(r + 1) * tq] + bias for r in range(rep)], axis=1)


def _nsa_attn_kernel(qt_ref, kc_ref, vct_ref, ks_ref, vst_ref, kw_ref, vw_ref, gt_ref, aggt_ref,
                     placet_ref, o_ref, *, tk, n_sel):
    tq = qt_ref.shape[1]
    per = tk // tq
    qi = pl.program_id(1)
    for j in range(ks_ref.shape[0] // tk):
        in_class = (qi >= j * per) & (qi < (j + 1) * per)
        pl.when(in_class)(functools.partial(
            _nsa_attn_body, qt_ref, kc_ref, vct_ref, ks_ref, vst_ref, kw_ref, vw_ref, gt_ref,
            aggt_ref, placet_ref, o_ref, tk=tk, n_sel=n_sel, n_main=j * tk))


def _nsa_attn_body(qt_ref, kc_ref, vct_ref, ks_ref, vst_ref, kw_ref, vw_ref, gt_ref, aggt_ref,
                   placet_ref, o_ref, *, tk, n_sel, n_main):
    tq = qt_ref.shape[1]
    seq = ks_ref.shape[0]
    ncp = kc_ref.shape[0]
    ns = aggt_ref.shape[0]
    hp, dk, rep = HEAD_PAD, NSA_DK, NSA_GROUP
    q0 = pl.program_id(1) * tq
    t = q0 + lax.broadcasted_iota(jnp.int32, (1, tq), 1)
    t_heads = jnp.concatenate([t] * rep, axis=1)

    blk = lax.broadcasted_iota(jnp.int32, (ns, 1), 0)
    cur = lax.shift_right_logical(t, int(np.log2(SLC_BLOCK)))
    forced = (blk == 0) | (blk == cur) | (blk == cur - 1)
    causal_blk = blk * SLC_BLOCK <= t
    cmp_end = lax.broadcasted_iota(jnp.int32, (ncp, 1), 0) * CMP_STRIDE + (CMP_BLOCK - 1)
    cmp_mask = (cmp_end <= t_heads) & (cmp_end < seq)
    win_len = WINDOW + tq
    w_start = pl.multiple_of(jnp.maximum(q0 - WINDOW, 0), tq)
    wpos = w_start + lax.broadcasted_iota(jnp.int32, (win_len, 1), 0)
    win_bias = jnp.where((wpos <= t) & (wpos > t - WINDOW), 0.0, -MASK_BIG)
    last_pos = n_main + lax.broadcasted_iota(jnp.int32, (tk, 1), 0)
    last_bias = jnp.where(last_pos <= t, 0.0, -MASK_BIG)
    pad_row = lax.broadcasted_iota(jnp.int32, (hp, 1), 0)
    sel_offset = jnp.where((pad_row >= dk) & (pad_row < dk + ns), MASK_BIG, 0.0)
    gates = gt_ref[...]

    n_groups = NSA_KV_HEADS
    kv_cols = [slice(g * hp, (g + 1) * hp) for g in range(n_groups)]
    q_heads = [[qt_ref[(g * rep + r) * hp:(g * rep + r + 1) * hp, :] for r in range(rep)]
               for g in range(n_groups)]
    qs = [jnp.concatenate(q_heads[g], axis=1) for g in range(n_groups)]
    st = [{} for _ in range(n_groups)]

    def cmp_scores(g):
        st[g]["s_c"] = jnp.where(cmp_mask, _dot(kc_ref[:, kv_cols[g]], qs[g]), NEG_INF)

    def cmp_softmax(g):
        s = st[g].pop("s_c")
        p = jnp.where(cmp_mask, jnp.exp2(s - jnp.max(s, axis=0, keepdims=True)), 0.0)
        st[g]["p_c"] = p * _row_recip(jnp.sum(p, axis=0, keepdims=True))

    def cmp_values(g):
        p_c = st[g].pop("p_c")
        st[g]["o_c"] = _dot(vct_ref[kv_cols[g], :], p_c.astype(BF16))
        p_sum = p_c[:, 0:tq]
        for r in range(1, rep):
            p_sum = p_sum + p_c[:, r * tq:(r + 1) * tq]
        agg_t = aggt_ref[...]
        hi, mid, lo = _split3(p_sum)
        st[g]["imp"] = _dot(agg_t, hi) + _dot(agg_t, mid) + _dot(agg_t, lo)

    def select(g):
        val = jnp.where(forced, FORCE_SCORE, jnp.where(causal_blk, st[g].pop("imp"), NEG_INF))
        rank = jnp.zeros((ns, tq), F32)
        for i in range(ns):
            other = val[i:i + 1, :]
            beats = (other > val) | ((other == val) & (blk > i))
            rank = rank + jnp.where(beats, 1.0, 0.0)
        sel = jnp.where((rank < n_sel) & causal_blk, 1.0, 0.0).astype(BF16)
        sel_pad = _dot(placet_ref[...], sel) - sel_offset
        st[g]["q_sel"] = jnp.concatenate(
            [(qh.astype(F32) + sel_pad).astype(BF16) for qh in q_heads[g]], axis=1)

    def win_scores(g):
        s = _dot(kw_ref[pl.ds(w_start, win_len), kv_cols[g]], qs[g])
        st[g]["s_w"] = _add_per_head(s, win_bias, rep)

    def win_softmax(g):
        s = st[g].pop("s_w")
        st[g]["p_w"] = jnp.exp2(s - jnp.max(s, axis=0, keepdims=True)).astype(BF16)

    def win_values(g):
        vw_t = vw_ref[pl.ds(w_start, win_len), kv_cols[g]].astype(F32).T.astype(BF16)
        st[g]["acc_w"] = _dot(vw_t, st[g].pop("p_w"))

    n_sel_chunks = n_main // tk + 1

    def sel_scores(g, c):
        keys = slice(c * tk, (c + 1) * tk)
        s = _dot(ks_ref[keys, kv_cols[g]], st[g]["q_sel"])
        st[g]["s_s", c] = _add_per_head(s, last_bias, rep) if c == n_sel_chunks - 1 else s

    def sel_softmax(g, c):
        s = st[g].pop(("s_s", c))
        m = jnp.max(s, axis=0, keepdims=True)
        st[g]["m_s", c] = m
        st[g]["p_s", c] = jnp.exp2(s - m).astype(BF16)

    def sel_values(g, c):
        keys = slice(c * tk, (c + 1) * tk)
        st[g]["acc_s", c] = _dot(vst_ref[kv_cols[g], keys], st[g].pop(("p_s", c)))

    def merge(g):
        o_c, acc_w = st[g].pop("o_c"), st[g].pop("acc_w")
        m_all = st[g]["m_s", 0]
        for c in range(1, n_sel_chunks):
            m_all = jnp.maximum(m_all, st[g]["m_s", c])
        acc_s = None
        for c in range(n_sel_chunks):
            part = st[g].pop(("acc_s", c))
            if n_sel_chunks > 1:
                part = part * jnp.exp2(st[g].pop(("m_s", c)) - m_all)
            acc_s = part if acc_s is None else acc_s + part
        r_s = _row_recip(acc_s[dk:dk + 1, :])
        r_w = _row_recip(acc_w[dk:dk + 1, :])
        out = []
        for r in range(rep):
            cols = slice(r * tq, (r + 1) * tq)
            gc = g * 3 * rep + r
            o_h = (gates[gc:gc + 1, :] * o_c[:, cols]
                   + (gates[gc + rep:gc + rep + 1, :] * r_s[:, cols]) * acc_s[:, cols]
                   + (gates[gc + 2 * rep:gc + 2 * rep + 1, :] * r_w[:, cols]) * acc_w[:, cols])
            out.append(o_h[0:dk, :])
        return out

    groups = range(n_groups)
    units = [(cmp_scores, cmp_softmax, cmp_values, (g,)) for g in groups]
    units += [(win_scores, win_softmax, win_values, (g,)) for g in groups]
    units += [(sel_scores, sel_softmax, sel_values, (g, c))
              for c in range(n_sel_chunks) for g in groups]
    for i in range(len(units) + 2):
        if i < len(units):
            units[i][0](*units[i][3])
        if 0 <= i - 1 < len(units):
            units[i - 1][1](*units[i - 1][3])
        if 0 <= i - 2 < len(units):
            units[i - 2][2](*units[i - 2][3])
            if units[i - 2][2] is cmp_values:
                select(*units[i - 2][3])
    head_out = [piece for g in groups for piece in merge(g)]
    o_ref[...] = jnp.concatenate(head_out, axis=0).T.astype(BF16)


def _nsa_tables(seq):
    nchunk = seq // CMP_STRIDE
    ns = seq // SLC_BLOCK
    cs = np.arange(nchunk)[:, None] * CMP_STRIDE
    ss = np.arange(ns)[None, :] * SLC_BLOCK
    overlap = np.clip(np.minimum(cs + CMP_BLOCK, ss + SLC_BLOCK) - np.maximum(cs, ss), 0, None)
    agg_t = (overlap.astype(np.float32) / CMP_BLOCK).T
    place_t = np.zeros((HEAD_PAD, ns), np.float32)
    place_t[NSA_DK + np.arange(ns), np.arange(ns)] = MASK_BIG
    return jnp.asarray(agg_t, BF16), jnp.asarray(place_t, BF16)


def _nsa_attn(qt, kc, vct, ks, vst, kw, vw, gt):
    b, _, s = qt.shape
    tq = min(NSA_TQ, s)
    tk = min(NSA_TK, s)
    assert s >= WINDOW + tq and s % tk == 0 and WINDOW % tq == 0
    agg_t, place_t = _nsa_tables(s)
    whole = lambda a: pl.BlockSpec((None,) + a.shape[1:], lambda i, j: (i, 0, 0))
    col = lambda a: pl.BlockSpec((None, a.shape[1], tq), lambda i, j: (i, 0, j))
    ow = NSA_HEADS * NSA_DK
    return pl.pallas_call(
        functools.partial(_nsa_attn_kernel, tk=tk, n_sel=min(N_SEL, s // SLC_BLOCK)),
        out_shape=jax.ShapeDtypeStruct((b, s, ow), BF16),
        grid=(b, s // tq),
        in_specs=[
            col(qt), whole(kc), whole(vct), whole(ks), whole(vst), whole(kw), whole(vw), col(gt),
            _const_spec(agg_t.shape), _const_spec(place_t.shape),
        ],
        out_specs=pl.BlockSpec((None, tq, ow), lambda i, j: (i, j, 0)),
        compiler_params=_cparams(("arbitrary", "arbitrary")),
        name="nsa_attn",
    )(qt, kc, vct, ks, vst, kw, vw, gt, agg_t, place_t)


def _nsa_gate_layout(w_in, gate_b):
    qd = NSA_HEADS * NSA_DK
    o = qd + 3 * NSA_KV_W
    perm = np.arange(3 * NSA_HEADS).reshape(NSA_KV_HEADS, NSA_GROUP, 3).transpose(0, 2, 1).reshape(-1)
    extra = HEAD_PAD - perm.size
    w_in = jnp.concatenate([w_in[:, :o], jnp.pad(w_in[:, o:][:, perm], ((0, 0), (0, extra)))], axis=1)
    return w_in, jnp.pad(gate_b[perm], (0, extra))


def kernel(x, c, positions, ada_w, ada_b, norm_g, final_g, ff_w13, ff_w2, hy_w_in, hy_conv_w,
           hy_q_norm, hy_kv_norm, hy_w_uq, hy_w_ukv, hy_w_out, nsa_w_in, nsa_cmp_pe, nsa_cmp_w1,
           nsa_cmp_w2, nsa_gate_b, nsa_w_out):
    depth = ada_w.shape[0]
    mod = _ada_mod(c, ada_w, ada_b)
    ropes = _rope_tables(positions)
    w13 = ff_w13.astype(BF16)
    w2 = ff_w2.astype(BF16)
    for l in range(depth):
        m = l // 2
        x = _ffn(x, mod[l], norm_g[l, 0], w13, w2, l, 0, final_g, k0=0, final=False)
        if l % 2 == 0:
            w_in = hy_w_in[m].astype(BF16)
            w_in = jnp.pad(w_in, ((0, 0), (0, -w_in.shape[1] % HEAD_PAD)))
            y_conv, q, k, v = _hy_proj(x, mod[l], norm_g[l, 1], w_in,
                                       hy_conv_w[m], hy_q_norm[m], hy_kv_norm[m],
                                       _hy_weights(hy_w_uq[m], hy_w_ukv[m]), ropes)
            y_att = _mla_attn(q, k, v)
            w_out = hy_w_out[m].astype(BF16)
            parts, w_parts = [y_conv, y_att], [w_out[:CONV_WIDTH], w_out[CONV_WIDTH:]]
        else:
            w_in, gate_b = _nsa_gate_layout(nsa_w_in[m].astype(BF16), nsa_gate_b[m])
            qt, kvc, ks, vst, kw, vw, gt = _nsa_proj(x, mod[l], norm_g[l, 1], w_in, gate_b)
            kc, vct = _nsa_cmp(kvc, *_nsa_cmp_weights(nsa_cmp_pe[m], nsa_cmp_w1[m], nsa_cmp_w2[m]))
            o = _nsa_attn(qt, kc, vct, ks, vst, kw, vw, gt)
            parts, w_parts = [o], [nsa_w_out[m].astype(BF16)]
        x = _ffn(x, mod[l], norm_g[l, 2], w13, w2, l, 1, final_g, k0=6, final=(l == depth - 1),
                 parts=parts, w_parts=w_parts)
    return x
```

```python
import functools

import jax
import jax.numpy as jnp
import numpy as np
from jax import lax
from jax.experimental import pallas as pl
from jax.experimental.pallas import tpu as pltpu

F32 = jnp.float32
BF16 = jnp.bfloat16

N_ADA = 9
EPS = 1e-6
NEG_INF = -1e30
CONV_WIDTH = 512
CONV_TAPS = 3
MLA_HEADS = 8
MLA_NOPE = 64
MLA_ROPE = 32
MLA_V = 64
Q_LORA = 256
KV_LORA = 128
ROPE_THETA = 10000.0
NSA_HEADS = 16
NSA_KV_HEADS = 2
NSA_GROUP = NSA_HEADS // NSA_KV_HEADS
NSA_DK = 64
CMP_BLOCK = 32
CMP_STRIDE = 16
CMP_HID = 128
SLC_BLOCK = 64
N_SEL = 8
WINDOW = 512
FORCE_SCORE = 1e4
NSA_KV_W = 2 * NSA_KV_HEADS * NSA_DK

HEAD_PAD = 128
VMEM_LIMIT = 56 * 1024 * 1024
ROW_TILE = 512
HY_ROW_TILE = 1024
PROJ_SUBTILES = 4
NSA_PROJ_SUBTILES = 1
FFN_SUBTILES = 1
MLA_TQ = 512
MLA_HEADS_PER_STEP = 4
NSA_TQ = 128
NSA_TK = 512

LOG2E = float(np.log2(np.e))
MASK_BIG = float(2.0 ** 100)


def _cparams(sem):
    return pltpu.CompilerParams(dimension_semantics=sem, vmem_limit_bytes=VMEM_LIMIT)


def _const_spec(shape):
    nd = len(shape)
    return pl.BlockSpec(shape, lambda *_: (0,) * nd, pipeline_mode=pl.Buffered(1))


def _sigmoid(v):
    return 1.0 / (1.0 + jnp.exp(-v))


def _rms(v, g):
    return v * lax.rsqrt(jnp.mean(v * v, axis=-1, keepdims=True) + EPS) * g


def _modulate(x, g, mod_ref, k0):
    shift = mod_ref[k0:k0 + 1, :]
    scale = mod_ref[k0 + 1:k0 + 2, :]
    return _rms(x, g) * (1.0 + scale) + shift


def _dot(a, b):
    return jnp.dot(a, b, preferred_element_type=F32)


def _dot_nt(a, b):
    return lax.dot_general(a, b, (((1,), (1,)), ((), ())), preferred_element_type=F32)


def _split3(a):
    hi = a.astype(BF16)
    r1 = a - hi.astype(F32)
    mid = r1.astype(BF16)
    lo = (r1 - mid.astype(F32)).astype(BF16)
    return hi, mid, lo


def _row_recip(v):
    return 1.0 / jnp.maximum(v, 1e-20)


def _ada_kernel(c_ref, w_ref, b_ref, o_ref):
    c = c_ref[...]
    ca = (c * _sigmoid(c)).astype(BF16)
    o_ref[...] = _dot(ca, w_ref[...].astype(BF16)) + b_ref[...]


def _ada_mod(c, ada_w, ada_b):
    depth, d, n = ada_w.shape
    b = c.shape[0]
    tn = n // 8
    out = pl.pallas_call(
        _ada_kernel,
        out_shape=jax.ShapeDtypeStruct((depth, b, n), F32),
        grid=(depth, n // tn),
        in_specs=[
            pl.BlockSpec((b, d), lambda l, j: (0, 0)),
            pl.BlockSpec((None, d, tn), lambda l, j: (l, 0, j)),
            pl.BlockSpec((None, 1, tn), lambda l, j: (l, 0, j)),
        ],
        out_specs=pl.BlockSpec((None, b, tn), lambda l, j: (l, 0, j)),
        compiler_params=_cparams(("arbitrary", "arbitrary")),
        name="ada_mod",
    )(c, ada_w, ada_b.reshape(depth, 1, n))
    return out.reshape(depth, b, N_ADA, d)


def _ffn_kernel(*refs, k0, d_ff, final, n_parts):
    x_ref, mod_ref, g_ref = refs[:3]
    part_refs = refs[3:3 + n_parts]
    wout_refs = refs[3 + n_parts:3 + 2 * n_parts]
    w13_ref, w2_ref, fg_ref, o_ref = refs[3 + 2 * n_parts:]
    sub = x_ref.shape[0] // FFN_SUBTILES
    rows = [slice(p * sub, (p + 1) * sub) for p in range(FFN_SUBTILES)]
    st = [{} for _ in range(FFN_SUBTILES)]

    def pre(p):
        x = x_ref[rows[p], :]
        if n_parts:
            y = _dot(part_refs[0][rows[p], :], wout_refs[0][...])
            for p_ref, w_ref in zip(part_refs[1:], wout_refs[1:]):
                y = y + _dot(p_ref[rows[p], :], w_ref[...])
            x = x + mod_ref[5:6, :] * y
        st[p]["x"] = x
        st[p]["h"] = _modulate(x, g_ref[...], mod_ref, k0).astype(BF16)

    def up(p):
        st[p]["ab"] = _dot(st[p].pop("h"), w13_ref[...])

    def act(p):
        ab = st[p].pop("ab")
        a = ab[:, :d_ff]
        b = ab[:, d_ff:]
        st[p]["u"] = (a * _sigmoid(a) * b).astype(BF16)

    def down(p):
        st[p]["y"] = _dot(st[p].pop("u"), w2_ref[...])

    def post(p):
        out = st[p].pop("x") + (0.5 * mod_ref[k0 + 2:k0 + 3, :]) * st[p].pop("y")
        if final:
            out = _rms(out, fg_ref[...])
        o_ref[rows[p], :] = out

    pre(0)
    up(0)
    for p in range(FFN_SUBTILES):
        if p + 1 < FFN_SUBTILES:
            pre(p + 1)
            up(p + 1)
        act(p)
        down(p)
        if p:
            post(p - 1)
    post(FFN_SUBTILES - 1)


def _ffn(x, mod_l, g, w13, w2, layer, which, final_g, *, k0, final, parts=(), w_parts=()):
    b, s, d = x.shape
    d_ff = w2.shape[2]
    tm = min(ROW_TILE, s)
    kern = functools.partial(_ffn_kernel, k0=k0, d_ff=d_ff, final=final, n_parts=len(parts))
    pick = lambda *_: (layer, which, 0, 0)
    row = lambda width: pl.BlockSpec((None, tm, width), lambda i, j: (i, j, 0))
    return pl.pallas_call(
        kern,
        out_shape=jax.ShapeDtypeStruct(x.shape, F32),
        grid=(b, s // tm),
        in_specs=[
            row(d),
            pl.BlockSpec((None, N_ADA, d), lambda i, j: (i, 0, 0)),
            _const_spec((1, d)),
        ] + [row(p.shape[-1]) for p in parts] + [_const_spec(w.shape) for w in w_parts] + [
            pl.BlockSpec((None, None, d, 2 * d_ff), pick, pipeline_mode=pl.Buffered(1)),
            pl.BlockSpec((None, None, d_ff, d), pick, pipeline_mode=pl.Buffered(1)),
            _const_spec((1, d)),
        ],
        out_specs=row(d),
        compiler_params=_cparams(("arbitrary", "arbitrary")),
        name="ffn",
    )(x, mod_l, g.reshape(1, d), *parts, *w_parts, w13, w2, final_g.reshape(1, d))


def _rope_swap(t):
    width = t.shape[-1]
    half = MLA_ROPE // 2
    lane = lax.broadcasted_iota(jnp.int32, (1, width), 1) & (HEAD_PAD - 1)
    return jnp.where(lane < MLA_NOPE + half, pltpu.roll(t, width - half, 1), pltpu.roll(t, half, 1))


def _hy_proj_kernel(x_ref, mod_ref, g_ref, win_ref, convw_ref, qn_ref, kvn_ref, wq_ref,
                    wk_ref, wv_ref, rc_ref, rs_ref,
                    yconv_ref, q_ref, k_ref, v_ref, carry_ref, *, q_scale):
    tm = x_ref.shape[0]
    cw = CONV_WIDTH

    @pl.when(pl.program_id(1) == 0)
    def _():
        carry_ref[...] = jnp.zeros_like(carry_ref)

    sub = tm // PROJ_SUBTILES
    rows = [slice(p * sub, (p + 1) * sub) for p in range(PROJ_SUBTILES)]
    st = [{} for _ in range(PROJ_SUBTILES)]
    o = 3 * cw
    n_rep = k_ref.shape[-1] // HEAD_PAD

    def project_in(p):
        h = _modulate(x_ref[rows[p], :], g_ref[...], mod_ref, 3).astype(BF16)
        st[p]["z"] = _dot(h, win_ref[...])

    def conv(p):
        z = st[p]["z"]
        u, gate_c, gate_b = z[:, :cw], z[:, cw:2 * cw], z[:, 2 * cw:3 * cw]
        prev = st[p - 1]["tail"] if p else carry_ref[...]
        v = gate_c * u
        row = lax.broadcasted_iota(jnp.int32, (sub, 1), 0)
        v1 = jnp.where(row == 0, prev[7:8, :], pltpu.roll(v, 1, 0))
        v2 = jnp.where(row == 0, prev[6:7, :], jnp.where(row == 1, prev[7:8, :], pltpu.roll(v, 2, 0)))
        w = convw_ref[...]
        yconv_ref[rows[p], :] = (gate_b * (w[0:1, :] * v2 + w[1:2, :] * v1 + w[2:3, :] * v)).astype(BF16)
        st[p]["tail"] = v[sub - 8:, :]

    def latent_norms(p):
        z = st[p]["z"]
        st[p]["qn"] = _rms(z[:, o:o + Q_LORA], qn_ref[...]).astype(BF16)
        st[p]["kvn"] = _rms(z[:, o + Q_LORA:o + Q_LORA + KV_LORA], kvn_ref[...]).astype(BF16)
        st[p]["kr"] = z[:, o + Q_LORA + KV_LORA:]

    def project_heads(p):
        st[p]["q"] = _dot(st[p].pop("qn"), wq_ref[...])
        kvn = st[p].pop("kvn")
        st[p]["k"] = _dot(kvn, wk_ref[...])
        st[p]["v"] = _dot(kvn, wv_ref[...])

    def rotary_store(p):
        rc = rc_ref[rows[p], :]
        rs = rs_ref[rows[p], :]
        q = st[p].pop("q")
        q = q * jnp.concatenate([rc] * n_rep, axis=1) + _rope_swap(q) * jnp.concatenate([rs] * n_rep, axis=1)
        kr = pltpu.roll(st[p].pop("kr"), MLA_NOPE, 1)
        kr = kr * rc + _rope_swap(kr) * rs
        k = st[p].pop("k") + jnp.concatenate([kr] * n_rep, axis=1)
        q_ref[:, rows[p]] = (q * q_scale).T.astype(BF16)
        k_ref[rows[p], :] = k.astype(BF16)
        lane = lax.broadcasted_iota(jnp.int32, (1, k_ref.shape[-1]), 1)
        ones_col = jnp.where((lane & (HEAD_PAD - 1)) == MLA_V, 1.0, 0.0)
        v_ref[:, rows[p]] = (st[p].pop("v") + ones_col).T.astype(BF16)

    project_in(0)
    for p in range(PROJ_SUBTILES):
        if p + 1 < PROJ_SUBTILES:
            project_in(p + 1)
        latent_norms(p)
        conv(p)
        project_heads(p)
        if p:
            rotary_store(p - 1)
    rotary_store(PROJ_SUBTILES - 1)
    carry_ref[...] = st[PROJ_SUBTILES - 1]["tail"]


def _hy_weights(w_uq, w_ukv):
    hp = HEAD_PAD
    pad_heads = lambda w: jnp.pad(w, ((0, 0), (0, 0), (0, hp - w.shape[-1]))).reshape(w.shape[0], -1)
    wq = pad_heads(w_uq.reshape(Q_LORA, MLA_HEADS, MLA_NOPE + MLA_ROPE))
    wkv = w_ukv.reshape(KV_LORA, MLA_HEADS, MLA_NOPE + MLA_V)
    wk = pad_heads(wkv[..., :MLA_NOPE])
    wv = pad_heads(wkv[..., MLA_NOPE:])
    return wq.astype(BF16), wk.astype(BF16), wv.astype(BF16)


def _rope_tables(positions):
    half = MLA_ROPE // 2
    inv = ROPE_THETA ** (-jnp.arange(half, dtype=F32) / half)
    ang = positions.astype(F32)[..., None] * inv
    cos, sin = jnp.cos(ang), jnp.sin(ang)
    lead = positions.shape
    ones = jnp.ones(lead + (MLA_NOPE,), F32)
    ztail = jnp.zeros(lead + (HEAD_PAD - MLA_NOPE - MLA_ROPE,), F32)
    zhead = jnp.zeros(lead + (MLA_NOPE,), F32)
    rc = jnp.concatenate([ones, cos, cos, ztail], axis=-1)
    rs = jnp.concatenate([zhead, -sin, sin, ztail], axis=-1)
    return rc, rs


def _hy_proj(x, mod_l, g, w_in, conv_w, q_norm, kv_norm, hy_w, ropes):
    b, s, d = x.shape
    tm = min(HY_ROW_TILE, s)
    hp = HEAD_PAD
    qw = MLA_HEADS * hp
    row = lambda width: pl.BlockSpec((None, tm, width), lambda i, j: (i, j, 0))
    col = lambda height: pl.BlockSpec((None, height, tm), lambda i, j: (i, 0, j))
    q_scale = (MLA_NOPE + MLA_ROPE) ** -0.5 * LOG2E
    return pl.pallas_call(
        functools.partial(_hy_proj_kernel, q_scale=q_scale),
        out_shape=(
            jax.ShapeDtypeStruct((b, s, CONV_WIDTH), BF16),
            jax.ShapeDtypeStruct((b, qw, s), BF16),
            jax.ShapeDtypeStruct((b, s, qw), BF16),
            jax.ShapeDtypeStruct((b, qw, s), BF16),
        ),
        grid=(b, s // tm),
        in_specs=[
            row(d),
            pl.BlockSpec((None, N_ADA, d), lambda i, j: (i, 0, 0)),
            _const_spec((1, d)),
            _const_spec(w_in.shape),
            _const_spec(conv_w.shape),
            _const_spec((1, Q_LORA)),
            _const_spec((1, KV_LORA)),
        ] + [_const_spec(w.shape) for w in hy_w] + [row(hp), row(hp)],
        out_specs=(row(CONV_WIDTH), col(qw), row(qw), col(qw)),
        scratch_shapes=[pltpu.VMEM((8, CONV_WIDTH), F32)],
        compiler_params=_cparams(("arbitrary", "arbitrary")),
        name="hy_proj",
    )(x, mod_l, g.reshape(1, d), w_in, conv_w, q_norm.reshape(1, -1), kv_norm.reshape(1, -1),
      *hy_w, *ropes)


def _mla_attn_body(qt_ref, k_ref, vt_ref, o_ref, n_main):
    tq = qt_ref.shape[1]
    hp = HEAD_PAD
    n_heads = qt_ref.shape[0] // hp
    n_chunks = n_main // tq + 1
    rel = (lax.broadcasted_iota(jnp.int32, (tq, 1), 0)
           <= lax.broadcasted_iota(jnp.int32, (1, tq), 1))
    diag_bias = jnp.where(rel, 0.0, -MASK_BIG)
    heads = [slice(hh * hp, (hh + 1) * hp) for hh in range(n_heads)]
    st = [{} for _ in range(n_heads)]

    def scores(hh, c):
        keys = slice(c * tq, (c + 1) * tq)
        s = _dot(k_ref[keys, heads[hh]], qt_ref[heads[hh], :])
        st[hh]["s", c] = s + diag_bias if c == n_chunks - 1 else s

    def softmax(hh, c):
        s = st[hh].pop(("s", c))
        m = jnp.max(s, axis=0, keepdims=True)
        st[hh]["m", c] = m
        st[hh]["p", c] = jnp.exp2(s - m).astype(BF16)

    def values(hh, c):
        keys = slice(c * tq, (c + 1) * tq)
        st[hh]["acc", c] = _dot(vt_ref[heads[hh], keys], st[hh].pop(("p", c)))

    units = [(hh, c) for c in range(n_chunks) for hh in range(n_heads)]
    for i in range(len(units) + 2):
        if i < len(units):
            scores(*units[i])
        if 0 <= i - 1 < len(units):
            softmax(*units[i - 1])
        if 0 <= i - 2 < len(units):
            values(*units[i - 2])

    outs = []
    for hh in range(n_heads):
        m_all = st[hh]["m", 0]
        for c in range(1, n_chunks):
            m_all = jnp.maximum(m_all, st[hh]["m", c])
        acc = None
        for c in range(n_chunks):
            part = st[hh].pop(("acc", c))
            if n_chunks > 1:
                part = part * jnp.exp2(st[hh].pop(("m", c)) - m_all)
            acc = part if acc is None else acc + part
        out = acc * _row_recip(acc[MLA_V:MLA_V + 1, :])
        outs.append(out[0:MLA_V, :])
    o_ref[...] = jnp.concatenate(outs, axis=0).T.astype(BF16)


def _mla_attn_kernel(qt_ref, k_ref, vt_ref, o_ref):
    tq = qt_ref.shape[1]
    qi = pl.program_id(2)
    for j in range(k_ref.shape[0] // tq):
        pl.when(qi == j)(functools.partial(_mla_attn_body, qt_ref, k_ref, vt_ref, o_ref, j * tq))


def _mla_attn(qt, k, vt):
    b, s, _ = k.shape
    tq = min(MLA_TQ, s)
    qw = MLA_HEADS_PER_STEP * HEAD_PAD
    ow = MLA_HEADS_PER_STEP * MLA_V
    return pl.pallas_call(
        _mla_attn_kernel,
        out_shape=jax.ShapeDtypeStruct((b, s, MLA_HEADS * MLA_V), BF16),
        grid=(b, MLA_HEADS // MLA_HEADS_PER_STEP, s // tq),
        in_specs=[
            pl.BlockSpec((None, qw, tq), lambda i, h, j: (i, h, j)),
            pl.BlockSpec((None, s, qw), lambda i, h, j: (i, 0, h)),
            pl.BlockSpec((None, qw, s), lambda i, h, j: (i, h, 0)),
        ],
        out_specs=pl.BlockSpec((None, tq, ow), lambda i, h, j: (i, j, h)),
        compiler_params=_cparams(("arbitrary", "arbitrary", "arbitrary")),
        name="mla_attn",
    )(qt, k, vt)


def _nsa_proj_kernel(x_ref, mod_ref, g_ref, win_ref, gb_ref, qt_ref, kvc_ref, ks_ref, vst_ref,
                     kw_ref, vw_ref, gt_ref, *, q_scale):
    tm = x_ref.shape[0]
    hp, dk = HEAD_PAD, NSA_DK
    pad = hp - dk
    qd = NSA_HEADS * dk
    kw = NSA_KV_W
    sub = tm // NSA_PROJ_SUBTILES
    zs = {}

    def project_in(p):
        h = _modulate(x_ref[p * sub:(p + 1) * sub, :], g_ref[...], mod_ref, 3).astype(BF16)
        zs[p] = _dot(h, win_ref[...])

    def emit(p):
        z = zs.pop(p)
        at = slice(p * sub, (p + 1) * sub)
        zeros = jnp.zeros((sub, pad), BF16)
        zero_rows = jnp.zeros((pad, sub), BF16)
        for pair in range(NSA_HEADS // 2):
            qt = (z[:, pair * hp:(pair + 1) * hp] * q_scale).T.astype(BF16)
            for half in range(2):
                hd = 2 * pair + half
                qt_ref[hd * hp:hd * hp + dk, at] = qt[half * dk:(half + 1) * dk, :]
                qt_ref[hd * hp + dk:(hd + 1) * hp, at] = zero_rows
        kvc_ref[at, :] = z[:, qd:qd + kw].astype(BF16)
        pos = pl.program_id(1) * tm + p * sub + lax.broadcasted_iota(jnp.int32, (sub, 1), 0)
        lane = lax.broadcasted_iota(jnp.int32, (1, pad), 1)
        blk_onehot = jnp.where(lax.shift_right_logical(pos, int(np.log2(SLC_BLOCK))) == lane, 1.0, 0.0)
        blk_onehot = blk_onehot.astype(BF16)
        ones_col = jnp.broadcast_to(jnp.where(lane == 0, 1.0, 0.0), (sub, pad)).astype(BF16)
        kvs = z[:, qd + kw:qd + 2 * kw].astype(BF16)
        kvw = z[:, qd + 2 * kw:qd + 3 * kw].astype(BF16)
        for g in range(NSA_KV_HEADS):
            lo, mid, hi = g * hp, g * hp + dk, (g + 1) * hp
            kc = slice(g * dk, (g + 1) * dk)
            vc = slice((NSA_KV_HEADS + g) * dk, (NSA_KV_HEADS + g + 1) * dk)
            ks_ref[at, lo:mid] = kvs[:, kc]
            ks_ref[at, mid:hi] = blk_onehot
            kw_ref[at, lo:mid] = kvw[:, kc]
            kw_ref[at, mid:hi] = zeros
            vw_ref[at, lo:mid] = kvw[:, vc]
            vw_ref[at, mid:hi] = ones_col
        v_off = qd + kw + NSA_KV_HEADS * dk
        vt = z[:, v_off:v_off + NSA_KV_HEADS * dk].T.astype(BF16)
        ones_row = jnp.where(lax.broadcasted_iota(jnp.int32, (pad, 1), 0) == 0, 1.0, 0.0)
        ones_row = jnp.broadcast_to(ones_row, (pad, sub)).astype(BF16)
        for g in range(NSA_KV_HEADS):
            vst_ref[g * hp:g * hp + dk, at] = vt[g * dk:(g + 1) * dk, :]
            vst_ref[g * hp + dk:(g + 1) * hp, at] = ones_row
        gt_ref[:, at] = _sigmoid(z[:, qd + 3 * kw:] + gb_ref[...]).T

    project_in(0)
    for p in range(NSA_PROJ_SUBTILES):
        if p + 1 < NSA_PROJ_SUBTILES:
            project_in(p + 1)
        emit(p)


def _nsa_proj(x, mod_l, g, w_in, gate_b):
    b, s, d = x.shape
    tm = min(ROW_TILE, s)
    assert s // SLC_BLOCK <= HEAD_PAD - NSA_DK
    qw = NSA_HEADS * HEAD_PAD
    kvw = NSA_KV_HEADS * HEAD_PAD
    ng = gate_b.shape[-1]
    row = lambda width: pl.BlockSpec((None, tm, width), lambda i, j: (i, j, 0))
    col = lambda height: pl.BlockSpec((None, height, tm), lambda i, j: (i, 0, j))
    sds = lambda width, dt=BF16: jax.ShapeDtypeStruct((b, s, width), dt)
    sds_t = lambda height, dt=BF16: jax.ShapeDtypeStruct((b, height, s), dt)
    return pl.pallas_call(
        functools.partial(_nsa_proj_kernel, q_scale=NSA_DK ** -0.5 * LOG2E),
        out_shape=(sds_t(qw), sds(NSA_KV_W), sds(kvw), sds_t(kvw), sds(kvw), sds(kvw), sds_t(ng, F32)),
        grid=(b, s // tm),
        in_specs=[
            row(d),
            pl.BlockSpec((None, N_ADA, d), lambda i, j: (i, 0, 0)),
            _const_spec((1, d)),
            _const_spec(w_in.shape),
            _const_spec((1, ng)),
        ],
        out_specs=(col(qw), row(NSA_KV_W), row(kvw), col(kvw), row(kvw), row(kvw), col(ng)),
        compiler_params=_cparams(("arbitrary", "arbitrary")),
        name="nsa_proj",
    )(x, mod_l, g.reshape(1, d), w_in, gate_b.reshape(1, ng))


def _nsa_cmp_kernel(ch_ref, w1a_ref, w1b_ref, pea_ref, peb_ref, w2_ref, k_ref, vt_ref):
    ch = ch_ref[...]
    n = ch.shape[0]
    first = _dot(ch, w1a_ref[...])
    second = _dot(ch, w1b_ref[...])
    bias = _dot(pea_ref[...], w1a_ref[...]) + _dot(peb_ref[...], w1b_ref[...])
    hid = first + pltpu.roll(second, n - 1, 0) + bias[0:1, :]
    act = (hid * _sigmoid(hid)).astype(BF16)
    kv = _dot(act, w2_ref[...])
    half = kv.shape[1] // 2
    k_ref[...] = kv[:, :half].astype(BF16)
    vt_ref[...] = kv[:, half:].T.astype(BF16)


def _nsa_cmp_weights(cmp_pe, cmp_w1, cmp_w2):
    ncomp = 2 * NSA_KV_HEADS
    half = CMP_BLOCK // 2
    kv_of = np.arange(ncomp) // NSA_KV_HEADS
    eye = jnp.eye(ncomp, dtype=BF16)
    w1 = cmp_w1.astype(BF16).reshape(2, 2, half, NSA_DK, CMP_HID)[kv_of]
    big1 = jnp.einsum("chldj,ce->hlcdej", w1, eye).reshape(2, half * ncomp * NSA_DK, ncomp * CMP_HID)
    w2 = jnp.pad(cmp_w2.astype(BF16)[kv_of], ((0, 0), (0, 0), (0, HEAD_PAD - NSA_DK)))
    big2 = jnp.einsum("cjd,ce->cjed", w2, eye).reshape(ncomp * CMP_HID, ncomp * HEAD_PAD)
    pe = cmp_pe.astype(BF16).reshape(2, 2, half, NSA_DK)[kv_of]
    pe = jnp.transpose(pe, (1, 2, 0, 3)).reshape(2, 1, half * ncomp * NSA_DK)
    pe = jnp.broadcast_to(pe, (2, 8, half * ncomp * NSA_DK))
    return big1[0], big1[1], pe[0], pe[1], big2


def _nsa_cmp(kvc, w1a, w1b, pea, peb, w2):
    b, s, kw = kvc.shape
    nchunk = s // CMP_STRIDE
    ow = w2.shape[1]
    chunks = kvc.reshape(b, nchunk, CMP_STRIDE * kw)
    half = ow // 2
    return pl.pallas_call(
        _nsa_cmp_kernel,
        out_shape=(jax.ShapeDtypeStruct((b, nchunk, half), BF16),
                   jax.ShapeDtypeStruct((b, half, nchunk), BF16)),
        grid=(b,),
        in_specs=[
            pl.BlockSpec((None, nchunk, CMP_STRIDE * kw), lambda i: (i, 0, 0)),
            _const_spec(w1a.shape), _const_spec(w1b.shape),
            _const_spec(pea.shape), _const_spec(peb.shape), _const_spec(w2.shape),
        ],
        out_specs=(pl.BlockSpec((None, nchunk, half), lambda i: (i, 0, 0)),
                   pl.BlockSpec((None, half, nchunk), lambda i: (i, 0, 0))),
        compiler_params=_cparams(("arbitrary",)),
        name="nsa_cmp",
    )(chunks, w1a, w1b, pea, peb, w2)


def _add_per_head(s, bias, rep):
    tq = bias.shape[1]
    return jnp.concatenate([s[:, r * tq:(r + 1) * tq] + bias for r in range(rep)], axis=1)


def _nsa_attn_kernel(qt_ref, kc_ref, vct_ref, ks_ref, vst_ref, kw_ref, vw_ref, gt_ref, aggt_ref,
                     placet_ref, o_ref, *, tk, n_sel):
    tq = qt_ref.shape[1]
    per = tk // tq
    qi = pl.program_id(1)
    for j in range(ks_ref.shape[0] // tk):
        in_class = (qi >= j * per) & (qi < (j + 1) * per)
        pl.when(in_class)(functools.partial(
            _nsa_attn_body, qt_ref, kc_ref, vct_ref, ks_ref, vst_ref, kw_ref, vw_ref, gt_ref,
            aggt_ref, placet_ref, o_ref, tk=tk, n_sel=n_sel, n_main=j * tk))


def _nsa_attn_body(qt_ref, kc_ref, vct_ref, ks_ref, vst_ref, kw_ref, vw_ref, gt_ref, aggt_ref,
                   placet_ref, o_ref, *, tk, n_sel, n_main):
    tq = qt_ref.shape[1]
    seq = ks_ref.shape[0]
    ncp = kc_ref.shape[0]
    ns = aggt_ref.shape[0]
    hp, dk, rep = HEAD_PAD, NSA_DK, NSA_GROUP
    q0 = pl.program_id(1) * tq
    t = q0 + lax.broadcasted_iota(jnp.int32, (1, tq), 1)
    t_heads = jnp.concatenate([t] * rep, axis=1)

    blk = lax.broadcasted_iota(jnp.int32, (ns, 1), 0)
    cur = lax.shift_right_logical(t, int(np.log2(SLC_BLOCK)))
    forced = (blk == 0) | (blk == cur) | (blk == cur - 1)
    causal_blk = blk * SLC_BLOCK <= t
    cmp_end = lax.broadcasted_iota(jnp.int32, (ncp, 1), 0) * CMP_STRIDE + (CMP_BLOCK - 1)
    cmp_mask = (cmp_end <= t_heads) & (cmp_end < seq)
    win_len = WINDOW + tq
    w_start = pl.multiple_of(jnp.maximum(q0 - WINDOW, 0), tq)
    wpos = w_start + lax.broadcasted_iota(jnp.int32, (win_len, 1), 0)
    win_bias = jnp.where((wpos <= t) & (wpos > t - WINDOW), 0.0, -MASK_BIG)
    last_pos = n_main + lax.broadcasted_iota(jnp.int32, (tk, 1), 0)
    last_bias = jnp.where(last_pos <= t, 0.0, -MASK_BIG)
    pad_row = lax.broadcasted_iota(jnp.int32, (hp, 1), 0)
    sel_offset = jnp.where((pad_row >= dk) & (pad_row < dk + ns), MASK_BIG, 0.0)
    gates = gt_ref[...]

    n_groups = NSA_KV_HEADS
    kv_cols = [slice(g * hp, (g + 1) * hp) for g in range(n_groups)]
    q_heads = [[qt_ref[(g * rep + r) * hp:(g * rep + r + 1) * hp, :] for r in range(rep)]
               for g in range(n_groups)]
    qs = [jnp.concatenate(q_heads[g], axis=1) for g in range(n_groups)]
    st = [{} for _ in range(n_groups)]

    def cmp_scores(g):
        st[g]["s_c"] = jnp.where(cmp_mask, _dot(kc_ref[:, kv_cols[g]], qs[g]), NEG_INF)

    def cmp_softmax(g):
        s = st[g].pop("s_c")
        p = jnp.where(cmp_mask, jnp.exp2(s - jnp.max(s, axis=0, keepdims=True)), 0.0)
        st[g]["p_c"] = p * _row_recip(jnp.sum(p, axis=0, keepdims=True))

    def cmp_values(g):
        p_c = st[g].pop("p_c")
        st[g]["o_c"] = _dot(vct_ref[kv_cols[g], :], p_c.astype(BF16))
        p_sum = p_c[:, 0:tq]
        for r in range(1, rep):
            p_sum = p_sum + p_c[:, r * tq:(r + 1) * tq]
        agg_t = aggt_ref[...]
        hi, mid, lo = _split3(p_sum)
        st[g]["imp"] = _dot(agg_t, hi) + _dot(agg_t, mid) + _dot(agg_t, lo)

    def select(g):
        val = jnp.where(forced, FORCE_SCORE, jnp.where(causal_blk, st[g].pop("imp"), NEG_INF))
        rank = jnp.zeros((ns, tq), F32)
        for i in range(ns):
            other = val[i:i + 1, :]
            beats = (other > val) | ((other == val) & (blk > i))
            rank = rank + jnp.where(beats, 1.0, 0.0)
        sel = jnp.where((rank < n_sel) & causal_blk, 1.0, 0.0).astype(BF16)
        sel_pad = _dot(placet_ref[...], sel) - sel_offset
        st[g]["q_sel"] = jnp.concatenate(
            [(qh.astype(F32) + sel_pad).astype(BF16) for qh in q_heads[g]], axis=1)

    def win_scores(g):
        s = _dot(kw_ref[pl.ds(w_start, win_len), kv_cols[g]], qs[g])
        st[g]["s_w"] = _add_per_head(s, win_bias, rep)

    def win_softmax(g):
        s = st[g].pop("s_w")
        st[g]["p_w"] = jnp.exp2(s - jnp.max(s, axis=0, keepdims=True)).astype(BF16)

    def win_values(g):
        vw_t = vw_ref[pl.ds(w_start, win_len), kv_cols[g]].astype(F32).T.astype(BF16)
        st[g]["acc_w"] = _dot(vw_t, st[g].pop("p_w"))

    n_sel_chunks = n_main // tk + 1

    def sel_scores(g, c):
        keys = slice(c * tk, (c + 1) * tk)
        s = _dot(ks_ref[keys, kv_cols[g]], st[g]["q_sel"])
        st[g]["s_s", c] = _add_per_head(s, last_bias, rep) if c == n_sel_chunks - 1 else s

    def sel_softmax(g, c):
        s = st[g].pop(("s_s", c))
        m = jnp.max(s, axis=0, keepdims=True)
        st[g]["m_s", c] = m
        st[g]["p_s", c] = jnp.exp2(s - m).astype(BF16)

    def sel_values(g, c):
        keys = slice(c * tk, (c + 1) * tk)
        st[g]["acc_s", c] = _dot(vst_ref[kv_cols[g], keys], st[g].pop(("p_s", c)))

    def merge(g):
        o_c, acc_w = st[g].pop("o_c"), st[g].pop("acc_w")
        m_all = st[g]["m_s", 0]
        for c in range(1, n_sel_chunks):
            m_all = jnp.maximum(m_all, st[g]["m_s", c])
        acc_s = None
        for c in range(n_sel_chunks):
            part = st[g].pop(("acc_s", c))
            if n_sel_chunks > 1:
                part = part * jnp.exp2(st[g].pop(("m_s", c)) - m_all)
            acc_s = part if acc_s is None else acc_s + part
        r_s = _row_recip(acc_s[dk:dk + 1, :])
        r_w = _row_recip(acc_w[dk:dk + 1, :])
        out = []
        for r in range(rep):
            cols = slice(r * tq, (r + 1) * tq)
            gc = g * 3 * rep + r
            o_h = (gates[gc:gc + 1, :] * o_c[:, cols]
                   + (gates[gc + rep:gc + rep + 1, :] * r_s[:, cols]) * acc_s[:, cols]
                   + (gates[gc + 2 * rep:gc + 2 * rep + 1, :] * r_w[:, cols]) * acc_w[:, cols])
            out.append(o_h[0:dk, :])
        return out

    groups = range(n_groups)
    units = [(cmp_scores, cmp_softmax, cmp_values, (g,)) for g in groups]
    units += [(win_scores, win_softmax, win_values, (g,)) for g in groups]
    units += [(sel_scores, sel_softmax, sel_values, (g, c))
              for c in range(n_sel_chunks) for g in groups]
    for i in range(len(units) + 2):
        if i < len(units):
            units[i][0](*units[i][3])
        if 0 <= i - 1 < len(units):
            units[i - 1][1](*units[i - 1][3])
        if 0 <= i - 2 < len(units):
            units[i - 2][2](*units[i - 2][3])
            if units[i - 2][2] is cmp_values:
                select(*units[i - 2][3])
    head_out = [piece for g in groups for piece in merge(g)]
    o_ref[...] = jnp.concatenate(head_out, axis=0).T.astype(BF16)


def _nsa_tables(seq):
    nchunk = seq // CMP_STRIDE
    ns = seq // SLC_BLOCK
    cs = np.arange(nchunk)[:, None] * CMP_STRIDE
    ss = np.arange(ns)[None, :] * SLC_BLOCK
    overlap = np.clip(np.minimum(cs + CMP_BLOCK, ss + SLC_BLOCK) - np.maximum(cs, ss), 0, None)
    agg_t = (overlap.astype(np.float32) / CMP_BLOCK).T
    place_t = np.zeros((HEAD_PAD, ns), np.float32)
    place_t[NSA_DK + np.arange(ns), np.arange(ns)] = MASK_BIG
    return jnp.asarray(agg_t, BF16), jnp.asarray(place_t, BF16)


def _nsa_attn(qt, kc, vct, ks, vst, kw, vw, gt):
    b, _, s = qt.shape
    tq = min(NSA_TQ, s)
    tk = min(NSA_TK, s)
    assert s >= WINDOW + tq and s % tk == 0 and WINDOW % tq == 0
    agg_t, place_t = _nsa_tables(s)
    whole = lambda a: pl.BlockSpec((None,) + a.shape[1:], lambda i, j: (i, 0, 0))
    col = lambda a: pl.BlockSpec((None, a.shape[1], tq), lambda i, j: (i, 0, j))
    ow = NSA_HEADS * NSA_DK
    return pl.pallas_call(
        functools.partial(_nsa_attn_kernel, tk=tk, n_sel=min(N_SEL, s // SLC_BLOCK)),
        out_shape=jax.ShapeDtypeStruct((b, s, ow), BF16),
        grid=(b, s // tq),
        in_specs=[
            col(qt), whole(kc), whole(vct), whole(ks), whole(vst), whole(kw), whole(vw), col(gt),
            _const_spec(agg_t.shape), _const_spec(place_t.shape),
        ],
        out_specs=pl.BlockSpec((None, tq, ow), lambda i, j: (i, j, 0)),
        compiler_params=_cparams(("arbitrary", "arbitrary")),
        name="nsa_attn",
    )(qt, kc, vct, ks, vst, kw, vw, gt, agg_t, place_t)


def _nsa_gate_layout(w_in, gate_b):
    qd = NSA_HEADS * NSA_DK
    o = qd + 3 * NSA_KV_W
    perm = np.arange(3 * NSA_HEADS).reshape(NSA_KV_HEADS, NSA_GROUP, 3).transpose(0, 2, 1).reshape(-1)
    extra = HEAD_PAD - perm.size
    w_in = jnp.concatenate([w_in[:, :o], jnp.pad(w_in[:, o:][:, perm], ((0, 0), (0, extra)))], axis=1)
    return w_in, jnp.pad(gate_b[perm], (0, extra))


def kernel(x, c, positions, ada_w, ada_b, norm_g, final_g, ff_w13, ff_w2, hy_w_in, hy_conv_w,
           hy_q_norm, hy_kv_norm, hy_w_uq, hy_w_ukv, hy_w_out, nsa_w_in, nsa_cmp_pe, nsa_cmp_w1,
           nsa_cmp_w2, nsa_gate_b, nsa_w_out):
    depth = ada_w.shape[0]
    mod = _ada_mod(c, ada_w, ada_b)
    ropes = _rope_tables(positions)
    w13 = ff_w13.astype(BF16)
    w2 = ff_w2.astype(BF16)
    for l in range(depth):
        m = l // 2
        x = _ffn(x, mod[l], norm_g[l, 0], w13, w2, l, 0, final_g, k0=0, final=False)
        if l % 2 == 0:
            w_in = hy_w_in[m].astype(BF16)
            w_in = jnp.pad(w_in, ((0, 0), (0, -w_in.shape[1] % HEAD_PAD)))
            y_conv, q, k, v = _hy_proj(x, mod[l], norm_g[l, 1], w_in,
                                       hy_conv_w[m], hy_q_norm[m], hy_kv_norm[m],
                                       _hy_weights(hy_w_uq[m], hy_w_ukv[m]), ropes)
            y_att = _mla_attn(q, k, v)
            w_out = hy_w_out[m].astype(BF16)
            parts, w_parts = [y_conv, y_att], [w_out[:CONV_WIDTH], w_out[CONV_WIDTH:]]
        else:
            w_in, gate_b = _nsa_gate_layout(nsa_w_in[m].astype(BF16), nsa_gate_b[m])
            qt, kvc, ks, vst, kw, vw, gt = _nsa_proj(x, mod[l], norm_g[l, 1], w_in, gate_b)
            kc, vct = _nsa_cmp(kvc, *_nsa_cmp_weights(nsa_cmp_pe[m], nsa_cmp_w1[m], nsa_cmp_w2[m]))
            o = _nsa_attn(qt, kc, vct, ks, vst, kw, vw, gt)
            parts, w_parts = [o], [nsa_w_out[m].astype(BF16)]
        x = _ffn(x, mod[l], norm_g[l, 2], w13, w2, l, 1, final_g, k0=6, final=(l == depth - 1),
                 parts=parts, w_parts=w_parts)
    return x
```
